```python
import math
import jax
import jax.numpy as jnp
from jax import lax
import numpy as np

D_MODEL = 1024
BATCH = 4
SEQ = 8192
DEPTH = 2

N_META = 16
RMS_EPS = 1e-6
GN_EPS = 64e-5

RWKV_HEADS = 12
RWKV_HEAD_DIM = 64
RWKV_DIM = RWKV_HEADS * RWKV_HEAD_DIM
DECAY_LORA = 64
AAA_LORA = 64
GATE_LORA = 128
RWKV_COLS = 3 * RWKV_DIM + DECAY_LORA + AAA_LORA + GATE_LORA

S5_DIM = D_MODEL - RWKV_DIM
S5_GROUP = 16
S5_GROUPS = S5_DIM // S5_GROUP
S5_STATE = 64
EVEN_IN = RWKV_COLS + S5_DIM

FOX_HEADS = 16
FOX_HEAD_DIM = D_MODEL // FOX_HEADS
ODD_IN = 3 * D_MODEL + FOX_HEADS
Q_BLOCK = 128

D_FF = 2816
CONV_WIDTH = 3

N_EVEN = (DEPTH + 1) // 2
N_ODD = DEPTH // 2

kernel_name = 'hybrid_rwkv7_s5_fox_convffn'


def rms_norm(x, gain):
    xf = x.astype(jnp.float32)
    y = xf * lax.rsqrt(jnp.mean(xf * xf, axis=-1, keepdims=True) + RMS_EPS)
    return (y * gain.astype(jnp.float32)).astype(x.dtype)


def token_shift(p, mu):
    prev = jnp.pad(p, ((0, 0), (1, 0), (0, 0)))[:, :-1]
    return p + (prev - p) * mu


def rwkv7_step(state, inp):
    r_t, w_t, k_t, v_t, kk_t, ka_t = inp
    sa = jnp.einsum('bhvk,bhk->bhv', state, -kk_t)
    state = (state * w_t[:, :, None, :] + sa[..., None] * ka_t[:, :, None, :]
             + v_t[..., None] * k_t[:, :, None, :])
    return state, jnp.einsum('bhvk,bhk->bhv', state, r_t)


def rwkv7_mix(p, mu, w0, w2, a0, a2, g2, k_k, k_a, r_k, ln_w, ln_b):
    bsz, length, _ = p.shape
    hd = (bsz, length, RWKV_HEADS, RWKV_HEAD_DIM)
    p = token_shift(p.astype(jnp.float32), mu)
    splits = [RWKV_DIM, 2 * RWKV_DIM, 3 * RWKV_DIM, 3 * RWKV_DIM + DECAY_LORA,
              3 * RWKV_DIM + DECAY_LORA + AAA_LORA]
    r, k, v, wd, ad, gd = jnp.split(p, splits, axis=-1)
    w = -jax.nn.softplus(-(w0 + jnp.tanh(wd) @ w2)) - 0.5
    decay = jnp.exp(-jnp.exp(w))
    a = jax.nn.sigmoid(a0 + ad @ a2)
    g = jax.nn.sigmoid(gd) @ g2
    kk = (k * k_k).reshape(hd)
    kk = kk / jnp.maximum(jnp.sqrt(jnp.sum(kk * kk, axis=-1, keepdims=True)), 1e-12)
    k = k * (1.0 + (a - 1.0) * k_a)
    r, k, v, decay, a = [t.reshape(hd) for t in (r, k, v, decay, a)]
    xs = tuple(jnp.moveaxis(t, 1, 0) for t in (r, decay, k, v, kk, kk * a))
    state0 = jnp.zeros((bsz, RWKV_HEADS, RWKV_HEAD_DIM, RWKV_HEAD_DIM), jnp.float32)
    _, y = lax.scan(rwkv7_step, state0, xs)
    y = jnp.moveaxis(y, 0, 1)
    mean = jnp.mean(y, axis=-1, keepdims=True)
    var = jnp.mean((y - mean) ** 2, axis=-1, keepdims=True)
    y = ((y - mean) * lax.rsqrt(var + GN_EPS) * ln_w.reshape(RWKV_HEADS, RWKV_HEAD_DIM)
         + ln_b.reshape(RWKV_HEADS, RWKV_HEAD_DIM))
    bonus = jnp.sum(r * k * r_k, axis=-1, keepdims=True) * v
    return (y + bonus).reshape(bsz, length, RWKV_DIM) * g


def complex_linear_combine(e1, e2):
    ar1, ai1, br1, bi1 = e1
    ar2, ai2, br2, bi2 = e2
    return (ar2 * ar1 - ai2 * ai1,
            ar2 * ai1 + ai2 * ar1,
            ar2 * br1 - ai2 * bi1 + br2,
            ar2 * bi1 + ai2 * br1 + bi2)


def s5_mix(u, a_re, a_im, log_dt, b_re, b_im, c_re, c_im, d_skip, glu_w, glu_b):
    bsz, length, _ = u.shape
    f32 = jnp.float32
    uf = u.astype(f32)
    ug = uf.reshape(bsz, length, S5_GROUPS, S5_GROUP)
    a_re = a_re.astype(f32)
    a_im = a_im.astype(f32)
    dt = jnp.exp(log_dt.astype(f32))[:, None]
    mag = jnp.exp(a_re * dt)
    lam_re = mag * jnp.cos(a_im * dt)
    lam_im = mag * jnp.sin(a_im * dt)
    den = a_re * a_re + a_im * a_im
    z_re = ((lam_re - 1.0) * a_re + lam_im * a_im) / den
    z_im = (lam_im * a_re - (lam_re - 1.0) * a_im) / den
    bb_re = z_re[..., None] * b_re - z_im[..., None] * b_im
    bb_im = z_re[..., None] * b_im + z_im[..., None] * b_re
    bu_re = jnp.einsum('gnc,blgc->blgn', bb_re, ug)
    bu_im = jnp.einsum('gnc,blgc->blgn', bb_im, ug)
    shape_a = (1, length, S5_GROUPS, S5_STATE)
    elems = (jnp.broadcast_to(lam_re, shape_a), jnp.broadcast_to(lam_im, shape_a), bu_re, bu_im)
    _, _, h_re, h_im = lax.associative_scan(complex_linear_combine, elems, axis=1)
    y = (jnp.einsum('gcn,blgn->blgc', c_re, h_re)
         - jnp.einsum('gcn,blgn->blgc', c_im, h_im)).reshape(bsz, length, S5_DIM)
    y = jax.nn.gelu(y + d_skip * uf)
    return y * jax.nn.sigmoid(y @ glu_w + glu_b)


def fox_mix(p, b_f, q_gain, k_gain):
    bsz, length, _ = p.shape
    hd = (bsz, length, FOX_HEADS, FOX_HEAD_DIM)
    q, k, v, f_pre = jnp.split(p.astype(jnp.float32), [D_MODEL, 2 * D_MODEL, 3 * D_MODEL], axis=-1)
    q = rms_norm(q.reshape(hd), q_gain)
    k = rms_norm(k.reshape(hd), k_gain)
    v = v.reshape(hd)
    log_f = jax.nn.log_sigmoid(f_pre + b_f)
    cum = jnp.transpose(jnp.cumsum(log_f, axis=1), (0, 2, 1))
    q, k, v = [jnp.transpose(t, (0, 2, 1, 3)) for t in (q, k, v)]
    scale = FOX_HEAD_DIM ** -0.5
    n_blocks = (length - N_META) // Q_BLOCK
    bounds = [(0, N_META)] + [(N_META + i * Q_BLOCK, N_META + (i + 1) * Q_BLOCK)
                              for i in range(n_blocks)]
    outs = []
    for s0, s1 in bounds:
        logits = (jnp.einsum('bhqd,bhkd->bhqk', q[:, :, s0:s1], k[:, :, :s1]) * scale
                  + cum[:, :, s0:s1, None] - cum[:, :, None, :s1])
        causal = jnp.arange(s1)[None, :] <= jnp.arange(s0, s1)[:, None]
        probs = jax.nn.softmax(jnp.where(causal, logits, -jnp.inf), axis=-1)
        outs.append(jnp.einsum('bhqk,bhkd->bhqd', probs, v[:, :, :s1]))
    o = jnp.concatenate(outs, axis=2)
    return jnp.transpose(o, (0, 2, 1, 3)).reshape(bsz, length, D_MODEL)


def conv_gated_mlp(x, w_up, conv_w, conv_b, w_down):
    h = x @ w_up
    h = lax.conv_general_dilated(h, conv_w[:, None, :].astype(h.dtype), (1,),
                                 [(CONV_WIDTH - 1, 0)],
                                 dimension_numbers=('NWC', 'WIO', 'NWC'),
                                 feature_group_count=2 * D_FF) + conv_b
    gate, val = jnp.split(h, 2, axis=-1)
    return (jax.nn.silu(gate) * val) @ w_down


def setup_inputs(seed: int = 0) -> dict:
    key = jax.random.key(seed)
    ks = iter(jax.random.split(key, 40))
    f32 = jnp.float32

    def nrm(shape, scale):
        return scale * jax.random.normal(next(ks), shape, f32)

    n_lin = jnp.arange(RWKV_DIM, dtype=f32) / (RWKV_DIM - 1)
    w0_base = -6.5 + 5.0 * n_lin ** 0.85
    a_im_base = jnp.pi * jnp.arange(S5_STATE, dtype=f32)
    return {
        'x': nrm((BATCH, SEQ, D_MODEL), 1.0),
        'meta_tokens': nrm((N_META, D_MODEL), 1.0),
        'mix_norm': 1.0 + nrm((DEPTH, D_MODEL), 0.02),
        'ffn_norm': 1.0 + nrm((DEPTH, D_MODEL), 0.02),
        'even_w_in': nrm((N_EVEN, D_MODEL, EVEN_IN), D_MODEL ** -0.5),
        'even_w_out': nrm((N_EVEN, D_MODEL, D_MODEL), D_MODEL ** -0.5),
        'rwkv_mu': jax.random.uniform(next(ks), (N_EVEN, RWKV_COLS), f32),
        'rwkv_w0': w0_base[None] + nrm((N_EVEN, RWKV_DIM), 0.1),
        'rwkv_w2': nrm((N_EVEN, DECAY_LORA, RWKV_DIM), 0.1 * DECAY_LORA ** -0.5),
        'rwkv_a0': nrm((N_EVEN, RWKV_DIM), 0.1),
        'rwkv_a2': nrm((N_EVEN, AAA_LORA, RWKV_DIM), AAA_LORA ** -0.5),
        'rwkv_g2': nrm((N_EVEN, GATE_LORA, RWKV_DIM), GATE_LORA ** -0.5),
        'rwkv_k_k': 0.85 + nrm((N_EVEN, RWKV_DIM), 0.02),
        'rwkv_k_a': 1.0 + nrm((N_EVEN, RWKV_DIM), 0.02),
        'rwkv_r_k': nrm((N_EVEN, RWKV_HEADS, RWKV_HEAD_DIM), 0.1),
        'rwkv_ln_w': 1.0 + nrm((N_EVEN, RWKV_DIM), 0.02),
        'rwkv_ln_b': nrm((N_EVEN, RWKV_DIM), 0.02),
        's5_a_re': -0.5 + nrm((N_EVEN, S5_GROUPS, S5_STATE), 0.01),
        's5_a_im': a_im_base[None, None] + nrm((N_EVEN, S5_GROUPS, S5_STATE), 0.01),
        's5_log_dt': jax.random.uniform(next(ks), (N_EVEN, S5_GROUPS), f32,
                                        minval=math.log(1e-3), maxval=math.log(1e-1)),
        's5_b_re': nrm((N_EVEN, S5_GROUPS, S5_STATE, S5_GROUP), (2 * S5_GROUP) ** -0.5),
        's5_b_im': nrm((N_EVEN, S5_GROUPS, S5_STATE, S5_GROUP), (2 * S5_GROUP) ** -0.5),
        's5_c_re': nrm((N_EVEN, S5_GROUPS, S5_GROUP, S5_STATE), S5_STATE ** -0.5),
        's5_c_im': nrm((N_EVEN, S5_GROUPS, S5_GROUP, S5_STATE), S5_STATE ** -0.5),
        's5_d': nrm((N_EVEN, S5_DIM), 1.0),
        's5_glu_w': nrm((N_EVEN, S5_DIM, S5_DIM), S5_DIM ** -0.5),
        's5_glu_b': nrm((N_EVEN, S5_DIM), 0.02),
        'odd_w_in': nrm((N_ODD, D_MODEL, ODD_IN), D_MODEL ** -0.5),
        'odd_w_out': nrm((N_ODD, D_MODEL, D_MODEL), D_MODEL ** -0.5),
        'fox_b_f': jnp.linspace(1.0, 6.0, FOX_HEADS, dtype=f32)[None] + nrm((N_ODD, FOX_HEADS), 0.1),
        'fox_q_gain': 1.0 + nrm((N_ODD, FOX_HEAD_DIM), 0.02),
        'fox_k_gain': 1.0 + nrm((N_ODD, FOX_HEAD_DIM), 0.02),
        'ffn_w_up': nrm((DEPTH, D_MODEL, 2 * D_FF), D_MODEL ** -0.5),
        'ffn_conv_w': nrm((DEPTH, CONV_WIDTH, 2 * D_FF), CONV_WIDTH ** -0.5),
        'ffn_conv_b': nrm((DEPTH, 2 * D_FF), 0.02),
        'ffn_w_down': nrm((DEPTH, D_FF, D_MODEL), D_FF ** -0.5),
    }


def reference(x, meta_tokens, mix_norm, ffn_norm, even_w_in, even_w_out, rwkv_mu, rwkv_w0,
              rwkv_w2, rwkv_a0, rwkv_a2, rwkv_g2, rwkv_k_k, rwkv_k_a, rwkv_r_k, rwkv_ln_w,
              rwkv_ln_b, s5_a_re, s5_a_im, s5_log_dt, s5_b_re, s5_b_im, s5_c_re, s5_c_im, s5_d,
              s5_glu_w, s5_glu_b, odd_w_in, odd_w_out, fox_b_f, fox_q_gain, fox_k_gain,
              ffn_w_up, ffn_conv_w, ffn_conv_b, ffn_w_down):
    bsz = x.shape[0]
    meta = jnp.broadcast_to(meta_tokens[None].astype(x.dtype), (bsz, N_META, D_MODEL))
    h = jnp.concatenate([meta, x], axis=1)
    for layer in range(DEPTH):
        i = layer // 2
        hn = rms_norm(h, mix_norm[layer])
        if layer % 2 == 0:
            p = hn @ even_w_in[i]
            a_out = rwkv7_mix(p[..., :RWKV_COLS], rwkv_mu[i], rwkv_w0[i], rwkv_w2[i],
                              rwkv_a0[i], rwkv_a2[i], rwkv_g2[i], rwkv_k_k[i], rwkv_k_a[i],
                              rwkv_r_k[i], rwkv_ln_w[i], rwkv_ln_b[i])
            b_out = s5_mix(p[..., RWKV_COLS:], s5_a_re[i], s5_a_im[i], s5_log_dt[i],
                           s5_b_re[i], s5_b_im[i], s5_c_re[i], s5_c_im[i], s5_d[i],
                           s5_glu_w[i], s5_glu_b[i])
            mix = jnp.concatenate([a_out, b_out], axis=-1) @ even_w_out[i]
        else:
            p = hn @ odd_w_in[i]
            mix = fox_mix(p, fox_b_f[i], fox_q_gain[i], fox_k_gain[i]) @ odd_w_out[i]
        h = h + mix.astype(h.dtype)
        ffn = conv_gated_mlp(rms_norm(h, ffn_norm[layer]), ffn_w_up[layer], ffn_conv_w[layer],
                             ffn_conv_b[layer], ffn_w_down[layer])
        h = h + ffn.astype(h.dtype)
    return h[:, N_META:]
```

```python
import functools
import math

import jax
import jax.numpy as jnp
import numpy as np
from jax import lax
from jax.experimental import pallas as pl
from jax.experimental.pallas import tpu as pltpu

F32 = jnp.float32
BF16 = jnp.bfloat16

D_MODEL = 1024
N_META = 16
RMS_EPS = 1e-6
GN_EPS = 64e-5

RWKV_HEADS = 12
HEAD_DIM = 64
RWKV_DIM = RWKV_HEADS * HEAD_DIM
LORA_COLS = 256
RWKV_COLS = 3 * RWKV_DIM + LORA_COLS
S5_DIM = 256
S5_GROUP = 16
S5_GROUPS = 16
S5_STATE = 64
S5_LANES = S5_GROUPS * S5_STATE
EVEN_IN = RWKV_COLS + S5_DIM
FOX_HEADS = 16
ODD_IN_PAD = 3 * D_MODEL + 128
D_FF = 2816
LANES = 128
PAIRS_RWKV = RWKV_DIM // LANES
PAIRS_FOX = D_MODEL // LANES
CHUNK = 64
S5_SUB = 128
FF_CHUNK = 256
VMEM_LIMIT = 56 * 1024 * 1024


def _mm(a, b):
    return jnp.dot(a.astype(BF16), b.astype(BF16), preferred_element_type=F32)


def _mm_nt(a, b):
    return lax.dot_general(a.astype(BF16), b.astype(BF16), (((1,), (1,)), ((), ())),
                           preferred_element_type=F32)


def _mm_tn(a, b):
    return lax.dot_general(a.astype(BF16), b.astype(BF16), (((0,), (0,)), ((), ())),
                           preferred_element_type=F32)


def _split2_mm(x, m):
    hi = x.astype(BF16)
    lo = (x - hi.astype(F32)).astype(BF16)
    return (jnp.dot(hi, m, preferred_element_type=F32)
            + jnp.dot(lo, m, preferred_element_type=F32))


def _cumsum_rows(tri, x):
    h1 = x.astype(BF16)
    r1 = x - h1.astype(F32)
    h2 = r1.astype(BF16)
    h3 = (r1 - h2.astype(F32)).astype(BF16)
    w = x.shape[1]
    res = jnp.dot(tri, jnp.concatenate([h1, h2, h3], axis=1), preferred_element_type=F32)
    return res[:, :w] + res[:, w:2 * w] + res[:, 2 * w:]


def _softplus(z):
    return jnp.maximum(z, 0.0) + jnp.log(1.0 + jnp.exp(-jnp.abs(z)))


def _sigmoid(z):
    return 1.0 / (1.0 + jnp.exp(-z))


def _rms_rows(x, gain):
    ms = jnp.mean(x * x, axis=-1, keepdims=True)
    return x * lax.rsqrt(ms + RMS_EPS) * gain


def _shift_rows(x, k, prev_rows):
    rolled = pltpu.roll(x, k, 0)
    row = lax.broadcasted_iota(jnp.int32, x.shape, 0)
    for i in range(k):
        rolled = jnp.where(row == i, prev_rows[i], rolled)
    return rolled


def _rms_matmul_kernel(x_ref, g_ref, w_ref, o_ref, *, n_chunk):
    xn = _rms_rows(x_ref[...], g_ref[...]).astype(BF16)
    n = o_ref.shape[-1]
    for j in range(0, n, n_chunk):
        o_ref[:, j:j + n_chunk] = jnp.dot(xn, w_ref[:, j:j + n_chunk],
                                          preferred_element_type=F32).astype(o_ref.dtype)


def _rms_matmul(h, gain, w, tl, out_dtype):
    b, lp, d = h.shape
    n = w.shape[1]
    return pl.pallas_call(
        functools.partial(_rms_matmul_kernel, n_chunk=256),
        grid=(b, lp // tl),
        in_specs=[pl.BlockSpec((None, tl, d), lambda i, t: (i, t, 0)),
                  pl.BlockSpec((1, d), lambda i, t: (0, 0)),
                  pl.BlockSpec((d, n), lambda i, t: (0, 0))],
        out_specs=pl.BlockSpec((None, tl, n), lambda i, t: (i, t, 0)),
        out_shape=jax.ShapeDtypeStruct((b, lp, n), out_dtype),
        compiler_params=pltpu.CompilerParams(
            dimension_semantics=("parallel", "parallel"), vmem_limit_bytes=VMEM_LIMIT),
        name="rms_in_proj",
    )(h, gain, w)


def _rwkv_kernel(r_ref, k_ref, v_ref, lo_ref, prm_ref, mul_ref, w2_ref, a2_ref, g2_ref,
                 hsum_ref, tri_ref, lvl_ref, o_ref,
                 s_scr, prev_scr, prevl_scr, y_scr, q_scr, *, tl, group):
    t = pl.program_id(2)

    @pl.when(t == 0)
    def _():
        s_scr[...] = jnp.zeros_like(s_scr)
        prev_scr[...] = jnp.zeros_like(prev_scr)
        prevl_scr[...] = jnp.zeros_like(prevl_scr)

    prm = prm_ref[...]
    row = lambda i: prm[i:i + 1, :]
    hsum = hsum_ref[...]

    def tshift(x, prev, mu):
        prevx = _shift_rows(x, 1, [prev])
        return x + (prevx - x) * mu

    r_raw, k_raw, v_raw, lo_raw = r_ref[...], k_ref[...], v_ref[...], lo_ref[...]
    r = tshift(r_raw, prev_scr[0:1, :], row(0))
    k = tshift(k_raw, prev_scr[1:2, :], row(1))
    v = tshift(v_raw, prev_scr[2:3, :], row(2))
    lo = tshift(lo_raw, prevl_scr[0:1, :], mul_ref[...])
    prev_scr[0:1, :] = r_raw[tl - 1:tl, :]
    prev_scr[1:2, :] = k_raw[tl - 1:tl, :]
    prev_scr[2:3, :] = v_raw[tl - 1:tl, :]
    prevl_scr[0:1, :] = lo_raw[tl - 1:tl, :]

    lo01 = lo[:, :LANES]
    w_pre = row(3) + _mm(jnp.tanh(lo01), w2_ref[...])
    logw = -jnp.exp(-_softplus(-w_pre) - 0.5)
    a = _sigmoid(row(4) + _mm(lo01, a2_ref[...]))
    g = _mm(_sigmoid(lo[:, LANES:]), g2_ref[...])
    kk = k * row(5)
    kk = kk / jnp.maximum(jnp.sqrt(_split2_mm(kk * kk, hsum)), 1e-12)
    kf = k * (1.0 + (a - 1.0) * row(6))
    ka = kk * a
    bonus = _split2_mm(r * kf * row(7), hsum) * v

    q_scr[0] = r
    q_scr[1] = logw
    q_scr[2] = kf
    q_scr[3] = v
    q_scr[4] = kk
    q_scr[5] = ka

    lane = lax.broadcasted_iota(jnp.int32, (1, LANES), 1)
    m0 = lane < HEAD_DIM
    tri = tri_ref[...]
    ri = lax.broadcasted_iota(jnp.int32, (LANES, LANES), 0)
    ci = lax.broadcasted_iota(jnp.int32, (LANES, LANES), 1)
    rt_, ct_ = ri & (CHUNK - 1), ci & (CHUNK - 1)
    strict = ct_ < rt_
    incl = ct_ <= rt_
    eye = ri == ci

    def stack(x):
        return jnp.concatenate([jnp.where(m0, x, 0.0), jnp.where(m0, 0.0, x)], axis=0).astype(BF16)

    def chunk_pre(c0):
        sl = pl.ds(c0, CHUNK)
        rc, lw, kc, vc, kkc, kac = (q_scr[i, sl, :] for i in range(6))
        cum = _cumsum_rows(tri, lw)
        cum_c = cum[CHUNK - 1:CHUNK, :]
        w_in = jnp.exp(cum)
        w_inv = jnp.exp(-cum)
        w_ex = jnp.exp(cum - lw)
        w_end = jnp.exp(cum_c - cum)
        at_s = stack(-kkc * w_ex)
        rt_f = rc * w_in
        rt_s = stack(rt_f)
        bt_s = stack(kac * w_inv)
        kt_s = stack(kc * w_inv)
        bh_s = stack(kac * w_end)
        kh_s = stack(kc * w_end)
        v_s = stack(vc)
        am = _mm_nt(jnp.concatenate([at_s, rt_s], axis=0), jnp.concatenate([bt_s, kt_s], axis=0))
        l_ab = jnp.where(strict, am[:LANES, :LANES], 0.0)
        a_ak = jnp.where(strict, am[:LANES, LANES:], 0.0)
        m_rb = jnp.where(incl, am[LANES:, :LANES], 0.0)
        m_rk = jnp.where(incl, am[LANES:, LANES:], 0.0)
        tm = jnp.where(eye, 1.0, 0.0) + l_ab * lvl_ref[0]
        for lv in range(1, 6):
            tm = tm + _mm(tm, _mm(l_ab * lvl_ref[lv], tm))
        av = _mm(a_ak, v_s)
        pq = _mm(tm, jnp.concatenate([at_s, av.astype(BF16)], axis=1))
        p_s, q_s = pq[:, :LANES], pq[:, LANES:]
        qv = jnp.concatenate([q_s.astype(BF16), v_s], axis=0)
        rp = rt_s.astype(F32) + _mm(m_rb, p_s)
        y0 = _mm(jnp.concatenate([m_rb, m_rk], axis=1), qv)
        gm = jnp.where(eye, jnp.exp(cum_c), 0.0) + _mm_tn(bh_s, p_s)
        hm = _mm_tn(jnp.concatenate([bh_s, kh_s], axis=0), qv)
        return rp, y0, gm, hm

    def group_body(gi, carry):
        base = pl.multiple_of(gi * (group * CHUNK), group * CHUNK)
        pres = [chunk_pre(base + j * CHUNK) for j in range(group)]
        s = s_scr[...]
        for j, (rp, y0, gm, hm) in enumerate(pres):
            ys = _mm(rp, s) + y0
            y_scr[pl.ds(base + j * CHUNK, CHUNK), :] = ys[:CHUNK] + ys[CHUNK:]
            s = _mm(gm, s) + hm
        s_scr[...] = s
        return carry

    lax.fori_loop(0, tl // (group * CHUNK), group_body, 0)

    y = y_scr[...]
    mean = _split2_mm(y, hsum) * (1.0 / HEAD_DIM)
    yc = y - mean
    var = _split2_mm(yc * yc, hsum) * (1.0 / HEAD_DIM)
    yn = yc * lax.rsqrt(var + GN_EPS) * row(8) + row(9)
    o_ref[...] = ((yn + bonus) * g).astype(o_ref.dtype)


def _rwkv(p, prm, mu_l, w2p, a2p, g2, hsum, tri, lvl, tl, group):
    b, lp, _ = p.shape
    nt = lp // tl
    blk = lambda off: pl.BlockSpec((None, tl, LANES), lambda i, h, t: (i, t, off + h))
    const2 = lambda shape: pl.BlockSpec(shape, lambda i, h, t: (0, 0))
    return pl.pallas_call(
        functools.partial(_rwkv_kernel, tl=tl, group=group),
        grid=(b, PAIRS_RWKV, nt),
        in_specs=[blk(0), blk(PAIRS_RWKV), blk(2 * PAIRS_RWKV),
                  pl.BlockSpec((None, tl, LORA_COLS), lambda i, h, t: (i, t, 3 * RWKV_DIM // LORA_COLS)),
                  pl.BlockSpec((None, 16, LANES), lambda i, h, t: (h, 0, 0)),
                  const2((1, LORA_COLS)),
                  pl.BlockSpec((LANES, LANES), lambda i, h, t: (0, h)),
                  pl.BlockSpec((LANES, LANES), lambda i, h, t: (0, h)),
                  pl.BlockSpec((LANES, LANES), lambda i, h, t: (0, h)),
                  const2((LANES, LANES)),
                  const2((CHUNK, CHUNK)),
                  pl.BlockSpec((6, LANES, LANES), lambda i, h, t: (0, 0, 0))],
        out_specs=pl.BlockSpec((None, tl, LANES), lambda i, h, t: (i, t, h)),
        out_shape=jax.ShapeDtypeStruct((b, lp, RWKV_DIM), BF16),
        scratch_shapes=[pltpu.VMEM((LANES, LANES), F32),
                        pltpu.VMEM((8, LANES), F32),
                        pltpu.VMEM((8, LORA_COLS), F32),
                        pltpu.VMEM((tl, LANES), F32),
                        pltpu.VMEM((6, tl, LANES), F32)],
        compiler_params=pltpu.CompilerParams(
            dimension_semantics=("parallel", "parallel", "arbitrary"), vmem_limit_bytes=VMEM_LIMIT),
        name="rwkv7_mix",
    )(p, p, p, p, prm, mu_l, w2p, a2p, g2, hsum, tri, lvl)


def _s5_kernel(u_ref, bm_ref, cm_ref, lam_ref, tab_ref, d_ref, gw_ref, gb_ref, o_ref,
               h0_scr, hs_scr, *, tl):
    t = pl.program_id(1)

    @pl.when(t == 0)
    def _():
        h0_scr[...] = jnp.zeros_like(h0_scr)

    n_lv = int(math.log2(S5_SUB))
    row_i = lax.broadcasted_iota(jnp.int32, (S5_SUB, LANES), 0)

    def sub_body(si, carry):
        r0 = pl.multiple_of(si * S5_SUB, S5_SUB)
        u = u_ref[pl.ds(r0, S5_SUB), :]
        bu = _mm(u, bm_ref[...])
        for lb in range(S5_LANES // LANES):
            ls = slice(lb * LANES, (lb + 1) * LANES)
            li = slice(S5_LANES + lb * LANES, S5_LANES + (lb + 1) * LANES)
            hr, hi = bu[:, ls], bu[:, li]
            for lv in range(n_lv):
                sh = 1 << lv
                lr = lam_ref[2 * lv:2 * lv + 1, ls]
                lm = lam_ref[2 * lv + 1:2 * lv + 2, ls]
                keep = row_i >= sh
                sr = jnp.where(keep, pltpu.roll(hr, sh, 0), 0.0)
                sm = jnp.where(keep, pltpu.roll(hi, sh, 0), 0.0)
                hr, hi = hr + (lr * sr - lm * sm), hi + (lr * sm + lm * sr)
            tr, tm = tab_ref[0, :, ls], tab_ref[1, :, ls]
            h0r, h0i = h0_scr[0:1, ls], h0_scr[1:2, ls]
            hr, hi = hr + (tr * h0r - tm * h0i), hi + (tr * h0i + tm * h0r)
            h0_scr[0:1, ls] = hr[S5_SUB - 1:S5_SUB, :]
            h0_scr[1:2, ls] = hi[S5_SUB - 1:S5_SUB, :]
            hs_scr[:, ls] = hr.astype(BF16)
            hs_scr[:, li] = hi.astype(BF16)
        y = jnp.dot(hs_scr[...], cm_ref[...], preferred_element_type=F32) + d_ref[...] * u
        y = 0.5 * y * (1.0 + jnp.tanh(0.7978845608028654 * (y + 0.044715 * (y * y * y))))
        z = _mm(y, gw_ref[...]) + gb_ref[...]
        o_ref[pl.ds(r0, S5_SUB), :] = (y * _sigmoid(z)).astype(o_ref.dtype)
        return carry

    lax.fori_loop(0, tl // S5_SUB, sub_body, 0)


def _s5(p, bm, cm, lam_pows, tab, d_skip, glu_w, glu_b, tl):
    b, lp, _ = p.shape
    c2 = lambda a: pl.BlockSpec(a.shape, lambda i, t: (0,) * a.ndim)
    return pl.pallas_call(
        functools.partial(_s5_kernel, tl=tl),
        grid=(b, lp // tl),
        in_specs=[pl.BlockSpec((None, tl, S5_DIM), lambda i, t: (i, t, RWKV_COLS // S5_DIM)),
                  c2(bm), c2(cm), c2(lam_pows), c2(tab), c2(d_skip), c2(glu_w), c2(glu_b)],
        out_specs=pl.BlockSpec((None, tl, S5_DIM), lambda i, t: (i, t, 0)),
        out_shape=jax.ShapeDtypeStruct((b, lp, S5_DIM), BF16),
        scratch_shapes=[pltpu.VMEM((8, S5_LANES), F32),
                        pltpu.VMEM((S5_SUB, 2 * S5_LANES), BF16)],
        compiler_params=pltpu.CompilerParams(
            dimension_semantics=("parallel", "arbitrary"), vmem_limit_bytes=VMEM_LIMIT),
        name="s5_mix",
    )(p, bm, cm, lam_pows, tab, d_skip, glu_w, glu_b)


def _out_proj_kernel(*refs, n_in):
    h_ref = refs[0]
    xs = refs[1:1 + n_in]
    ws = refs[1 + n_in:1 + 2 * n_in]
    o_ref = refs[1 + 2 * n_in]
    acc = h_ref[...]
    for x_ref, w_ref in zip(xs, ws):
        acc = acc + jnp.dot(x_ref[...], w_ref[...], preferred_element_type=F32)
    o_ref[...] = acc


def _out_proj(h, xs, ws, tl):
    b, lp, d = h.shape
    n_in = len(xs)
    in_specs = [pl.BlockSpec((None, tl, d), lambda i, t: (i, t, 0))]
    in_specs += [pl.BlockSpec((None, tl, x.shape[-1]), lambda i, t: (i, t, 0)) for x in xs]
    in_specs += [pl.BlockSpec(w.shape, lambda i, t: (0, 0)) for w in ws]
    return pl.pallas_call(
        functools.partial(_out_proj_kernel, n_in=n_in),
        grid=(b, lp // tl),
        in_specs=in_specs,
        out_specs=pl.BlockSpec((None, tl, d), lambda i, t: (i, t, 0)),
        out_shape=jax.ShapeDtypeStruct((b, lp, d), F32),
        compiler_params=pltpu.CompilerParams(
            dimension_semantics=("parallel", "parallel"), vmem_limit_bytes=VMEM_LIMIT),
        name="out_proj_residual",
    )(h, *xs, *ws)


def _ffn_kernel(h_ref, g_ref, wg_ref, wv_ref, cw_ref, cb_ref, wd_ref, o_ref,
                xn_scr, acc_scr, prev_scr, *, tl):
    t = pl.program_id(1)
    n_ch = D_FF // FF_CHUNK

    @pl.when(t == 0)
    def _():
        prev_scr[...] = jnp.zeros_like(prev_scr)

    x = h_ref[...]
    xn_scr[...] = _rms_rows(x, g_ref[...]).astype(BF16)
    acc_scr[...] = x

    def conv(hh, prev8, cw, cb):
        p6, p7 = prev8[6:7, :], prev8[7:8, :]
        s1 = _shift_rows(hh, 1, [p7])
        s2 = _shift_rows(hh, 2, [p6, p7])
        return cw[0:1, :] * s2 + cw[1:2, :] * s1 + cw[2:3, :] * hh + cb

    def body(c, carry):
        c0 = pl.multiple_of(c * FF_CHUNK, FF_CHUNK)
        xn = xn_scr[...]
        hg = jnp.dot(xn, wg_ref[:, pl.ds(c0, FF_CHUNK)], preferred_element_type=F32)
        hv = jnp.dot(xn, wv_ref[:, pl.ds(c0, FF_CHUNK)], preferred_element_type=F32)
        cg = conv(hg, prev_scr[0, c], cw_ref[0, c], cb_ref[0, c])
        cv = conv(hv, prev_scr[1, c], cw_ref[1, c], cb_ref[1, c])
        prev_scr[0, c] = hg[tl - 8:tl, :]
        prev_scr[1, c] = hv[tl - 8:tl, :]
        act = (cg * _sigmoid(cg) * cv).astype(BF16)
        acc_scr[...] += jnp.dot(act, wd_ref[pl.ds(c0, FF_CHUNK), :], preferred_element_type=F32)
        return carry

    lax.fori_loop(0, n_ch, body, 0)
    o_ref[...] = acc_scr[...]


def _ffn(h, gain, wg, wv, cw, cb, wd, tl):
    b, lp, d = h.shape
    n_ch = D_FF // FF_CHUNK
    full = lambda a: pl.BlockSpec(a.shape, lambda i, t: (0,) * a.ndim, pipeline_mode=pl.Buffered(1))
    return pl.pallas_call(
        functools.partial(_ffn_kernel, tl=tl),
        grid=(b, lp // tl),
        in_specs=[pl.BlockSpec((None, tl, d), lambda i, t: (i, t, 0)),
                  full(gain), full(wg), full(wv), full(cw), full(cb), full(wd)],
        out_specs=pl.BlockSpec((None, tl, d), lambda i, t: (i, t, 0)),
        out_shape=jax.ShapeDtypeStruct((b, lp, d), F32),
        scratch_shapes=[pltpu.VMEM((tl, d), BF16),
                        pltpu.VMEM((tl, d), F32),
                        pltpu.VMEM((2, n_ch, 8, FF_CHUNK), F32)],
        compiler_params=pltpu.CompilerParams(
            dimension_semantics=("parallel", "arbitrary"), vmem_limit_bytes=VMEM_LIMIT),
        name="conv_gated_mlp",
    )(h, gain, wg, wv, cw, cb, wd)


def _fox_in_kernel(x_ref, g_ref, w_ref, qg_ref, kg_ref, bf_ref, avg_ref, tri_ref,
                   q_ref, k_ref, v_ref, cum_ref, carry_scr, *, tl):
    t = pl.program_id(1)

    @pl.when(t == 0)
    def _():
        carry_scr[...] = jnp.zeros_like(carry_scr)

    xn = _rms_rows(x_ref[...], g_ref[...]).astype(BF16)
    avg = avg_ref[...]
    for s in range(PAIRS_FOX):
        cs = slice(s * LANES, (s + 1) * LANES)
        for off, gain_ref, dst in ((0, qg_ref, q_ref), (D_MODEL, kg_ref, k_ref)):
            z = jnp.dot(xn, w_ref[:, off + s * LANES:off + (s + 1) * LANES], preferred_element_type=F32)
            ms = _split2_mm(z * z, avg)
            dst[:, cs] = (z * lax.rsqrt(ms + RMS_EPS) * gain_ref[...]).astype(BF16)
        v_ref[:, cs] = jnp.dot(xn, w_ref[:, 2 * D_MODEL + s * LANES:2 * D_MODEL + (s + 1) * LANES],
                               preferred_element_type=F32).astype(BF16)
    f = jnp.dot(xn, w_ref[:, 3 * D_MODEL:], preferred_element_type=F32) + bf_ref[...]
    log_f = jnp.minimum(f, 0.0) - jnp.log(1.0 + jnp.exp(-jnp.abs(f)))
    cum = _cumsum_rows(tri_ref[...], log_f) + carry_scr[0:1, :]
    carry_scr[0:1, :] = cum[tl - 1:tl, :]
    cum_ref[...] = cum


def _fox_in(h, gain, w, qg, kg, bf, avg, tri, tl):
    b, lp, d = h.shape
    c2 = lambda a: pl.BlockSpec(a.shape, lambda i, t: (0,) * a.ndim)
    act = lambda n, dt: (pl.BlockSpec((None, tl, n), lambda i, t: (i, t, 0)),
                         jax.ShapeDtypeStruct((b, lp, n), dt))
    specs, shapes = zip(act(d, BF16), act(d, BF16), act(d, BF16), act(LANES, F32))
    return pl.pallas_call(
        functools.partial(_fox_in_kernel, tl=tl),
        grid=(b, lp // tl),
        in_specs=[pl.BlockSpec((None, tl, d), lambda i, t: (i, t, 0)),
                  c2(gain), c2(w), c2(qg), c2(kg), c2(bf), c2(avg), c2(tri)],
        out_specs=list(specs),
        out_shape=list(shapes),
        scratch_shapes=[pltpu.VMEM((8, LANES), F32)],
        compiler_params=pltpu.CompilerParams(
            dimension_semantics=("parallel", "arbitrary"), vmem_limit_bytes=VMEM_LIMIT),
        name="fox_in_proj",
    )(h, gain, w, qg, kg, bf, avg, tri)


def _fox_attn_kernel(qi_ref, ki_ref, q_ref, k_ref, v_ref, cq_ref, ck_ref, o_ref,
                     m_scr, l_scr, acc_scr, *, ta):
    s = pl.program_id(2)
    qi, ki = qi_ref[s], ki_ref[s]

    @pl.when(ki == 0)
    def _():
        m_scr[...] = jnp.full_like(m_scr, -1e30)
        l_scr[...] = jnp.zeros_like(l_scr)
        acc_scr[...] = jnp.zeros_like(acc_scr)

    q, k, v = q_ref[...], k_ref[...], v_ref[...]
    lane = lax.broadcasted_iota(jnp.int32, (1, LANES), 1)
    m0 = lane < HEAD_DIM
    rr = lax.broadcasted_iota(jnp.int32, (ta, ta), 0)
    cc = lax.broadcasted_iota(jnp.int32, (ta, ta), 1)
    allowed = jnp.logical_or(cc <= rr, ki < qi)
    zero = jnp.zeros_like(q)
    acc = acc_scr[...]
    for h in range(2):
        mask = m0 if h == 0 else jnp.logical_not(m0)
        sc = lax.dot_general(jnp.where(mask, q, zero), k, (((1,), (1,)), ((), ())),
                             preferred_element_type=F32)
        sc = sc + (cq_ref[:, h:h + 1] - ck_ref[h:h + 1, :])
        sc = jnp.where(allowed, sc, -1e30)
        m_prev = m_scr[h]
        m_new = jnp.maximum(m_prev, jnp.max(sc, axis=1, keepdims=True))
        p = jnp.exp(sc - m_new)
        alpha = jnp.exp(m_prev - m_new)
        l_scr[h] = alpha * l_scr[h] + jnp.sum(p, axis=1, keepdims=True)
        m_scr[h] = m_new
        pv = jnp.dot(p.astype(BF16), jnp.where(mask, v, zero), preferred_element_type=F32)
        acc = acc * jnp.where(mask, alpha, 1.0) + pv
    acc_scr[...] = acc

    @pl.when(ki == qi)
    def _():
        inv = jnp.where(m0, 1.0 / l_scr[0], 1.0 / l_scr[1])
        o_ref[...] = (acc_scr[...] * inv).astype(o_ref.dtype)


def _fox_attn(q, k, v, cq, ck, ta):
    b, lp, d = q.shape
    nq = lp // ta
    qi_tab = np.array([i for i in range(nq) for _ in range(i + 1)], np.int32)
    ki_tab = np.array([j for i in range(nq) for j in range(i + 1)], np.int32)
    grid_spec = pltpu.PrefetchScalarGridSpec(
        num_scalar_prefetch=2,
        grid=(b, PAIRS_FOX, len(qi_tab)),
        in_specs=[pl.BlockSpec((None, ta, LANES), lambda i, h, s, qt, kt: (i, qt[s], h)),
                  pl.BlockSpec((None, ta, LANES), lambda i, h, s, qt, kt: (i, kt[s], h)),
                  pl.BlockSpec((None, ta, LANES), lambda i, h, s, qt, kt: (i, kt[s], h)),
                  pl.BlockSpec((None, None, ta, 2), lambda i, h, s, qt, kt: (i, h, qt[s], 0)),
                  pl.BlockSpec((None, None, 2, ta), lambda i, h, s, qt, kt: (i, h, 0, kt[s]))],
        out_specs=pl.BlockSpec((None, ta, LANES), lambda i, h, s, qt, kt: (i, qt[s], h)),
        scratch_shapes=[pltpu.VMEM((2, ta, 1), F32),
                        pltpu.VMEM((2, ta, 1), F32),
                        pltpu.VMEM((ta, LANES), F32)],
    )
    return pl.pallas_call(
        functools.partial(_fox_attn_kernel, ta=ta),
        grid_spec=grid_spec,
        out_shape=jax.ShapeDtypeStruct((b, lp, d), BF16),
        compiler_params=pltpu.CompilerParams(
            dimension_semantics=("parallel", "parallel", "arbitrary"), vmem_limit_bytes=VMEM_LIMIT),
        name="fox_attention",
    )(jnp.asarray(qi_tab), jnp.asarray(ki_tab), q, k, v, cq, ck)


def _block_diag(blocks):
    g, r, c = blocks.shape
    eye = jnp.eye(g, dtype=blocks.dtype)
    return (eye[:, None, :, None] * blocks[:, :, None, :]).reshape(g * r, g * c)


def _cmul(ar, ai, br, bi):
    return ar * br - ai * bi, ar * bi + ai * br


def _s5_params(a_re, a_im, log_dt, b_re, b_im, c_re, c_im):
    dt = jnp.exp(log_dt)[:, None]
    mag = jnp.exp(a_re * dt)
    lam_re, lam_im = mag * jnp.cos(a_im * dt), mag * jnp.sin(a_im * dt)
    den = a_re * a_re + a_im * a_im
    z_re = ((lam_re - 1.0) * a_re + lam_im * a_im) / den
    z_im = (lam_im * a_re - (lam_re - 1.0) * a_im) / den
    bb_re = z_re[..., None] * b_re - z_im[..., None] * b_im
    bb_im = z_re[..., None] * b_im + z_im[..., None] * b_re
    to_in = lambda m: _block_diag(jnp.swapaxes(m, 1, 2))
    bm = jnp.concatenate([to_in(bb_re), to_in(bb_im)], axis=1).astype(BF16)
    to_out = lambda m: _block_diag(jnp.swapaxes(m, 1, 2))
    cm = jnp.concatenate([to_out(c_re), -to_out(c_im)], axis=0).astype(BF16)
    lr, li = lam_re.reshape(1, S5_LANES), lam_im.reshape(1, S5_LANES)
    n_lv = int(math.log2(S5_SUB))
    pows, tr, ti = [], lr, li
    pr, pi = lr, li
    for _ in range(n_lv):
        pows += [pr, pi]
        nr, ni = _cmul(tr, ti, pr, pi)
        tr, ti = jnp.concatenate([tr, nr], axis=0), jnp.concatenate([ti, ni], axis=0)
        pr, pi = _cmul(pr, pi, pr, pi)
    pows = jnp.concatenate(pows + [jnp.zeros((16 - 2 * n_lv, S5_LANES), F32)], axis=0)
    return bm, cm, pows, jnp.stack([tr, ti])


def _rwkv_consts():
    idx = np.arange(LANES)
    hsum = (idx[:, None] // HEAD_DIM == idx[None, :] // HEAD_DIM).astype(np.float32)
    t = idx % CHUNK
    lv = [(t[:, None] // 2 == t[None, :] // 2)]
    b = 2
    while b < CHUNK:
        lv.append((t[:, None] // (2 * b) == t[None, :] // (2 * b)) & (t[:, None] // b != t[None, :] // b))
        b *= 2
    tri = (np.arange(CHUNK)[None, :] <= np.arange(CHUNK)[:, None]).astype(np.float32)
    return (jnp.asarray(hsum, BF16), jnp.asarray(tri, BF16), jnp.asarray(np.stack(lv).astype(np.float32)))


def _time_tile(length):
    return 640 if length > 2048 else 128


def kernel(x, meta_tokens, mix_norm, ffn_norm, even_w_in, even_w_out, rwkv_mu, rwkv_w0, rwkv_w2, rwkv_a0, rwkv_a2, rwkv_g2, rwkv_k_k, rwkv_k_a, rwkv_r_k, rwkv_ln_w, rwkv_ln_b, s5_a_re, s5_a_im, s5_log_dt, s5_b_re, s5_b_im, s5_c_re, s5_c_im, s5_d, s5_glu_w, s5_glu_b, odd_w_in, odd_w_out, fox_b_f, fox_q_gain, fox_k_gain, ffn_w_up, ffn_conv_w, ffn_conv_b, ffn_w_down):
    bsz, seq, d = x.shape
    length = N_META + seq
    tl = _time_tile(length)
    lp = -(-length // tl) * tl
    meta = jnp.broadcast_to(meta_tokens[None].astype(x.dtype), (bsz, N_META, d))
    h = jnp.concatenate([meta, x, jnp.zeros((bsz, lp - length, d), x.dtype)], axis=1)

    hsum, tri64, lvl = _rwkv_consts()
    n_ch = D_FF // FF_CHUNK

    def ffn(h, layer):
        w_up = ffn_w_up[layer].astype(BF16)
        cw = ffn_conv_w[layer].reshape(3, 2, n_ch, FF_CHUNK).transpose(1, 2, 0, 3)
        cw = jnp.pad(cw, ((0, 0), (0, 0), (0, 5), (0, 0)))
        cb = ffn_conv_b[layer].reshape(2, n_ch, 1, FF_CHUNK)
        return _ffn(h, ffn_norm[layer][None], w_up[:, :D_FF], w_up[:, D_FF:], cw, cb,
                    ffn_w_down[layer].astype(BF16), tl)

    p = _rms_matmul(h, mix_norm[0][None], even_w_in[0].astype(BF16), tl, F32)
    pairs = lambda vec: vec.reshape(PAIRS_RWKV, 1, LANES)
    mu = rwkv_mu[0]
    prm = jnp.concatenate(
        [pairs(mu[:RWKV_DIM]), pairs(mu[RWKV_DIM:2 * RWKV_DIM]), pairs(mu[2 * RWKV_DIM:3 * RWKV_DIM]),
         pairs(rwkv_w0[0]), pairs(rwkv_a0[0]), pairs(rwkv_k_k[0]), pairs(rwkv_k_a[0]),
         pairs(rwkv_r_k[0].reshape(-1)), pairs(rwkv_ln_w[0]), pairs(rwkv_ln_b[0]),
         jnp.zeros((PAIRS_RWKV, 6, LANES), F32)], axis=1)
    zeros64 = jnp.zeros((64, RWKV_DIM), F32)
    w2p = jnp.concatenate([rwkv_w2[0], zeros64], axis=0).astype(BF16)
    a2p = jnp.concatenate([zeros64, rwkv_a2[0]], axis=0).astype(BF16)
    a_out = _rwkv(p, prm, mu[3 * RWKV_DIM:][None], w2p, a2p, rwkv_g2[0].astype(BF16),
                  hsum, tri64, lvl, tl, 2 if tl % (2 * CHUNK) == 0 else 1)
    bm, cm, pows, tab = _s5_params(s5_a_re[0], s5_a_im[0], s5_log_dt[0], s5_b_re[0], s5_b_im[0],
                                   s5_c_re[0], s5_c_im[0])
    b_out = _s5(p, bm, cm, pows, tab, s5_d[0][None], s5_glu_w[0].astype(BF16), s5_glu_b[0][None], tl)
    w_out = even_w_out[0].astype(BF16)
    h = _out_proj(h, [a_out, b_out], [w_out[:RWKV_DIM], w_out[RWKV_DIM:]], tl)
    h = ffn(h, 0)

    w_in = jnp.pad(odd_w_in[0], ((0, 0), (0, ODD_IN_PAD - odd_w_in.shape[-1]))).astype(BF16)
    scale = HEAD_DIM ** -0.5
    qg = jnp.tile(fox_q_gain[0], 2)[None] * scale
    kg = jnp.tile(fox_k_gain[0], 2)[None]
    bf = jnp.pad(fox_b_f[0], (0, LANES - FOX_HEADS))[None]
    tri_t = jnp.asarray(np.tril(np.ones((tl, tl), np.float32)), BF16)
    q, k, v, cum = _fox_in(h, mix_norm[1][None], w_in, qg, kg, bf, hsum * (1.0 / HEAD_DIM), tri_t, tl)
    cum = cum[:, :, :FOX_HEADS].reshape(bsz, lp, PAIRS_FOX, 2)
    cq = jnp.transpose(cum, (0, 2, 1, 3))
    ck = jnp.transpose(cum, (0, 2, 3, 1))
    o = _fox_attn(q, k, v, cq, ck, tl)
    h = _out_proj(h, [o], [odd_w_out[0].astype(BF16)], tl)
    h = ffn(h, 1)
    return h[:, N_META:N_META + seq]
```

```python
import functools
import math

import jax
import jax.numpy as jnp
import numpy as np
from jax import lax
from jax.experimental import pallas as pl
from jax.experimental.pallas import tpu as pltpu

F32 = jnp.float32
BF16 = jnp.bfloat16

D_MODEL = 1024
N_META = 16
RMS_EPS = 1e-6
GN_EPS = 64e-5

RWKV_HEADS = 12
HEAD_DIM = 64
RWKV_DIM = RWKV_HEADS * HEAD_DIM
LORA_COLS = 256
RWKV_COLS = 3 * RWKV_DIM + LORA_COLS
S5_DIM = 256
S5_GROUP = 16
S5_GROUPS = 16
S5_STATE = 64
S5_LANES = S5_GROUPS * S5_STATE
EVEN_IN = RWKV_COLS + S5_DIM
FOX_HEADS = 16
ODD_IN_PAD = 3 * D_MODEL + 128
D_FF = 2816
LANES = 128
PAIRS_RWKV = RWKV_DIM // LANES
PAIRS_FOX = D_MODEL // LANES
CHUNK = 64
S5_SUB = 128
FF_CHUNK = 256
VMEM_LIMIT = 56 * 1024 * 1024
LOG2E = 1.4426950408889634
ATTN_ROWS = 256


def _mm(a, b):
    return jnp.dot(a.astype(BF16), b.astype(BF16), preferred_element_type=F32)


def _mm_nt(a, b):
    return lax.dot_general(a.astype(BF16), b.astype(BF16), (((1,), (1,)), ((), ())),
                           preferred_element_type=F32)


def _mm_tn(a, b):
    return lax.dot_general(a.astype(BF16), b.astype(BF16), (((0,), (0,)), ((), ())),
                           preferred_element_type=F32)


def _split2_mm(x, m):
    hi = x.astype(BF16)
    lo = (x - hi.astype(F32)).astype(BF16)
    return (jnp.dot(hi, m, preferred_element_type=F32)
            + jnp.dot(lo, m, preferred_element_type=F32))


def _cumsum_rows(tri, x):
    h1 = x.astype(BF16)
    r1 = x - h1.astype(F32)
    h2 = r1.astype(BF16)
    h3 = (r1 - h2.astype(F32)).astype(BF16)
    w = x.shape[1]
    res = jnp.dot(tri, jnp.concatenate([h1, h2, h3], axis=1), preferred_element_type=F32)
    return res[:, :w] + res[:, w:2 * w] + res[:, 2 * w:]


def _softplus(z):
    return jnp.maximum(z, 0.0) + jnp.log(1.0 + jnp.exp(-jnp.abs(z)))


def _sigmoid(z):
    return 1.0 / (1.0 + jnp.exp(-z))


def _rms_rows(x, gain):
    ms = jnp.mean(x * x, axis=-1, keepdims=True)
    return x * lax.rsqrt(ms + RMS_EPS) * gain


def _shift_rows(x, k, prev_rows):
    rolled = pltpu.roll(x, k, 0)
    row = lax.broadcasted_iota(jnp.int32, x.shape, 0)
    for i in range(k):
        rolled = jnp.where(row == i, prev_rows[i], rolled)
    return rolled


def _rms_matmul_kernel(x_ref, g_ref, w_ref, o_ref, *, n_chunk):
    xn = _rms_rows(x_ref[...], g_ref[...]).astype(BF16)
    n = o_ref.shape[-1]
    for j in range(0, n, n_chunk):
        o_ref[:, j:j + n_chunk] = jnp.dot(xn, w_ref[:, j:j + n_chunk],
                                          preferred_element_type=F32).astype(o_ref.dtype)


def _rms_matmul(h, gain, w, tl, out_dtype):
    b, lp, d = h.shape
    n = w.shape[1]
    return pl.pallas_call(
        functools.partial(_rms_matmul_kernel, n_chunk=256),
        grid=(b, lp // tl),
        in_specs=[pl.BlockSpec((None, tl, d), lambda i, t: (i, t, 0)),
                  pl.BlockSpec((1, d), lambda i, t: (0, 0)),
                  pl.BlockSpec((d, n), lambda i, t: (0, 0))],
        out_specs=pl.BlockSpec((None, tl, n), lambda i, t: (i, t, 0)),
        out_shape=jax.ShapeDtypeStruct((b, lp, n), out_dtype),
        compiler_params=pltpu.CompilerParams(
            dimension_semantics=("parallel", "parallel"), vmem_limit_bytes=VMEM_LIMIT),
        name="rms_in_proj",
    )(h, gain, w)


def _rwkv_kernel(r_ref, k_ref, v_ref, lo_ref, prm_ref, mul_ref, w2_ref, a2_ref, g2_ref,
                 hsum_ref, tri_ref, lvl_ref, o_ref,
                 s_scr, prev_scr, prevl_scr, y_scr, q_scr, *, tl, group):
    t = pl.program_id(2)

    @pl.when(t == 0)
    def _():
        s_scr[...] = jnp.zeros_like(s_scr)
        prev_scr[...] = jnp.zeros_like(prev_scr)
        prevl_scr[...] = jnp.zeros_like(prevl_scr)

    prm = prm_ref[...]
    row = lambda i: prm[i:i + 1, :]
    hsum = hsum_ref[...]

    def tshift(x, prev, mu):
        prevx = _shift_rows(x, 1, [prev])
        return x + (prevx - x) * mu

    r_raw, k_raw, v_raw, lo_raw = r_ref[...], k_ref[...], v_ref[...], lo_ref[...]
    r = tshift(r_raw, prev_scr[0:1, :], row(0))
    k = tshift(k_raw, prev_scr[1:2, :], row(1))
    v = tshift(v_raw, prev_scr[2:3, :], row(2))
    lo = tshift(lo_raw, prevl_scr[0:1, :], mul_ref[...])
    prev_scr[0:1, :] = r_raw[tl - 1:tl, :]
    prev_scr[1:2, :] = k_raw[tl - 1:tl, :]
    prev_scr[2:3, :] = v_raw[tl - 1:tl, :]
    prevl_scr[0:1, :] = lo_raw[tl - 1:tl, :]

    lo01 = lo[:, :LANES]
    w_pre = row(3) + _mm(jnp.tanh(lo01), w2_ref[...])
    logw = -jnp.exp(-_softplus(-w_pre) - 0.5)
    a = _sigmoid(row(4) + _mm(lo01, a2_ref[...]))
    g = _mm(_sigmoid(lo[:, LANES:]), g2_ref[...])
    kk = k * row(5)
    kk = kk / jnp.maximum(jnp.sqrt(_split2_mm(kk * kk, hsum)), 1e-12)
    kf = k * (1.0 + (a - 1.0) * row(6))
    ka = kk * a
    bonus = _split2_mm(r * kf * row(7), hsum) * v

    q_scr[0] = r
    q_scr[1] = logw
    q_scr[2] = kf
    q_scr[3] = v
    q_scr[4] = kk
    q_scr[5] = ka

    lane = lax.broadcasted_iota(jnp.int32, (1, LANES), 1)
    m0 = lane < HEAD_DIM
    tri = tri_ref[...]
    ri = lax.broadcasted_iota(jnp.int32, (LANES, LANES), 0)
    ci = lax.broadcasted_iota(jnp.int32, (LANES, LANES), 1)
    rt_, ct_ = ri & (CHUNK - 1), ci & (CHUNK - 1)
    strict = ct_ < rt_
    incl = ct_ <= rt_
    eye = ri == ci

    def stack(x):
        return jnp.concatenate([jnp.where(m0, x, 0.0), jnp.where(m0, 0.0, x)], axis=0).astype(BF16)

    def group_pre(bases):
        rng = range(len(bases))
        ld = [[q_scr[i, pl.ds(b, CHUNK), :] for i in range(6)] for b in bases]
        cum = [_cumsum_rows(tri, ld[j][1]) for j in rng]
        ops = []
        for j in rng:
            rc, lw, kc, vc, kkc, kac = ld[j]
            cum_c = cum[j][CHUNK - 1:CHUNK, :]
            w_in = jnp.exp(cum[j])
            w_inv = jnp.exp(-cum[j])
            w_ex = jnp.exp(cum[j] - lw)
            w_end = jnp.exp(cum_c - cum[j])
            ops.append(dict(
                at=stack(-kkc * w_ex), rt=stack(rc * w_in), bt=stack(kac * w_inv), kt=stack(kc * w_inv),
                bh=stack(kac * w_end), kh=stack(kc * w_end), v=stack(vc), wc=jnp.exp(cum_c)))
        am = [_mm_nt(jnp.concatenate([o["at"], o["rt"]], axis=0), jnp.concatenate([o["bt"], o["kt"]], axis=0))
              for o in ops]
        l_ab = [jnp.where(strict, m[:LANES, :LANES], 0.0) for m in am]
        a_ak = [jnp.where(strict, m[:LANES, LANES:], 0.0).astype(BF16) for m in am]
        m_rb = [jnp.where(incl, m[LANES:, :LANES], 0.0).astype(BF16) for m in am]
        m_rk = [jnp.where(incl, m[LANES:, LANES:], 0.0).astype(BF16) for m in am]
        lv0 = lvl_ref[0]
        tm = [jnp.where(eye, 1.0, 0.0) + l * lv0 for l in l_ab]
        av = [_mm(a_ak[j], ops[j]["v"]) for j in rng]
        for lv in range(1, 6):
            lvm = lvl_ref[lv]
            z = [_mm(l_ab[j] * lvm, tm[j]) for j in rng]
            tm = [tm[j] + _mm(tm[j], z[j]) for j in rng]
        pq = [_mm(tm[j], jnp.concatenate([ops[j]["at"], av[j].astype(BF16)], axis=1)) for j in rng]
        p_s = [m[:, :LANES].astype(BF16) for m in pq]
        qv = [jnp.concatenate([pq[j][:, LANES:].astype(BF16), ops[j]["v"]], axis=0) for j in rng]
        rp = [ops[j]["rt"].astype(F32) + _mm(m_rb[j], p_s[j]) for j in rng]
        y0 = [_mm(jnp.concatenate([m_rb[j], m_rk[j]], axis=1), qv[j]) for j in rng]
        gm = [jnp.where(eye, ops[j]["wc"], 0.0) + _mm_tn(ops[j]["bh"], p_s[j]) for j in rng]
        hm = [_mm_tn(jnp.concatenate([ops[j]["bh"], ops[j]["kh"]], axis=0), qv[j]) for j in rng]
        return rp, y0, gm, hm

    def run_group(base):
        bases = [base + j * CHUNK for j in range(group)]
        rp, y0, gm, hm = group_pre(bases)
        s = s_scr[...]
        for j in range(group):
            ys = _mm(rp[j], s) + y0[j]
            y_scr[pl.ds(bases[j], CHUNK), :] = ys[:CHUNK] + ys[CHUNK:]
            s = _mm(gm[j], s) + hm[j]
        s_scr[...] = s

    n_groups = tl // (group * CHUNK)
    if n_groups == 1:
        run_group(0)
    else:
        def group_body(gi, carry):
            run_group(pl.multiple_of(gi * (group * CHUNK), group * CHUNK))
            return carry
        lax.fori_loop(0, n_groups, group_body, 0)

    y = y_scr[...]
    mean = _split2_mm(y, hsum) * (1.0 / HEAD_DIM)
    yc = y - mean
    var = _split2_mm(yc * yc, hsum) * (1.0 / HEAD_DIM)
    yn = yc * lax.rsqrt(var + GN_EPS) * row(8) + row(9)
    o_ref[...] = ((yn + bonus) * g).astype(o_ref.dtype)


def _rwkv(p, prm, mu_l, w2p, a2p, g2, hsum, tri, lvl, tl, group):
    b, lp, _ = p.shape
    nt = lp // tl
    blk = lambda off: pl.BlockSpec((None, tl, LANES), lambda i, h, t: (i, t, off + h))
    const2 = lambda shape: pl.BlockSpec(shape, lambda i, h, t: (0, 0))
    return pl.pallas_call(
        functools.partial(_rwkv_kernel, tl=tl, group=group),
        grid=(b, PAIRS_RWKV, nt),
        in_specs=[blk(0), blk(PAIRS_RWKV), blk(2 * PAIRS_RWKV),
                  pl.BlockSpec((None, tl, LORA_COLS), lambda i, h, t: (i, t, 3 * RWKV_DIM // LORA_COLS)),
                  pl.BlockSpec((None, 16, LANES), lambda i, h, t: (h, 0, 0)),
                  const2((1, LORA_COLS)),
                  pl.BlockSpec((LANES, LANES), lambda i, h, t: (0, h)),
                  pl.BlockSpec((LANES, LANES), lambda i, h, t: (0, h)),
                  pl.BlockSpec((LANES, LANES), lambda i, h, t: (0, h)),
                  const2((LANES, LANES)),
                  const2((CHUNK, CHUNK)),
                  pl.BlockSpec((6, LANES, LANES), lambda i, h, t: (0, 0, 0))],
        out_specs=pl.BlockSpec((None, tl, LANES), lambda i, h, t: (i, t, h)),
        out_shape=jax.ShapeDtypeStruct((b, lp, RWKV_DIM), BF16),
        scratch_shapes=[pltpu.VMEM((LANES, LANES), F32),
                        pltpu.VMEM((8, LANES), F32),
                        pltpu.VMEM((8, LORA_COLS), F32),
                        pltpu.VMEM((tl, LANES), F32),
                        pltpu.VMEM((6, tl, LANES), F32)],
        compiler_params=pltpu.CompilerParams(
            dimension_semantics=("parallel", "parallel", "arbitrary"), vmem_limit_bytes=VMEM_LIMIT),
        name="rwkv7_mix",
    )(p, p, p, p, prm, mu_l, w2p, a2p, g2, hsum, tri, lvl)


def _s5_kernel(u_ref, bm_ref, cm_ref, lam_ref, tab_ref, d_ref, gw_ref, gb_ref, o_ref,
               h0_scr, hs_scr, *, tl):
    t = pl.program_id(1)

    @pl.when(t == 0)
    def _():
        h0_scr[...] = jnp.zeros_like(h0_scr)

    n_lv = int(math.log2(S5_SUB))
    row_i = lax.broadcasted_iota(jnp.int32, (S5_SUB, LANES), 0)

    def sub_body(si, carry):
        r0 = pl.multiple_of(si * S5_SUB, S5_SUB)
        u = u_ref[pl.ds(r0, S5_SUB), :]
        bu = _mm(u, bm_ref[...])
        for lb in range(S5_LANES // LANES):
            ls = slice(lb * LANES, (lb + 1) * LANES)
            li = slice(S5_LANES + lb * LANES, S5_LANES + (lb + 1) * LANES)
            hr, hi = bu[:, ls], bu[:, li]
            for lv in range(n_lv):
                sh = 1 << lv
                lr = lam_ref[2 * lv:2 * lv + 1, ls]
                lm = lam_ref[2 * lv + 1:2 * lv + 2, ls]
                keep = row_i >= sh
                sr = jnp.where(keep, pltpu.roll(hr, sh, 0), 0.0)
                sm = jnp.where(keep, pltpu.roll(hi, sh, 0), 0.0)
                hr, hi = hr + (lr * sr - lm * sm), hi + (lr * sm + lm * sr)
            tr, tm = tab_ref[0, :, ls], tab_ref[1, :, ls]
            h0r, h0i = h0_scr[0:1, ls], h0_scr[1:2, ls]
            hr, hi = hr + (tr * h0r - tm * h0i), hi + (tr * h0i + tm * h0r)
            h0_scr[0:1, ls] = hr[S5_SUB - 1:S5_SUB, :]
            h0_scr[1:2, ls] = hi[S5_SUB - 1:S5_SUB, :]
            hs_scr[:, ls] = hr.astype(BF16)
            hs_scr[:, li] = hi.astype(BF16)
        y = jnp.dot(hs_scr[...], cm_ref[...], preferred_element_type=F32) + d_ref[...] * u
        y = 0.5 * y * (1.0 + jnp.tanh(0.7978845608028654 * (y + 0.044715 * (y * y * y))))
        z = _mm(y, gw_ref[...]) + gb_ref[...]
        o_ref[pl.ds(r0, S5_SUB), :] = (y * _sigmoid(z)).astype(o_ref.dtype)
        return carry

    lax.fori_loop(0, tl // S5_SUB, sub_body, 0)


def _s5(p, bm, cm, lam_pows, tab, d_skip, glu_w, glu_b, tl):
    b, lp, _ = p.shape
    c2 = lambda a: pl.BlockSpec(a.shape, lambda i, t: (0,) * a.ndim)
    return pl.pallas_call(
        functools.partial(_s5_kernel, tl=tl),
        grid=(b, lp // tl),
        in_specs=[pl.BlockSpec((None, tl, S5_DIM), lambda i, t: (i, t, RWKV_COLS // S5_DIM)),
                  c2(bm), c2(cm), c2(lam_pows), c2(tab), c2(d_skip), c2(glu_w), c2(glu_b)],
        out_specs=pl.BlockSpec((None, tl, S5_DIM), lambda i, t: (i, t, 0)),
        out_shape=jax.ShapeDtypeStruct((b, lp, S5_DIM), BF16),
        scratch_shapes=[pltpu.VMEM((8, S5_LANES), F32),
                        pltpu.VMEM((S5_SUB, 2 * S5_LANES), BF16)],
        compiler_params=pltpu.CompilerParams(
            dimension_semantics=("parallel", "arbitrary"), vmem_limit_bytes=VMEM_LIMIT),
        name="s5_mix",
    )(p, bm, cm, lam_pows, tab, d_skip, glu_w, glu_b)


def _out_proj_kernel(*refs, n_in):
    h_ref = refs[0]
    xs = refs[1:1 + n_in]
    ws = refs[1 + n_in:1 + 2 * n_in]
    o_ref = refs[1 + 2 * n_in]
    acc = h_ref[...]
    for x_ref, w_ref in zip(xs, ws):
        acc = acc + jnp.dot(x_ref[...], w_ref[...], preferred_element_type=F32)
    o_ref[...] = acc


def _out_proj(h, xs, ws, tl):
    b, lp, d = h.shape
    n_in = len(xs)
    in_specs = [pl.BlockSpec((None, tl, d), lambda i, t: (i, t, 0))]
    in_specs += [pl.BlockSpec((None, tl, x.shape[-1]), lambda i, t: (i, t, 0)) for x in xs]
    in_specs += [pl.BlockSpec(w.shape, lambda i, t: (0, 0)) for w in ws]
    return pl.pallas_call(
        functools.partial(_out_proj_kernel, n_in=n_in),
        grid=(b, lp // tl),
        in_specs=in_specs,
        out_specs=pl.BlockSpec((None, tl, d), lambda i, t: (i, t, 0)),
        out_shape=jax.ShapeDtypeStruct((b, lp, d), F32),
        compiler_params=pltpu.CompilerParams(
            dimension_semantics=("parallel", "parallel"), vmem_limit_bytes=VMEM_LIMIT),
        name="out_proj_residual",
    )(h, *xs, *ws)


def _ffn_kernel(h_ref, g_ref, wg_ref, wv_ref, cw_ref, cb_ref, wd_ref, o_ref,
                xn_scr, acc_scr, prev_scr, *, tl):
    t = pl.program_id(1)
    n_ch = D_FF // FF_CHUNK

    @pl.when(t == 0)
    def _():
        prev_scr[...] = jnp.zeros_like(prev_scr)

    x = h_ref[...]
    xn_scr[...] = _rms_rows(x, g_ref[...]).astype(BF16)
    acc_scr[...] = x

    def conv(hh, prev8, cw, cb):
        p6, p7 = prev8[6:7, :], prev8[7:8, :]
        s1 = _shift_rows(hh, 1, [p7])
        s2 = _shift_rows(hh, 2, [p6, p7])
        return cw[0:1, :] * s2 + cw[1:2, :] * s1 + cw[2:3, :] * hh + cb

    def body(c, carry):
        c0 = pl.multiple_of(c * FF_CHUNK, FF_CHUNK)
        xn = xn_scr[...]
        hg = jnp.dot(xn, wg_ref[:, pl.ds(c0, FF_CHUNK)], preferred_element_type=F32)
        hv = jnp.dot(xn, wv_ref[:, pl.ds(c0, FF_CHUNK)], preferred_element_type=F32)
        cg = conv(hg, prev_scr[0, c], cw_ref[0, c], cb_ref[0, c])
        cv = conv(hv, prev_scr[1, c], cw_ref[1, c], cb_ref[1, c])
        prev_scr[0, c] = hg[tl - 8:tl, :]
        prev_scr[1, c] = hv[tl - 8:tl, :]
        act = (cg * _sigmoid(cg) * cv).astype(BF16)
        acc_scr[...] += jnp.dot(act, wd_ref[pl.ds(c0, FF_CHUNK), :], preferred_element_type=F32)
        return carry

    lax.fori_loop(0, n_ch, body, 0)
    o_ref[...] = acc_scr[...]


def _ffn(h, gain, wg, wv, cw, cb, wd, tl):
    b, lp, d = h.shape
    n_ch = D_FF // FF_CHUNK
    full = lambda a: pl.BlockSpec(a.shape, lambda i, t: (0,) * a.ndim, pipeline_mode=pl.Buffered(1))
    return pl.pallas_call(
        functools.partial(_ffn_kernel, tl=tl),
        grid=(b, lp // tl),
        in_specs=[pl.BlockSpec((None, tl, d), lambda i, t: (i, t, 0)),
                  full(gain), full(wg), full(wv), full(cw), full(cb), full(wd)],
        out_specs=pl.BlockSpec((None, tl, d), lambda i, t: (i, t, 0)),
        out_shape=jax.ShapeDtypeStruct((b, lp, d), F32),
        scratch_shapes=[pltpu.VMEM((tl, d), BF16),
                        pltpu.VMEM((tl, d), F32),
                        pltpu.VMEM((2, n_ch, 8, FF_CHUNK), F32)],
        compiler_params=pltpu.CompilerParams(
            dimension_semantics=("parallel", "arbitrary"), vmem_limit_bytes=VMEM_LIMIT),
        name="conv_gated_mlp",
    )(h, gain, wg, wv, cw, cb, wd)


def _fox_in_kernel(x_ref, g_ref, w_ref, qg_ref, kg_ref, bf_ref, avg_ref, tri_ref,
                   pq_ref, pk_ref, oq_ref, ok_ref, q_ref, k_ref, v_ref, carry_scr, *, tl):
    t = pl.program_id(1)

    @pl.when(t == 0)
    def _():
        carry_scr[...] = jnp.zeros_like(carry_scr)

    xn = _rms_rows(x_ref[...], g_ref[...]).astype(BF16)
    v_ref[...] = jnp.dot(xn, w_ref[:, 2 * D_MODEL:3 * D_MODEL], preferred_element_type=F32).astype(BF16)
    f = jnp.dot(xn, w_ref[:, 3 * D_MODEL:], preferred_element_type=F32) + bf_ref[...]
    log_f = jnp.minimum(f, 0.0) - jnp.log(1.0 + jnp.exp(-jnp.abs(f)))
    cum = _cumsum_rows(tri_ref[...], log_f) + carry_scr[0:1, :]
    carry_scr[0:1, :] = cum[tl - 1:tl, :]
    c = cum * LOG2E
    c1 = c.astype(BF16)
    r1 = c - c1.astype(F32)
    c2 = r1.astype(BF16)
    c3 = (r1 - c2.astype(F32)).astype(BF16)
    c123 = jnp.concatenate([c1, c2, c3], axis=1)
    feat_q = jnp.dot(c123, pq_ref[...], preferred_element_type=F32) + oq_ref[...]
    feat_k = jnp.dot(c123, pk_ref[...], preferred_element_type=F32) + ok_ref[...]

    avg = avg_ref[...]
    lane = lax.broadcasted_iota(jnp.int32, (1, LANES), 1)
    m0 = lane < HEAD_DIM
    for off, gain_ref, feat, dst in ((0, qg_ref, feat_q, q_ref), (D_MODEL, kg_ref, feat_k, k_ref)):
        z_all = jnp.dot(xn, w_ref[:, off:off + D_MODEL], preferred_element_type=F32)
        for s in range(PAIRS_FOX):
            z = z_all[:, s * LANES:(s + 1) * LANES]
            ms = _split2_mm(z * z, avg)
            zn = z * lax.rsqrt(ms + RMS_EPS) * gain_ref[...]
            for hh in range(2):
                hs = slice((2 * s + hh) * LANES, (2 * s + hh + 1) * LANES)
                own = m0 if hh == 0 else jnp.logical_not(m0)
                dst[:, hs] = jnp.where(own, zn, feat[:, hs]).astype(BF16)


def _fox_in(h, gain, w, qg, kg, bf, avg, tri, pq, pk, oq, ok, tl):
    b, lp, d = h.shape
    c2 = lambda a: pl.BlockSpec(a.shape, lambda i, t: (0,) * a.ndim)
    act = lambda n: (pl.BlockSpec((None, tl, n), lambda i, t: (i, t, 0)),
                     jax.ShapeDtypeStruct((b, lp, n), BF16))
    specs, shapes = zip(act(2 * d), act(2 * d), act(d))
    consts = (gain, w, qg, kg, bf, avg, tri, pq, pk, oq, ok)
    return pl.pallas_call(
        functools.partial(_fox_in_kernel, tl=tl),
        grid=(b, lp // tl),
        in_specs=[pl.BlockSpec((None, tl, d), lambda i, t: (i, t, 0))] + [c2(a) for a in consts],
        out_specs=list(specs),
        out_shape=list(shapes),
        scratch_shapes=[pltpu.VMEM((8, LANES), F32)],
        compiler_params=pltpu.CompilerParams(
            dimension_semantics=("parallel", "arbitrary"), vmem_limit_bytes=VMEM_LIMIT),
        name="fox_in_proj",
    )(h, *consts)


def _fox_feature_consts():
    pq = np.zeros((3 * LANES, FOX_HEADS * LANES), np.float32)
    pk = np.zeros_like(pq)
    oq = np.zeros((1, FOX_HEADS * LANES), np.float32)
    ok = np.zeros_like(oq)
    for h in range(FOX_HEADS):
        base = h * LANES + HEAD_DIM * (1 - h % 2)
        for i in range(3):
            pq[i * LANES + h, base + i] = 1.0
            pk[i * LANES + h, base + 3 + i] = -1.0
            oq[0, base + 3 + i] = 1.0
            ok[0, base + i] = 1.0
    return jnp.asarray(pq, BF16), jnp.asarray(pk, BF16), jnp.asarray(oq), jnp.asarray(ok)


def _fox_attn_kernel(qi_ref, ki_ref, q_ref, k_ref, v_ref, o_ref, m_scr, acc_scr, *, ta):
    s = pl.program_id(2)
    qi, ki = qi_ref[s], ki_ref[s]

    @pl.when(ki == 0)
    def _():
        m_scr[...] = jnp.full_like(m_scr, -1e30)
        acc_scr[...] = jnp.zeros_like(acc_scr)

    lane = lax.broadcasted_iota(jnp.int32, (1, LANES), 1)
    m0 = lane < HEAD_DIM
    ones_lane = (lane == HEAD_DIM, lane == 0)

    rt = min(ta, ATTN_ROWS)
    units = [(r, h) for r in range(ta // rt) for h in range(2)]

    def step(masked):
        v = v_ref[...]
        va = [jnp.where(m0 if h == 0 else jnp.logical_not(m0), v,
                        jnp.where(ones_lane[h], 1.0, 0.0).astype(BF16)) for h in range(2)]

        def n_keys(r):
            return (r + 1) * rt if masked else ta

        def scores(u):
            r, h = u
            hs = slice(h * LANES, (h + 1) * LANES)
            return lax.dot_general(q_ref[r * rt:(r + 1) * rt, hs], k_ref[:n_keys(r), hs],
                                   (((1,), (1,)), ((), ())), preferred_element_type=F32)

        def finish(u, sc):
            r, h = u
            rows = slice(r * rt, (r + 1) * rt)
            if masked:
                rr = lax.broadcasted_iota(jnp.int32, sc.shape, 0) + r * rt
                cc = lax.broadcasted_iota(jnp.int32, sc.shape, 1)
                sc = jnp.where(cc <= rr, sc, -1e30)
            m_prev = m_scr[h, rows]
            m_new = jnp.maximum(m_prev, jnp.max(sc, axis=1, keepdims=True))
            p = jnp.exp2(sc - m_new).astype(BF16)
            alpha = jnp.exp2(m_prev - m_new)
            m_scr[h, rows] = m_new
            acc_scr[h, rows] = alpha * acc_scr[h, rows] + jnp.dot(p, va[h][:n_keys(r)],
                                                                  preferred_element_type=F32)

        sc_next = scores(units[0])
        for i, u in enumerate(units):
            sc = sc_next
            if i + 1 < len(units):
                sc_next = scores(units[i + 1])
            finish(u, sc)

    @pl.when(ki < qi)
    def _():
        step(False)

    @pl.when(ki == qi)
    def _():
        step(True)
        a0, a1 = acc_scr[0], acc_scr[1]
        o0 = a0 / a0[:, HEAD_DIM:HEAD_DIM + 1]
        o1 = a1 / a1[:, 0:1]
        o_ref[...] = jnp.where(m0, o0, o1).astype(o_ref.dtype)


def _fox_attn(q, k, v, ta):
    b, lp, d = v.shape
    nq = lp // ta
    qi_tab = np.array([i for i in range(nq) for _ in range(i + 1)], np.int32)
    ki_tab = np.array([j for i in range(nq) for j in range(i + 1)], np.int32)
    grid_spec = pltpu.PrefetchScalarGridSpec(
        num_scalar_prefetch=2,
        grid=(b, PAIRS_FOX, len(qi_tab)),
        in_specs=[pl.BlockSpec((None, ta, 2 * LANES), lambda i, h, s, qt, kt: (i, qt[s], h)),
                  pl.BlockSpec((None, ta, 2 * LANES), lambda i, h, s, qt, kt: (i, kt[s], h)),
                  pl.BlockSpec((None, ta, LANES), lambda i, h, s, qt, kt: (i, kt[s], h))],
        out_specs=pl.BlockSpec((None, ta, LANES), lambda i, h, s, qt, kt: (i, qt[s], h)),
        scratch_shapes=[pltpu.VMEM((2, ta, 1), F32),
                        pltpu.VMEM((2, ta, LANES), F32)],
    )
    return pl.pallas_call(
        functools.partial(_fox_attn_kernel, ta=ta),
        grid_spec=grid_spec,
        out_shape=jax.ShapeDtypeStruct((b, lp, d), BF16),
        compiler_params=pltpu.CompilerParams(
            dimension_semantics=("parallel", "parallel", "arbitrary"), vmem_limit_bytes=VMEM_LIMIT),
        name="fox_attention",
    )(jnp.asarray(qi_tab), jnp.asarray(ki_tab), q, k, v)


def _block_diag(blocks):
    g, r, c = blocks.shape
    eye = jnp.eye(g, dtype=blocks.dtype)
    return (eye[:, None, :, None] * blocks[:, :, None, :]).reshape(g * r, g * c)


def _cmul(ar, ai, br, bi):
    return ar * br - ai * bi, ar * bi + ai * br


def _s5_params(a_re, a_im, log_dt, b_re, b_im, c_re, c_im):
    dt = jnp.exp(log_dt)[:, None]
    mag = jnp.exp(a_re * dt)
    lam_re, lam_im = mag * jnp.cos(a_im * dt), mag * jnp.sin(a_im * dt)
    den = a_re * a_re + a_im * a_im
    z_re = ((lam_re - 1.0) * a_re + lam_im * a_im) / den
    z_im = (lam_im * a_re - (lam_re - 1.0) * a_im) / den
    bb_re = z_re[..., None] * b_re - z_im[..., None] * b_im
    bb_im = z_re[..., None] * b_im + z_im[..., None] * b_re
    to_in = lambda m: _block_diag(jnp.swapaxes(m, 1, 2))
    bm = jnp.concatenate([to_in(bb_re), to_in(bb_im)], axis=1).astype(BF16)
    to_out = lambda m: _block_diag(jnp.swapaxes(m, 1, 2))
    cm = jnp.concatenate([to_out(c_re), -to_out(c_im)], axis=0).astype(BF16)
    lr, li = lam_re.reshape(1, S5_LANES), lam_im.reshape(1, S5_LANES)
    n_lv = int(math.log2(S5_SUB))
    pows, tr, ti = [], lr, li
    pr, pi = lr, li
    for _ in range(n_lv):
        pows += [pr, pi]
        nr, ni = _cmul(tr, ti, pr, pi)
        tr, ti = jnp.concatenate([tr, nr], axis=0), jnp.concatenate([ti, ni], axis=0)
        pr, pi = _cmul(pr, pi, pr, pi)
    pows = jnp.concatenate(pows + [jnp.zeros((16 - 2 * n_lv, S5_LANES), F32)], axis=0)
    return bm, cm, pows, jnp.stack([tr, ti])


def _rwkv_consts():
    idx = np.arange(LANES)
    hsum = (idx[:, None] // HEAD_DIM == idx[None, :] // HEAD_DIM).astype(np.float32)
    t = idx % CHUNK
    lv = [(t[:, None] // 2 == t[None, :] // 2)]
    b = 2
    while b < CHUNK:
        lv.append((t[:, None] // (2 * b) == t[None, :] // (2 * b)) & (t[:, None] // b != t[None, :] // b))
        b *= 2
    tri = (np.arange(CHUNK)[None, :] <= np.arange(CHUNK)[:, None]).astype(np.float32)
    return (jnp.asarray(hsum, BF16), jnp.asarray(tri, BF16), jnp.asarray(np.stack(lv).astype(np.float32)))


def _time_tile(length):
    return 768 if length > 2048 else 128


def kernel(x, meta_tokens, mix_norm, ffn_norm, even_w_in, even_w_out, rwkv_mu, rwkv_w0, rwkv_w2, rwkv_a0, rwkv_a2, rwkv_g2, rwkv_k_k, rwkv_k_a, rwkv_r_k, rwkv_ln_w, rwkv_ln_b, s5_a_re, s5_a_im, s5_log_dt, s5_b_re, s5_b_im, s5_c_re, s5_c_im, s5_d, s5_glu_w, s5_glu_b, odd_w_in, odd_w_out, fox_b_f, fox_q_gain, fox_k_gain, ffn_w_up, ffn_conv_w, ffn_conv_b, ffn_w_down):
    bsz, seq, d = x.shape
    length = N_META + seq
    tl = _time_tile(length)
    lp = -(-length // tl) * tl
    meta = jnp.broadcast_to(meta_tokens[None].astype(x.dtype), (bsz, N_META, d))
    h = jnp.concatenate([meta, x, jnp.zeros((bsz, lp - length, d), x.dtype)], axis=1)

    hsum, tri64, lvl = _rwkv_consts()
    n_ch = D_FF // FF_CHUNK

    def ffn(h, layer):
        w_up = ffn_w_up[layer].astype(BF16)
        cw = ffn_conv_w[layer].reshape(3, 2, n_ch, FF_CHUNK).transpose(1, 2, 0, 3)
        cw = jnp.pad(cw, ((0, 0), (0, 0), (0, 5), (0, 0)))
        cb = ffn_conv_b[layer].reshape(2, n_ch, 1, FF_CHUNK)
        return _ffn(h, ffn_norm[layer][None], w_up[:, :D_FF], w_up[:, D_FF:], cw, cb,
                    ffn_w_down[layer].astype(BF16), tl)

    p = _rms_matmul(h, mix_norm[0][None], even_w_in[0].astype(BF16), tl, F32)
    pairs = lambda vec: vec.reshape(PAIRS_RWKV, 1, LANES)
    mu = rwkv_mu[0]
    prm = jnp.concatenate(
        [pairs(mu[:RWKV_DIM]), pairs(mu[RWKV_DIM:2 * RWKV_DIM]), pairs(mu[2 * RWKV_DIM:3 * RWKV_DIM]),
         pairs(rwkv_w0[0]), pairs(rwkv_a0[0]), pairs(rwkv_k_k[0]), pairs(rwkv_k_a[0]),
         pairs(rwkv_r_k[0].reshape(-1)), pairs(rwkv_ln_w[0]), pairs(rwkv_ln_b[0]),
         jnp.zeros((PAIRS_RWKV, 6, LANES), F32)], axis=1)
    zeros64 = jnp.zeros((64, RWKV_DIM), F32)
    w2p = jnp.concatenate([rwkv_w2[0], zeros64], axis=0).astype(BF16)
    a2p = jnp.concatenate([zeros64, rwkv_a2[0]], axis=0).astype(BF16)
    a_out = _rwkv(p, prm, mu[3 * RWKV_DIM:][None], w2p, a2p, rwkv_g2[0].astype(BF16),
                  hsum, tri64, lvl, tl, tl // CHUNK)
    bm, cm, pows, tab = _s5_params(s5_a_re[0], s5_a_im[0], s5_log_dt[0], s5_b_re[0], s5_b_im[0],
                                   s5_c_re[0], s5_c_im[0])
    b_out = _s5(p, bm, cm, pows, tab, s5_d[0][None], s5_glu_w[0].astype(BF16), s5_glu_b[0][None], tl)
    w_out = even_w_out[0].astype(BF16)
    h = _out_proj(h, [a_out, b_out], [w_out[:RWKV_DIM], w_out[RWKV_DIM:]], tl)
    h = ffn(h, 0)

    w_in = jnp.pad(odd_w_in[0], ((0, 0), (0, ODD_IN_PAD - odd_w_in.shape[-1]))).astype(BF16)
    scale = HEAD_DIM ** -0.5
    qg = jnp.tile(fox_q_gain[0], 2)[None] * (scale * LOG2E)
    kg = jnp.tile(fox_k_gain[0], 2)[None]
    bf = jnp.pad(fox_b_f[0], (0, LANES - FOX_HEADS))[None]
    tri_t = jnp.asarray(np.tril(np.ones((tl, tl), np.float32)), BF16)
    q, k, v = _fox_in(h, mix_norm[1][None], w_in, qg, kg, bf, hsum * (1.0 / HEAD_DIM), tri_t,
                      *_fox_feature_consts(), tl)
    o = _fox_attn(q, k, v, tl)
    h = _out_proj(h, [o], [odd_w_out[0].astype(BF16)], tl)
    h = ffn(h, 1)
    return h[:, N_META:N_META + seq]
```

```python
import functools
import math

import jax
import jax.numpy as jnp
import numpy as np
from jax import lax
from jax.experimental import pallas as pl
from jax.experimental.pallas import tpu as pltpu

F32 = jnp.float32
BF16 = jnp.bfloat16

D_MODEL = 1024
N_META = 16
RMS_EPS = 1e-6
GN_EPS = 64e-5

RWKV_HEADS = 12
HEAD_DIM = 64
RWKV_DIM = RWKV_HEADS * HEAD_DIM
LORA_COLS = 256
RWKV_COLS = 3 * RWKV_DIM + LORA_COLS
S5_DIM = 256
S5_GROUP = 16
S5_GROUPS = 16
S5_STATE = 64
S5_LANES = S5_GROUPS * S5_STATE
EVEN_IN = RWKV_COLS + S5_DIM
FOX_HEADS = 16
ODD_IN_PAD = 3 * D_MODEL + 128
D_FF = 2816
LANES = 128
PAIRS_RWKV = RWKV_DIM // LANES
PAIRS_FOX = D_MODEL // LANES
CHUNK = 64
S5_SUB = 128
FF_CHUNK = 256
VMEM_LIMIT = 56 * 1024 * 1024
LOG2E = 1.4426950408889634
ATTN_ROWS = 256
ATTN_DEPTH = 2


def _mm(a, b):
    return jnp.dot(a.astype(BF16), b.astype(BF16), preferred_element_type=F32)


def _mm_nt(a, b):
    return lax.dot_general(a.astype(BF16), b.astype(BF16), (((1,), (1,)), ((), ())),
                           preferred_element_type=F32)


def _mm_tn(a, b):
    return lax.dot_general(a.astype(BF16), b.astype(BF16), (((0,), (0,)), ((), ())),
                           preferred_element_type=F32)


def _split2_mm(x, m):
    hi = x.astype(BF16)
    lo = (x - hi.astype(F32)).astype(BF16)
    return (jnp.dot(hi, m, preferred_element_type=F32)
            + jnp.dot(lo, m, preferred_element_type=F32))


def _cumsum_rows(tri, x):
    h1 = x.astype(BF16)
    r1 = x - h1.astype(F32)
    h2 = r1.astype(BF16)
    h3 = (r1 - h2.astype(F32)).astype(BF16)
    w = x.shape[1]
    res = jnp.dot(tri, jnp.concatenate([h1, h2, h3], axis=1), preferred_element_type=F32)
    return res[:, :w] + res[:, w:2 * w] + res[:, 2 * w:]


def _softplus(z):
    return jnp.maximum(z, 0.0) + jnp.log(1.0 + jnp.exp(-jnp.abs(z)))


def _sigmoid(z):
    return 1.0 / (1.0 + jnp.exp(-z))


def _rms_rows(x, gain):
    ms = jnp.mean(x * x, axis=-1, keepdims=True)
    return x * lax.rsqrt(ms + RMS_EPS) * gain


def _shift_rows(x, k, prev_rows):
    rolled = pltpu.roll(x, k, 0)
    row = lax.broadcasted_iota(jnp.int32, x.shape, 0)
    for i in range(k):
        rolled = jnp.where(row == i, prev_rows[i], rolled)
    return rolled


def _rms_matmul_kernel(x_ref, g_ref, w_ref, o_ref, *, n_chunk):
    xn = _rms_rows(x_ref[...], g_ref[...]).astype(BF16)
    n = o_ref.shape[-1]
    for j in range(0, n, n_chunk):
        o_ref[:, j:j + n_chunk] = jnp.dot(xn, w_ref[:, j:j + n_chunk],
                                          preferred_element_type=F32).astype(o_ref.dtype)


def _rms_matmul(h, gain, w, tl, out_dtype):
    b, lp, d = h.shape
    n = w.shape[1]
    return pl.pallas_call(
        functools.partial(_rms_matmul_kernel, n_chunk=256),
        grid=(b, lp // tl),
        in_specs=[pl.BlockSpec((None, tl, d), lambda i, t: (i, t, 0)),
                  pl.BlockSpec((1, d), lambda i, t: (0, 0)),
                  pl.BlockSpec((d, n), lambda i, t: (0, 0))],
        out_specs=pl.BlockSpec((None, tl, n), lambda i, t: (i, t, 0)),
        out_shape=jax.ShapeDtypeStruct((b, lp, n), out_dtype),
        compiler_params=pltpu.CompilerParams(
            dimension_semantics=("parallel", "parallel"), vmem_limit_bytes=VMEM_LIMIT),
        name="rms_in_proj",
    )(h, gain, w)


def _rwkv_kernel(r_ref, k_ref, v_ref, lo_ref, prm_ref, mul_ref, w2_ref, a2_ref, g2_ref,
                 hsum_ref, tri_ref, lvl_ref, o_ref,
                 s_scr, prev_scr, prevl_scr, y_scr, q_scr, *, tl, group):
    t = pl.program_id(2)

    @pl.when(t == 0)
    def _():
        s_scr[...] = jnp.zeros_like(s_scr)
        prev_scr[...] = jnp.zeros_like(prev_scr)
        prevl_scr[...] = jnp.zeros_like(prevl_scr)

    prm = prm_ref[...]
    row = lambda i: prm[i:i + 1, :]
    hsum = hsum_ref[...]

    def tshift(x, prev, mu):
        prevx = _shift_rows(x, 1, [prev])
        return x + (prevx - x) * mu

    r_raw, k_raw, v_raw, lo_raw = r_ref[...], k_ref[...], v_ref[...], lo_ref[...]
    r = tshift(r_raw, prev_scr[0:1, :], row(0))
    k = tshift(k_raw, prev_scr[1:2, :], row(1))
    v = tshift(v_raw, prev_scr[2:3, :], row(2))
    lo = tshift(lo_raw, prevl_scr[0:1, :], mul_ref[...])
    prev_scr[0:1, :] = r_raw[tl - 1:tl, :]
    prev_scr[1:2, :] = k_raw[tl - 1:tl, :]
    prev_scr[2:3, :] = v_raw[tl - 1:tl, :]
    prevl_scr[0:1, :] = lo_raw[tl - 1:tl, :]

    lo01 = lo[:, :LANES]
    w_pre = row(3) + _mm(jnp.tanh(lo01), w2_ref[...])
    logw = -jnp.exp(-_softplus(-w_pre) - 0.5)
    a = _sigmoid(row(4) + _mm(lo01, a2_ref[...]))
    g = _mm(_sigmoid(lo[:, LANES:]), g2_ref[...])
    kk = k * row(5)
    kk = kk / jnp.maximum(jnp.sqrt(_split2_mm(kk * kk, hsum)), 1e-12)
    kf = k * (1.0 + (a - 1.0) * row(6))
    ka = kk * a
    bonus = _split2_mm(r * kf * row(7), hsum) * v

    q_scr[0] = r
    q_scr[1] = logw
    q_scr[2] = kf
    q_scr[3] = v
    q_scr[4] = kk
    q_scr[5] = ka

    lane = lax.broadcasted_iota(jnp.int32, (1, LANES), 1)
    m0 = lane < HEAD_DIM
    tri = tri_ref[...]
    ri = lax.broadcasted_iota(jnp.int32, (LANES, LANES), 0)
    ci = lax.broadcasted_iota(jnp.int32, (LANES, LANES), 1)
    rt_, ct_ = ri & (CHUNK - 1), ci & (CHUNK - 1)
    strict = ct_ < rt_
    incl = ct_ <= rt_
    eye = ri == ci

    def stack(x):
        return jnp.concatenate([jnp.where(m0, x, 0.0), jnp.where(m0, 0.0, x)], axis=0).astype(BF16)

    def group_pre(bases):
        rng = range(len(bases))
        ld = [[q_scr[i, pl.ds(b, CHUNK), :] for i in range(6)] for b in bases]
        cum = [_cumsum_rows(tri, ld[j][1]) for j in rng]
        ops = []
        for j in rng:
            rc, lw, kc, vc, kkc, kac = ld[j]
            cum_c = cum[j][CHUNK - 1:CHUNK, :]
            w_in = jnp.exp(cum[j])
            w_inv = jnp.exp(-cum[j])
            w_ex = jnp.exp(cum[j] - lw)
            w_end = jnp.exp(cum_c - cum[j])
            ops.append(dict(
                at=stack(-kkc * w_ex), rt=stack(rc * w_in), bt=stack(kac * w_inv), kt=stack(kc * w_inv),
                bh=stack(kac * w_end), kh=stack(kc * w_end), v=stack(vc), wc=jnp.exp(cum_c)))
        am = [_mm_nt(jnp.concatenate([o["at"], o["rt"]], axis=0), jnp.concatenate([o["bt"], o["kt"]], axis=0))
              for o in ops]
        l_ab = [jnp.where(strict, m[:LANES, :LANES], 0.0) for m in am]
        a_ak = [jnp.where(strict, m[:LANES, LANES:], 0.0).astype(BF16) for m in am]
        m_rb = [jnp.where(incl, m[LANES:, :LANES], 0.0).astype(BF16) for m in am]
        m_rk = [jnp.where(incl, m[LANES:, LANES:], 0.0).astype(BF16) for m in am]
        lv0 = lvl_ref[0]
        tm = [jnp.where(eye, 1.0, 0.0) + l * lv0 for l in l_ab]
        av = [_mm(a_ak[j], ops[j]["v"]) for j in rng]
        for lv in range(1, 6):
            b = 1 << lv
            lvm = lvl_ref[lv]
            if b < 8:
                z = [_mm(l_ab[j] * lvm, tm[j]) for j in rng]
                tm = [tm[j] + _mm(tm[j], z[j]) for j in rng]
            else:
                odd = lambda x: jnp.concatenate([x[s:s + b] for s in range(b, LANES, 2 * b)], axis=0)
                lvm_o = odd(lvm)
                z_o = [_mm(odd(l_ab[j]) * lvm_o, tm[j]) for j in rng]
                zero = jnp.zeros((b, LANES), F32)

                def spread(y, base=None):
                    out = []
                    for i, s in enumerate(range(0, LANES, b)):
                        piece = y[(i // 2) * b:(i // 2 + 1) * b] if i % 2 else None
                        if base is None:
                            out.append(piece if i % 2 else zero)
                        else:
                            out.append(base[s:s + b] + piece if i % 2 else base[s:s + b])
                    return jnp.concatenate(out, axis=0)

                upd = [_mm(odd(tm[j]), spread(z_o[j])) for j in rng]
                tm = [spread(upd[j], tm[j]) for j in rng]
        pq = [_mm(tm[j], jnp.concatenate([ops[j]["at"], av[j].astype(BF16)], axis=1)) for j in rng]
        p_s = [m[:, :LANES].astype(BF16) for m in pq]
        qv = [jnp.concatenate([pq[j][:, LANES:].astype(BF16), ops[j]["v"]], axis=0) for j in rng]
        rp = [ops[j]["rt"].astype(F32) + _mm(m_rb[j], p_s[j]) for j in rng]
        y0 = [_mm(jnp.concatenate([m_rb[j], m_rk[j]], axis=1), qv[j]) for j in rng]
        gm = [jnp.where(eye, ops[j]["wc"], 0.0) + _mm_tn(ops[j]["bh"], p_s[j]) for j in rng]
        hm = [_mm_tn(jnp.concatenate([ops[j]["bh"], ops[j]["kh"]], axis=0), qv[j]) for j in rng]
        return rp, y0, gm, hm

    def run_group(base):
        bases = [base + j * CHUNK for j in range(group)]
        rp, y0, gm, hm = group_pre(bases)
        s = s_scr[...]
        for j in range(group):
            ys = _mm(rp[j], s) + y0[j]
            y_scr[pl.ds(bases[j], CHUNK), :] = ys[:CHUNK] + ys[CHUNK:]
            s = _mm(gm[j], s) + hm[j]
        s_scr[...] = s

    n_groups = tl // (group * CHUNK)
    if n_groups == 1:
        run_group(0)
    else:
        def group_body(gi, carry):
            run_group(pl.multiple_of(gi * (group * CHUNK), group * CHUNK))
            return carry
        lax.fori_loop(0, n_groups, group_body, 0)

    y = y_scr[...]
    mean = _split2_mm(y, hsum) * (1.0 / HEAD_DIM)
    yc = y - mean
    var = _split2_mm(yc * yc, hsum) * (1.0 / HEAD_DIM)
    yn = yc * lax.rsqrt(var + GN_EPS) * row(8) + row(9)
    o_ref[...] = ((yn + bonus) * g).astype(o_ref.dtype)


def _rwkv(p, prm, mu_l, w2p, a2p, g2, hsum, tri, lvl, tl, group):
    b, lp, _ = p.shape
    nt = lp // tl
    blk = lambda off: pl.BlockSpec((None, tl, LANES), lambda i, h, t: (i, t, off + h))
    const2 = lambda shape: pl.BlockSpec(shape, lambda i, h, t: (0, 0))
    return pl.pallas_call(
        functools.partial(_rwkv_kernel, tl=tl, group=group),
        grid=(b, PAIRS_RWKV, nt),
        in_specs=[blk(0), blk(PAIRS_RWKV), blk(2 * PAIRS_RWKV),
                  pl.BlockSpec((None, tl, LORA_COLS), lambda i, h, t: (i, t, 3 * RWKV_DIM // LORA_COLS)),
                  pl.BlockSpec((None, 16, LANES), lambda i, h, t: (h, 0, 0)),
                  const2((1, LORA_COLS)),
                  pl.BlockSpec((LANES, LANES), lambda i, h, t: (0, h)),
                  pl.BlockSpec((LANES, LANES), lambda i, h, t: (0, h)),
                  pl.BlockSpec((LANES, LANES), lambda i, h, t: (0, h)),
                  const2((LANES, LANES)),
                  const2((CHUNK, CHUNK)),
                  pl.BlockSpec((6, LANES, LANES), lambda i, h, t: (0, 0, 0))],
        out_specs=pl.BlockSpec((None, tl, LANES), lambda i, h, t: (i, t, h)),
        out_shape=jax.ShapeDtypeStruct((b, lp, RWKV_DIM), BF16),
        scratch_shapes=[pltpu.VMEM((LANES, LANES), F32),
                        pltpu.VMEM((8, LANES), F32),
                        pltpu.VMEM((8, LORA_COLS), F32),
                        pltpu.VMEM((tl, LANES), F32),
                        pltpu.VMEM((6, tl, LANES), F32)],
        compiler_params=pltpu.CompilerParams(
            dimension_semantics=("parallel", "parallel", "arbitrary"), vmem_limit_bytes=VMEM_LIMIT),
        name="rwkv7_mix",
    )(p, p, p, p, prm, mu_l, w2p, a2p, g2, hsum, tri, lvl)


def _s5_kernel(u_ref, bm_ref, cm_ref, lam_ref, tab_ref, d_ref, gw_ref, gb_ref, o_ref,
               h0_scr, hs_scr, *, tl):
    t = pl.program_id(1)

    @pl.when(t == 0)
    def _():
        h0_scr[...] = jnp.zeros_like(h0_scr)

    n_lv = int(math.log2(S5_SUB))
    row_i = lax.broadcasted_iota(jnp.int32, (S5_SUB, LANES), 0)

    def sub_body(si, carry):
        r0 = pl.multiple_of(si * S5_SUB, S5_SUB)
        u = u_ref[pl.ds(r0, S5_SUB), :]
        bu = _mm(u, bm_ref[...])
        for lb in range(S5_LANES // LANES):
            ls = slice(lb * LANES, (lb + 1) * LANES)
            li = slice(S5_LANES + lb * LANES, S5_LANES + (lb + 1) * LANES)
            hr, hi = bu[:, ls], bu[:, li]
            for lv in range(n_lv):
                sh = 1 << lv
                lr = lam_ref[2 * lv:2 * lv + 1, ls]
                lm = lam_ref[2 * lv + 1:2 * lv + 2, ls]
                keep = row_i >= sh
                sr = jnp.where(keep, pltpu.roll(hr, sh, 0), 0.0)
                sm = jnp.where(keep, pltpu.roll(hi, sh, 0), 0.0)
                hr, hi = hr + (lr * sr - lm * sm), hi + (lr * sm + lm * sr)
            tr, tm = tab_ref[0, :, ls], tab_ref[1, :, ls]
            h0r, h0i = h0_scr[0:1, ls], h0_scr[1:2, ls]
            hr, hi = hr + (tr * h0r - tm * h0i), hi + (tr * h0i + tm * h0r)
            h0_scr[0:1, ls] = hr[S5_SUB - 1:S5_SUB, :]
            h0_scr[1:2, ls] = hi[S5_SUB - 1:S5_SUB, :]
            hs_scr[:, ls] = hr.astype(BF16)
            hs_scr[:, li] = hi.astype(BF16)
        y = jnp.dot(hs_scr[...], cm_ref[...], preferred_element_type=F32) + d_ref[...] * u
        y = 0.5 * y * (1.0 + jnp.tanh(0.7978845608028654 * (y + 0.044715 * (y * y * y))))
        z = _mm(y, gw_ref[...]) + gb_ref[...]
        o_ref[pl.ds(r0, S5_SUB), :] = (y * _sigmoid(z)).astype(o_ref.dtype)
        return carry

    lax.fori_loop(0, tl // S5_SUB, sub_body, 0)


def _s5(p, bm, cm, lam_pows, tab, d_skip, glu_w, glu_b, tl):
    b, lp, _ = p.shape
    c2 = lambda a: pl.BlockSpec(a.shape, lambda i, t: (0,) * a.ndim)
    return pl.pallas_call(
        functools.partial(_s5_kernel, tl=tl),
        grid=(b, lp // tl),
        in_specs=[pl.BlockSpec((None, tl, S5_DIM), lambda i, t: (i, t, RWKV_COLS // S5_DIM)),
                  c2(bm), c2(cm), c2(lam_pows), c2(tab), c2(d_skip), c2(glu_w), c2(glu_b)],
        out_specs=pl.BlockSpec((None, tl, S5_DIM), lambda i, t: (i, t, 0)),
        out_shape=jax.ShapeDtypeStruct((b, lp, S5_DIM), BF16),
        scratch_shapes=[pltpu.VMEM((8, S5_LANES), F32),
                        pltpu.VMEM((S5_SUB, 2 * S5_LANES), BF16)],
        compiler_params=pltpu.CompilerParams(
            dimension_semantics=("parallel", "arbitrary"), vmem_limit_bytes=VMEM_LIMIT),
        name="s5_mix",
    )(p, bm, cm, lam_pows, tab, d_skip, glu_w, glu_b)


def _out_proj_kernel(*refs, n_in):
    h_ref = refs[0]
    xs = refs[1:1 + n_in]
    ws = refs[1 + n_in:1 + 2 * n_in]
    o_ref = refs[1 + 2 * n_in]
    acc = h_ref[...]
    for x_ref, w_ref in zip(xs, ws):
        acc = acc + jnp.dot(x_ref[...], w_ref[...], preferred_element_type=F32)
    o_ref[...] = acc


def _out_proj(h, xs, ws, tl):
    b, lp, d = h.shape
    n_in = len(xs)
    in_specs = [pl.BlockSpec((None, tl, d), lambda i, t: (i, t, 0))]
    in_specs += [pl.BlockSpec((None, tl, x.shape[-1]), lambda i, t: (i, t, 0)) for x in xs]
    in_specs += [pl.BlockSpec(w.shape, lambda i, t: (0, 0)) for w in ws]
    return pl.pallas_call(
        functools.partial(_out_proj_kernel, n_in=n_in),
        grid=(b, lp // tl),
        in_specs=in_specs,
        out_specs=pl.BlockSpec((None, tl, d), lambda i, t: (i, t, 0)),
        out_shape=jax.ShapeDtypeStruct((b, lp, d), F32),
        compiler_params=pltpu.CompilerParams(
            dimension_semantics=("parallel", "parallel"), vmem_limit_bytes=VMEM_LIMIT),
        name="out_proj_residual",
    )(h, *xs, *ws)


def _ffn_kernel(h_ref, g_ref, wg_ref, wv_ref, cw_ref, cb_ref, wd_ref, o_ref,
                xn_scr, prev_scr, *, tl):
    t = pl.program_id(1)
    n_ch = D_FF // FF_CHUNK

    @pl.when(t == 0)
    def _():
        prev_scr[...] = jnp.zeros_like(prev_scr)

    x = h_ref[...]
    xn_scr[...] = _rms_rows(x, g_ref[...]).astype(BF16)
    o_ref[...] = x

    def conv(hh, prev8, cw, cb):
        p6, p7 = prev8[6:7, :], prev8[7:8, :]
        s1 = _shift_rows(hh, 1, [p7])
        s2 = _shift_rows(hh, 2, [p6, p7])
        return cw[0:1, :] * s2 + cw[1:2, :] * s1 + cw[2:3, :] * hh + cb

    def up(c):
        cs = slice(c * FF_CHUNK, (c + 1) * FF_CHUNK)
        xn = xn_scr[...]
        return (jnp.dot(xn, wg_ref[:, cs], preferred_element_type=F32),
                jnp.dot(xn, wv_ref[:, cs], preferred_element_type=F32))

    nxt = up(0)
    for c in range(n_ch):
        hg, hv = nxt
        if c + 1 < n_ch:
            nxt = up(c + 1)
        cg = conv(hg, prev_scr[0, c], cw_ref[0, c], cb_ref[0, c])
        cv = conv(hv, prev_scr[1, c], cw_ref[1, c], cb_ref[1, c])
        prev_scr[0, c] = hg[tl - 8:tl, :]
        prev_scr[1, c] = hv[tl - 8:tl, :]
        act = (cg * _sigmoid(cg) * cv).astype(BF16)
        o_ref[...] += jnp.dot(act, wd_ref[c * FF_CHUNK:(c + 1) * FF_CHUNK, :], preferred_element_type=F32)


def _ffn(h, gain, wg, wv, cw, cb, wd, tl):
    b, lp, d = h.shape
    n_ch = D_FF // FF_CHUNK
    full = lambda a: pl.BlockSpec(a.shape, lambda i, t: (0,) * a.ndim, pipeline_mode=pl.Buffered(1))
    return pl.pallas_call(
        functools.partial(_ffn_kernel, tl=tl),
        grid=(b, lp // tl),
        in_specs=[pl.BlockSpec((None, tl, d), lambda i, t: (i, t, 0)),
                  full(gain), full(wg), full(wv), full(cw), full(cb), full(wd)],
        out_specs=pl.BlockSpec((None, tl, d), lambda i, t: (i, t, 0)),
        out_shape=jax.ShapeDtypeStruct((b, lp, d), F32),
        scratch_shapes=[pltpu.VMEM((tl, d), BF16),
                        pltpu.VMEM((2, n_ch, 8, FF_CHUNK), F32)],
        compiler_params=pltpu.CompilerParams(
            dimension_semantics=("parallel", "arbitrary"), vmem_limit_bytes=VMEM_LIMIT),
        name="conv_gated_mlp",
    )(h, gain, wg, wv, cw, cb, wd)


def _fox_in_kernel(x_ref, g_ref, w_ref, qg_ref, kg_ref, bf_ref, avg_ref, tri_ref,
                   pq_ref, pk_ref, oq_ref, ok_ref, q_ref, k_ref, v_ref, carry_scr, *, tl):
    t = pl.program_id(1)

    @pl.when(t == 0)
    def _():
        carry_scr[...] = jnp.zeros_like(carry_scr)

    xn = _rms_rows(x_ref[...], g_ref[...]).astype(BF16)
    v_ref[...] = jnp.dot(xn, w_ref[:, 2 * D_MODEL:3 * D_MODEL], preferred_element_type=F32).astype(BF16)
    f = jnp.dot(xn, w_ref[:, 3 * D_MODEL:], preferred_element_type=F32) + bf_ref[...]
    log_f = jnp.minimum(f, 0.0) - jnp.log(1.0 + jnp.exp(-jnp.abs(f)))
    cum = _cumsum_rows(tri_ref[...], log_f) + carry_scr[0:1, :]
    carry_scr[0:1, :] = cum[tl - 1:tl, :]
    c = cum * LOG2E
    c1 = c.astype(BF16)
    r1 = c - c1.astype(F32)
    c2 = r1.astype(BF16)
    c3 = (r1 - c2.astype(F32)).astype(BF16)
    c123 = jnp.concatenate([c1, c2, c3], axis=1)
    feat_q = jnp.dot(c123, pq_ref[...], preferred_element_type=F32) + oq_ref[...]
    feat_k = jnp.dot(c123, pk_ref[...], preferred_element_type=F32) + ok_ref[...]

    avg = avg_ref[...]
    lane = lax.broadcasted_iota(jnp.int32, (1, LANES), 1)
    m0 = lane < HEAD_DIM
    for off, gain_ref, feat, dst in ((0, qg_ref, feat_q, q_ref), (D_MODEL, kg_ref, feat_k, k_ref)):
        z_all = jnp.dot(xn, w_ref[:, off:off + D_MODEL], preferred_element_type=F32)
        for s in range(PAIRS_FOX):
            z = z_all[:, s * LANES:(s + 1) * LANES]
            ms = _split2_mm(z * z, avg)
            zn = z * lax.rsqrt(ms + RMS_EPS) * gain_ref[...]
            for hh in range(2):
                hs = slice((2 * s + hh) * LANES, (2 * s + hh + 1) * LANES)
                own = m0 if hh == 0 else jnp.logical_not(m0)
                dst[:, hs] = jnp.where(own, zn, feat[:, hs]).astype(BF16)


def _fox_in(h, gain, w, qg, kg, bf, avg, tri, pq, pk, oq, ok, tl):
    b, lp, d = h.shape
    c2 = lambda a: pl.BlockSpec(a.shape, lambda i, t: (0,) * a.ndim)
    act = lambda n: (pl.BlockSpec((None, tl, n), lambda i, t: (i, t, 0)),
                     jax.ShapeDtypeStruct((b, lp, n), BF16))
    specs, shapes = zip(act(2 * d), act(2 * d), act(d))
    consts = (gain, w, qg, kg, bf, avg, tri, pq, pk, oq, ok)
    return pl.pallas_call(
        functools.partial(_fox_in_kernel, tl=tl),
        grid=(b, lp // tl),
        in_specs=[pl.BlockSpec((None, tl, d), lambda i, t: (i, t, 0))] + [c2(a) for a in consts],
        out_specs=list(specs),
        out_shape=list(shapes),
        scratch_shapes=[pltpu.VMEM((8, LANES), F32)],
        compiler_params=pltpu.CompilerParams(
            dimension_semantics=("parallel", "arbitrary"), vmem_limit_bytes=VMEM_LIMIT),
        name="fox_in_proj",
    )(h, *consts)


def _fox_feature_consts():
    pq = np.zeros((3 * LANES, FOX_HEADS * LANES), np.float32)
    pk = np.zeros_like(pq)
    oq = np.zeros((1, FOX_HEADS * LANES), np.float32)
    ok = np.zeros_like(oq)
    for h in range(FOX_HEADS):
        base = h * LANES + HEAD_DIM * (1 - h % 2)
        for i in range(3):
            pq[i * LANES + h, base + i] = 1.0
            pk[i * LANES + h, base + 3 + i] = -1.0
            oq[0, base + 3 + i] = 1.0
            ok[0, base + i] = 1.0
    return jnp.asarray(pq, BF16), jnp.asarray(pk, BF16), jnp.asarray(oq), jnp.asarray(ok)


def _fox_attn_kernel(q_ref, k_ref, v_ref, o_ref, m_scr, acc_scr, *, ta):
    qi = pl.program_id(2)
    m_scr[...] = jnp.full_like(m_scr, -1e30)
    acc_scr[...] = jnp.zeros_like(acc_scr)

    lane = lax.broadcasted_iota(jnp.int32, (1, LANES), 1)
    m0 = lane < HEAD_DIM
    ones_lane = (lane == HEAD_DIM, lane == 0)

    rt = min(ta, ATTN_ROWS)
    units = [(r, h) for r in range(ta // rt) for h in range(2)]

    def step(k0, masked):
        def n_keys(r):
            return (r + 1) * rt if masked else ta

        v = v_ref[pl.ds(k0, ta), :]
        va = [jnp.where(m0 if h == 0 else jnp.logical_not(m0), v,
                        jnp.where(ones_lane[h], 1.0, 0.0).astype(BF16)) for h in range(2)]

        def scores(u):
            r, h = u
            hs = slice(h * LANES, (h + 1) * LANES)
            return lax.dot_general(q_ref[r * rt:(r + 1) * rt, hs], k_ref[pl.ds(k0, n_keys(r)), hs],
                                   (((1,), (1,)), ((), ())), preferred_element_type=F32)

        def finish(u, sc):
            r, h = u
            rows = slice(r * rt, (r + 1) * rt)
            if masked:
                rr = lax.broadcasted_iota(jnp.int32, sc.shape, 0) + r * rt
                cc = lax.broadcasted_iota(jnp.int32, sc.shape, 1)
                sc = jnp.where(cc <= rr, sc, -1e30)
            m_prev = m_scr[h, rows]
            m_new = jnp.maximum(m_prev, jnp.max(sc, axis=1, keepdims=True))
            p = jnp.exp2(sc - m_new).astype(BF16)
            alpha = jnp.exp2(m_prev - m_new)
            m_scr[h, rows] = m_new
            acc_scr[h, rows] = alpha * acc_scr[h, rows] + jnp.dot(p, va[h][:n_keys(r)],
                                                                  preferred_element_type=F32)

        pending = [scores(u) for u in units[:ATTN_DEPTH]]
        for i, u in enumerate(units):
            sc = pending.pop(0)
            if i + ATTN_DEPTH < len(units):
                pending.append(scores(units[i + ATTN_DEPTH]))
            finish(u, sc)

    def body(ki, carry):
        step(pl.multiple_of(ki * ta, ta), False)
        return carry

    lax.fori_loop(0, qi, body, 0)
    step(pl.multiple_of(qi * ta, ta), True)
    a0, a1 = acc_scr[0], acc_scr[1]
    o0 = a0 / a0[:, HEAD_DIM:HEAD_DIM + 1]
    o1 = a1 / a1[:, 0:1]
    o_ref[...] = jnp.where(m0, o0, o1).astype(o_ref.dtype)


def _fox_attn(q, k, v, ta):
    b, lp, d = v.shape
    return pl.pallas_call(
        functools.partial(_fox_attn_kernel, ta=ta),
        grid=(b, PAIRS_FOX, lp // ta),
        in_specs=[pl.BlockSpec((None, ta, 2 * LANES), lambda i, h, t: (i, t, h)),
                  pl.BlockSpec((None, lp, 2 * LANES), lambda i, h, t: (i, 0, h)),
                  pl.BlockSpec((None, lp, LANES), lambda i, h, t: (i, 0, h))],
        out_specs=pl.BlockSpec((None, ta, LANES), lambda i, h, t: (i, t, h)),
        out_shape=jax.ShapeDtypeStruct((b, lp, d), BF16),
        scratch_shapes=[pltpu.VMEM((2, ta, 1), F32),
                        pltpu.VMEM((2, ta, LANES), F32)],
        compiler_params=pltpu.CompilerParams(
            dimension_semantics=("parallel", "parallel", "parallel"), vmem_limit_bytes=VMEM_LIMIT),
        name="fox_attention",
    )(q, k, v)


def _block_diag(blocks):
    g, r, c = blocks.shape
    eye = jnp.eye(g, dtype=blocks.dtype)
    return (eye[:, None, :, None] * blocks[:, :, None, :]).reshape(g * r, g * c)


def _cmul(ar, ai, br, bi):
    return ar * br - ai * bi, ar * bi + ai * br


def _s5_params(a_re, a_im, log_dt, b_re, b_im, c_re, c_im):
    dt = jnp.exp(log_dt)[:, None]
    mag = jnp.exp(a_re * dt)
    lam_re, lam_im = mag * jnp.cos(a_im * dt), mag * jnp.sin(a_im * dt)
    den = a_re * a_re + a_im * a_im
    z_re = ((lam_re - 1.0) * a_re + lam_im * a_im) / den
    z_im = (lam_im * a_re - (lam_re - 1.0) * a_im) / den
    bb_re = z_re[..., None] * b_re - z_im[..., None] * b_im
    bb_im = z_re[..., None] * b_im + z_im[..., None] * b_re
    to_in = lambda m: _block_diag(jnp.swapaxes(m, 1, 2))
    bm = jnp.concatenate([to_in(bb_re), to_in(bb_im)], axis=1).astype(BF16)
    to_out = lambda m: _block_diag(jnp.swapaxes(m, 1, 2))
    cm = jnp.concatenate([to_out(c_re), -to_out(c_im)], axis=0).astype(BF16)
    lr, li = lam_re.reshape(1, S5_LANES), lam_im.reshape(1, S5_LANES)
    n_lv = int(math.log2(S5_SUB))
    pows, tr, ti = [], lr, li
    pr, pi = lr, li
    for _ in range(n_lv):
        pows += [pr, pi]
        nr, ni = _cmul(tr, ti, pr, pi)
        tr, ti = jnp.concatenate([tr, nr], axis=0), jnp.concatenate([ti, ni], axis=0)
        pr, pi = _cmul(pr, pi, pr, pi)
    pows = jnp.concatenate(pows + [jnp.zeros((16 - 2 * n_lv, S5_LANES), F32)], axis=0)
    return bm, cm, pows, jnp.stack([tr, ti])


def _rwkv_consts():
    idx = np.arange(LANES)
    hsum = (idx[:, None] // HEAD_DIM == idx[None, :] // HEAD_DIM).astype(np.float32)
    t = idx % CHUNK
    lv = [(t[:, None] // 2 == t[None, :] // 2)]
    b = 2
    while b < CHUNK:
        lv.append((t[:, None] // (2 * b) == t[None, :] // (2 * b)) & (t[:, None] // b != t[None, :] // b))
        b *= 2
    tri = (np.arange(CHUNK)[None, :] <= np.arange(CHUNK)[:, None]).astype(np.float32)
    return (jnp.asarray(hsum, BF16), jnp.asarray(tri, BF16), jnp.asarray(np.stack(lv).astype(np.float32)))


def _time_tile(length):
    return 768 if length > 2048 else 128


def kernel(x, meta_tokens, mix_norm, ffn_norm, even_w_in, even_w_out, rwkv_mu, rwkv_w0, rwkv_w2, rwkv_a0, rwkv_a2, rwkv_g2, rwkv_k_k, rwkv_k_a, rwkv_r_k, rwkv_ln_w, rwkv_ln_b, s5_a_re, s5_a_im, s5_log_dt, s5_b_re, s5_b_im, s5_c_re, s5_c_im, s5_d, s5_glu_w, s5_glu_b, odd_w_in, odd_w_out, fox_b_f, fox_q_gain, fox_k_gain, ffn_w_up, ffn_conv_w, ffn_conv_b, ffn_w_down):
    bsz, seq, d = x.shape
    length = N_META + seq
    tl = _time_tile(length)
    lp = -(-length // tl) * tl
    meta = jnp.broadcast_to(meta_tokens[None].astype(x.dtype), (bsz, N_META, d))
    h = jnp.concatenate([meta, x, jnp.zeros((bsz, lp - length, d), x.dtype)], axis=1)

    hsum, tri64, lvl = _rwkv_consts()
    n_ch = D_FF // FF_CHUNK

    def ffn(h, layer):
        w_up = ffn_w_up[layer].astype(BF16)
        cw = ffn_conv_w[layer].reshape(3, 2, n_ch, FF_CHUNK).transpose(1, 2, 0, 3)
        cw = jnp.pad(cw, ((0, 0), (0, 0), (0, 5), (0, 0)))
        cb = ffn_conv_b[layer].reshape(2, n_ch, 1, FF_CHUNK)
        return _ffn(h, ffn_norm[layer][None], w_up[:, :D_FF], w_up[:, D_FF:], cw, cb,
                    ffn_w_down[layer].astype(BF16), tl)

    p = _rms_matmul(h, mix_norm[0][None], even_w_in[0].astype(BF16), tl, F32)
    pairs = lambda vec: vec.reshape(PAIRS_RWKV, 1, LANES)
    mu = rwkv_mu[0]
    prm = jnp.concatenate(
        [pairs(mu[:RWKV_DIM]), pairs(mu[RWKV_DIM:2 * RWKV_DIM]), pairs(mu[2 * RWKV_DIM:3 * RWKV_DIM]),
         pairs(rwkv_w0[0]), pairs(rwkv_a0[0]), pairs(rwkv_k_k[0]), pairs(rwkv_k_a[0]),
         pairs(rwkv_r_k[0].reshape(-1)), pairs(rwkv_ln_w[0]), pairs(rwkv_ln_b[0]),
         jnp.zeros((PAIRS_RWKV, 6, LANES), F32)], axis=1)
    zeros64 = jnp.zeros((64, RWKV_DIM), F32)
    w2p = jnp.concatenate([rwkv_w2[0], zeros64], axis=0).astype(BF16)
    a2p = jnp.concatenate([zeros64, rwkv_a2[0]], axis=0).astype(BF16)
    a_out = _rwkv(p, prm, mu[3 * RWKV_DIM:][None], w2p, a2p, rwkv_g2[0].astype(BF16),
                  hsum, tri64, lvl, tl, tl // CHUNK)
    bm, cm, pows, tab = _s5_params(s5_a_re[0], s5_a_im[0], s5_log_dt[0], s5_b_re[0], s5_b_im[0],
                                   s5_c_re[0], s5_c_im[0])
    b_out = _s5(p, bm, cm, pows, tab, s5_d[0][None], s5_glu_w[0].astype(BF16), s5_glu_b[0][None], tl)
    w_out = even_w_out[0].astype(BF16)
    h = _out_proj(h, [a_out, b_out], [w_out[:RWKV_DIM], w_out[RWKV_DIM:]], tl)
    h = ffn(h, 0)

    w_in = jnp.pad(odd_w_in[0], ((0, 0), (0, ODD_IN_PAD - odd_w_in.shape[-1]))).astype(BF16)
    scale = HEAD_DIM ** -0.5
    qg = jnp.tile(fox_q_gain[0], 2)[None] * (scale * LOG2E)
    kg = jnp.tile(fox_k_gain[0], 2)[None]
    bf = jnp.pad(fox_b_f[0], (0, LANES - FOX_HEADS))[None]
    tri_t = jnp.asarray(np.tril(np.ones((tl, tl), np.float32)), BF16)
    q, k, v = _fox_in(h, mix_norm[1][None], w_in, qg, kg, bf, hsum * (1.0 / HEAD_DIM), tri_t,
                      *_fox_feature_consts(), tl)
    o = _fox_attn(q, k, v, tl)
    h = _out_proj(h, [o], [odd_w_out[0].astype(BF16)], tl)
    h = ffn(h, 1)
    return h[:, N_META:N_META + seq]
```

```python
import functools
import math

import jax
import jax.numpy as jnp
import numpy as np
from jax import lax
from jax.experimental import pallas as pl
from jax.experimental.pallas import tpu as pltpu

F32 = jnp.float32
BF16 = jnp.bfloat16

D_MODEL = 1024
N_META = 16
RMS_EPS = 1e-6
GN_EPS = 64e-5

RWKV_HEADS = 12
HEAD_DIM = 64
RWKV_DIM = RWKV_HEADS * HEAD_DIM
LORA_COLS = 256
RWKV_COLS = 3 * RWKV_DIM + LORA_COLS
S5_DIM = 256
S5_GROUP = 16
S5_GROUPS = 16
S5_STATE = 64
S5_LANES = S5_GROUPS * S5_STATE
EVEN_IN = RWKV_COLS + S5_DIM
FOX_HEADS = 16
ODD_IN_PAD = 3 * D_MODEL + 128
D_FF = 2816
LANES = 128
PAIRS_RWKV = RWKV_DIM // LANES
PAIRS_FOX = D_MODEL // LANES
CHUNK = 64
S5_SUB = 128
S5_ROWS = 8
FF_CHUNK = 256
VMEM_LIMIT = 56 * 1024 * 1024
LOG2E = 1.4426950408889634
ATTN_ROWS = 256
ATTN_DEPTH = 2


def _mm(a, b):
    return jnp.dot(a.astype(BF16), b.astype(BF16), preferred_element_type=F32)


def _mm_nt(a, b):
    return lax.dot_general(a.astype(BF16), b.astype(BF16), (((1,), (1,)), ((), ())),
                           preferred_element_type=F32)


def _mm_tn(a, b):
    return lax.dot_general(a.astype(BF16), b.astype(BF16), (((0,), (0,)), ((), ())),
                           preferred_element_type=F32)


def _split2_mm(x, m):
    hi = x.astype(BF16)
    lo = (x - hi.astype(F32)).astype(BF16)
    return (jnp.dot(hi, m, preferred_element_type=F32)
            + jnp.dot(lo, m, preferred_element_type=F32))


def _cumsum_rows(tri, x):
    h1 = x.astype(BF16)
    r1 = x - h1.astype(F32)
    h2 = r1.astype(BF16)
    h3 = (r1 - h2.astype(F32)).astype(BF16)
    w = x.shape[1]
    res = jnp.dot(tri, jnp.concatenate([h1, h2, h3], axis=1), preferred_element_type=F32)
    return res[:, :w] + res[:, w:2 * w] + res[:, 2 * w:]


def _softplus(z):
    return jnp.maximum(z, 0.0) + jnp.log(1.0 + jnp.exp(-jnp.abs(z)))


def _sigmoid(z):
    return 1.0 / (1.0 + jnp.exp(-z))


def _rms_rows(x, gain):
    ms = jnp.mean(x * x, axis=-1, keepdims=True)
    return x * lax.rsqrt(ms + RMS_EPS) * gain


def _shift_rows(x, k, prev_rows):
    rolled = pltpu.roll(x, k, 0)
    row = lax.broadcasted_iota(jnp.int32, x.shape, 0)
    for i in range(k):
        rolled = jnp.where(row == i, prev_rows[i], rolled)
    return rolled


def _rms_matmul_kernel(x_ref, g_ref, w_ref, o_ref, *, n_chunk):
    xn = _rms_rows(x_ref[...], g_ref[...]).astype(BF16)
    n = o_ref.shape[-1]
    for j in range(0, n, n_chunk):
        o_ref[:, j:j + n_chunk] = jnp.dot(xn, w_ref[:, j:j + n_chunk],
                                          preferred_element_type=F32).astype(o_ref.dtype)


def _rms_matmul(h, gain, w, tl, out_dtype):
    b, lp, d = h.shape
    n = w.shape[1]
    return pl.pallas_call(
        functools.partial(_rms_matmul_kernel, n_chunk=256),
        grid=(b, lp // tl),
        in_specs=[pl.BlockSpec((None, tl, d), lambda i, t: (i, t, 0)),
                  pl.BlockSpec((1, d), lambda i, t: (0, 0)),
                  pl.BlockSpec((d, n), lambda i, t: (0, 0))],
        out_specs=pl.BlockSpec((None, tl, n), lambda i, t: (i, t, 0)),
        out_shape=jax.ShapeDtypeStruct((b, lp, n), out_dtype),
        compiler_params=pltpu.CompilerParams(
            dimension_semantics=("parallel", "parallel"), vmem_limit_bytes=VMEM_LIMIT),
        name="rms_in_proj",
    )(h, gain, w)


def _rwkv_kernel(r_ref, k_ref, v_ref, lo_ref, prm_ref, mul_ref, w2_ref, a2_ref, g2_ref,
                 hsum_ref, tri_ref, lvl_ref, o_ref,
                 s_scr, prev_scr, prevl_scr, y_scr, q_scr, *, tl, group):
    t = pl.program_id(2)

    @pl.when(t == 0)
    def _():
        s_scr[...] = jnp.zeros_like(s_scr)
        prev_scr[...] = jnp.zeros_like(prev_scr)
        prevl_scr[...] = jnp.zeros_like(prevl_scr)

    prm = prm_ref[...]
    row = lambda i: prm[i:i + 1, :]
    hsum = hsum_ref[...]

    def tshift(x, prev, mu):
        prevx = _shift_rows(x, 1, [prev])
        return x + (prevx - x) * mu

    r_raw, k_raw, v_raw, lo_raw = (ref[...].astype(F32) for ref in (r_ref, k_ref, v_ref, lo_ref))
    r = tshift(r_raw, prev_scr[0:1, :], row(0))
    k = tshift(k_raw, prev_scr[1:2, :], row(1))
    v = tshift(v_raw, prev_scr[2:3, :], row(2))
    lo = tshift(lo_raw, prevl_scr[0:1, :], mul_ref[...])
    prev_scr[0:1, :] = r_raw[tl - 1:tl, :]
    prev_scr[1:2, :] = k_raw[tl - 1:tl, :]
    prev_scr[2:3, :] = v_raw[tl - 1:tl, :]
    prevl_scr[0:1, :] = lo_raw[tl - 1:tl, :]

    lo01 = lo[:, :LANES]
    w_pre = row(3) + _mm(jnp.tanh(lo01), w2_ref[...])
    logw = -jnp.exp(-_softplus(-w_pre) - 0.5)
    a = _sigmoid(row(4) + _mm(lo01, a2_ref[...]))
    g = _mm(_sigmoid(lo[:, LANES:]), g2_ref[...])
    kk = k * row(5)
    kk = kk / jnp.maximum(jnp.sqrt(_split2_mm(kk * kk, hsum)), 1e-12)
    kf = k * (1.0 + (a - 1.0) * row(6))
    ka = kk * a
    bonus = _split2_mm(r * kf * row(7), hsum) * v

    q_scr[0] = r
    q_scr[1] = logw
    q_scr[2] = kf
    q_scr[3] = v
    q_scr[4] = kk
    q_scr[5] = ka

    lane = lax.broadcasted_iota(jnp.int32, (1, LANES), 1)
    m0 = lane < HEAD_DIM
    tri = tri_ref[...]
    ri = lax.broadcasted_iota(jnp.int32, (LANES, LANES), 0)
    ci = lax.broadcasted_iota(jnp.int32, (LANES, LANES), 1)
    rt_, ct_ = ri & (CHUNK - 1), ci & (CHUNK - 1)
    strict = ct_ < rt_
    incl = ct_ <= rt_
    eye = ri == ci

    def stack(x):
        return jnp.concatenate([jnp.where(m0, x, 0.0), jnp.where(m0, 0.0, x)], axis=0).astype(BF16)

    def group_pre(bases):
        rng = range(len(bases))
        ld = [[q_scr[i, pl.ds(b, CHUNK), :] for i in range(6)] for b in bases]
        cum = [_cumsum_rows(tri, ld[j][1]) for j in rng]
        ops = []
        for j in rng:
            rc, lw, kc, vc, kkc, kac = ld[j]
            cum_c = cum[j][CHUNK - 1:CHUNK, :]
            w_in = jnp.exp(cum[j])
            w_inv = jnp.exp(-cum[j])
            w_ex = jnp.exp(cum[j] - lw)
            w_end = jnp.exp(cum_c - cum[j])
            ops.append(dict(
                at=stack(-kkc * w_ex), rt=stack(rc * w_in), bt=stack(kac * w_inv), kt=stack(kc * w_inv),
                bh=stack(kac * w_end), kh=stack(kc * w_end), v=stack(vc), wc=jnp.exp(cum_c)))
        am = [_mm_nt(jnp.concatenate([o["at"], o["rt"]], axis=0), jnp.concatenate([o["bt"], o["kt"]], axis=0))
              for o in ops]
        l_ab = [jnp.where(strict, m[:LANES, :LANES], 0.0) for m in am]
        a_ak = [jnp.where(strict, m[:LANES, LANES:], 0.0).astype(BF16) for m in am]
        m_rb = [jnp.where(incl, m[LANES:, :LANES], 0.0).astype(BF16) for m in am]
        m_rk = [jnp.where(incl, m[LANES:, LANES:], 0.0).astype(BF16) for m in am]
        lv0 = lvl_ref[0]
        tm = [jnp.where(eye, 1.0, 0.0) + l * lv0 for l in l_ab]
        av = [_mm(a_ak[j], ops[j]["v"]) for j in rng]
        for lv in range(1, 6):
            b = 1 << lv
            lvm = lvl_ref[lv]
            if b < 8:
                z = [_mm(l_ab[j] * lvm, tm[j]) for j in rng]
                tm = [tm[j] + _mm(tm[j], z[j]) for j in rng]
            else:
                odd = lambda x: jnp.concatenate([x[s:s + b] for s in range(b, LANES, 2 * b)], axis=0)
                lvm_o = odd(lvm)
                z_o = [_mm(odd(l_ab[j]) * lvm_o, tm[j]) for j in rng]
                zero = jnp.zeros((b, LANES), F32)

                def spread(y, base=None):
                    out = []
                    for i, s in enumerate(range(0, LANES, b)):
                        piece = y[(i // 2) * b:(i // 2 + 1) * b] if i % 2 else None
                        if base is None:
                            out.append(piece if i % 2 else zero)
                        else:
                            out.append(base[s:s + b] + piece if i % 2 else base[s:s + b])
                    return jnp.concatenate(out, axis=0)

                upd = [_mm(odd(tm[j]), spread(z_o[j])) for j in rng]
                tm = [spread(upd[j], tm[j]) for j in rng]
        pq = [_mm(tm[j], jnp.concatenate([ops[j]["at"], av[j].astype(BF16)], axis=1)) for j in rng]
        p_s = [m[:, :LANES].astype(BF16) for m in pq]
        qv = [jnp.concatenate([pq[j][:, LANES:].astype(BF16), ops[j]["v"]], axis=0) for j in rng]
        rp = [ops[j]["rt"].astype(F32) + _mm(m_rb[j], p_s[j]) for j in rng]
        y0 = [_mm(jnp.concatenate([m_rb[j], m_rk[j]], axis=1), qv[j]) for j in rng]
        gm = [jnp.where(eye, ops[j]["wc"], 0.0) + _mm_tn(ops[j]["bh"], p_s[j]) for j in rng]
        hm = [_mm_tn(jnp.concatenate([ops[j]["bh"], ops[j]["kh"]], axis=0), qv[j]) for j in rng]
        return rp, y0, gm, hm

    def run_group(base):
        bases = [base + j * CHUNK for j in range(group)]
        rp, y0, gm, hm = group_pre(bases)
        s = s_scr[...]
        for j in range(group):
            ys = _mm(rp[j], s) + y0[j]
            y_scr[pl.ds(bases[j], CHUNK), :] = ys[:CHUNK] + ys[CHUNK:]
            s = _mm(gm[j], s) + hm[j]
        s_scr[...] = s

    n_groups = tl // (group * CHUNK)
    if n_groups == 1:
        run_group(0)
    else:
        def group_body(gi, carry):
            run_group(pl.multiple_of(gi * (group * CHUNK), group * CHUNK))
            return carry
        lax.fori_loop(0, n_groups, group_body, 0)

    y = y_scr[...]
    mean = _split2_mm(y, hsum) * (1.0 / HEAD_DIM)
    yc = y - mean
    var = _split2_mm(yc * yc, hsum) * (1.0 / HEAD_DIM)
    yn = yc * lax.rsqrt(var + GN_EPS) * row(8) + row(9)
    o_ref[...] = ((yn + bonus) * g).astype(o_ref.dtype)


def _rwkv(p, prm, mu_l, w2p, a2p, g2, hsum, tri, lvl, tl, group):
    b, lp, _ = p.shape
    nt = lp // tl
    blk = lambda off: pl.BlockSpec((None, tl, LANES), lambda i, h, t: (i, t, off + h))
    const2 = lambda shape: pl.BlockSpec(shape, lambda i, h, t: (0, 0))
    return pl.pallas_call(
        functools.partial(_rwkv_kernel, tl=tl, group=group),
        grid=(b, PAIRS_RWKV, nt),
        in_specs=[blk(0), blk(PAIRS_RWKV), blk(2 * PAIRS_RWKV),
                  pl.BlockSpec((None, tl, LORA_COLS), lambda i, h, t: (i, t, 3 * RWKV_DIM // LORA_COLS)),
                  pl.BlockSpec((None, 16, LANES), lambda i, h, t: (h, 0, 0)),
                  const2((1, LORA_COLS)),
                  pl.BlockSpec((LANES, LANES), lambda i, h, t: (0, h)),
                  pl.BlockSpec((LANES, LANES), lambda i, h, t: (0, h)),
                  pl.BlockSpec((LANES, LANES), lambda i, h, t: (0, h)),
                  const2((LANES, LANES)),
                  const2((CHUNK, CHUNK)),
                  pl.BlockSpec((6, LANES, LANES), lambda i, h, t: (0, 0, 0))],
        out_specs=pl.BlockSpec((None, tl, LANES), lambda i, h, t: (i, t, h)),
        out_shape=jax.ShapeDtypeStruct((b, lp, RWKV_DIM), BF16),
        scratch_shapes=[pltpu.VMEM((LANES, LANES), F32),
                        pltpu.VMEM((8, LANES), F32),
                        pltpu.VMEM((8, LORA_COLS), F32),
                        pltpu.VMEM((tl, LANES), F32),
                        pltpu.VMEM((6, tl, LANES), F32)],
        compiler_params=pltpu.CompilerParams(
            dimension_semantics=("parallel", "parallel", "arbitrary"), vmem_limit_bytes=VMEM_LIMIT),
        name="rwkv7_mix",
    )(p, p, p, p, prm, mu_l, w2p, a2p, g2, hsum, tri, lvl)


def _s5_kernel(u_ref, bm_ref, cm_ref, lam_ref, tab_ref, d_ref, gw_ref, gb_ref, o_ref,
               h0_scr, hs_scr, *, tl):
    t = pl.program_id(1)

    @pl.when(t == 0)
    def _():
        h0_scr[...] = jnp.zeros_like(h0_scr)

    n_lv = int(math.log2(S5_ROWS))
    n_grp = S5_SUB // S5_ROWS
    row_i = lax.broadcasted_iota(jnp.int32, (n_grp, S5_ROWS, LANES), 1)

    def sub_body(si, carry):
        r0 = pl.multiple_of(si * S5_SUB, S5_SUB)
        u = u_ref[pl.ds(r0, S5_SUB), :].astype(F32)
        bu = _mm(u, bm_ref[...])
        for lb in range(S5_LANES // LANES):
            ls = slice(lb * LANES, (lb + 1) * LANES)
            li = slice(S5_LANES + lb * LANES, S5_LANES + (lb + 1) * LANES)
            hr = bu[:, ls].reshape(n_grp, S5_ROWS, LANES)
            hi = bu[:, li].reshape(n_grp, S5_ROWS, LANES)
            for lv in range(n_lv):
                sh = 1 << lv
                lr = lam_ref[2 * lv:2 * lv + 1, ls]
                lm = lam_ref[2 * lv + 1:2 * lv + 2, ls]
                keep = row_i >= sh
                sr = jnp.where(keep, pltpu.roll(hr, sh, 1), 0.0)
                sm = jnp.where(keep, pltpu.roll(hi, sh, 1), 0.0)
                hr, hi = hr + (lr * sr - lm * sm), hi + (lr * sm + lm * sr)
            tr, tm = tab_ref[0, :, ls], tab_ref[1, :, ls]
            cr, ci = h0_scr[0:1, ls], h0_scr[1:2, ls]
            out_r, out_i = [], []
            for g in range(n_grp):
                gr = hr[g] + (tr * cr - tm * ci)
                gi = hi[g] + (tr * ci + tm * cr)
                cr, ci = gr[S5_ROWS - 1:S5_ROWS, :], gi[S5_ROWS - 1:S5_ROWS, :]
                out_r.append(gr)
                out_i.append(gi)
            h0_scr[0:1, ls] = cr
            h0_scr[1:2, ls] = ci
            hs_scr[:, ls] = jnp.concatenate(out_r, axis=0).astype(BF16)
            hs_scr[:, li] = jnp.concatenate(out_i, axis=0).astype(BF16)
        y = jnp.dot(hs_scr[...], cm_ref[...], preferred_element_type=F32) + d_ref[...] * u
        y = 0.5 * y * (1.0 + jnp.tanh(0.7978845608028654 * (y + 0.044715 * (y * y * y))))
        z = _mm(y, gw_ref[...]) + gb_ref[...]
        o_ref[pl.ds(r0, S5_SUB), :] = (y * _sigmoid(z)).astype(o_ref.dtype)
        return carry

    lax.fori_loop(0, tl // S5_SUB, sub_body, 0)


def _s5(p, bm, cm, lam_pows, tab, d_skip, glu_w, glu_b, tl):
    b, lp, _ = p.shape
    c2 = lambda a: pl.BlockSpec(a.shape, lambda i, t: (0,) * a.ndim)
    return pl.pallas_call(
        functools.partial(_s5_kernel, tl=tl),
        grid=(b, lp // tl),
        in_specs=[pl.BlockSpec((None, tl, S5_DIM), lambda i, t: (i, t, RWKV_COLS // S5_DIM)),
                  c2(bm), c2(cm), c2(lam_pows), c2(tab), c2(d_skip), c2(glu_w), c2(glu_b)],
        out_specs=pl.BlockSpec((None, tl, S5_DIM), lambda i, t: (i, t, 0)),
        out_shape=jax.ShapeDtypeStruct((b, lp, S5_DIM), BF16),
        scratch_shapes=[pltpu.VMEM((8, S5_LANES), F32),
                        pltpu.VMEM((S5_SUB, 2 * S5_LANES), BF16)],
        compiler_params=pltpu.CompilerParams(
            dimension_semantics=("parallel", "arbitrary"), vmem_limit_bytes=VMEM_LIMIT),
        name="s5_mix",
    )(p, bm, cm, lam_pows, tab, d_skip, glu_w, glu_b)


def _mix_ffn_kernel(*refs, tl, n_in):
    h_ref = refs[0]
    xs = refs[1:1 + n_in]
    ws = refs[1 + n_in:1 + 2 * n_in]
    g_ref, wg_ref, wv_ref, cw_ref, cb_ref, wd_ref, o_ref, xn_scr, prev_scr = refs[1 + 2 * n_in:]
    t = pl.program_id(1)
    n_ch = D_FF // FF_CHUNK

    @pl.when(t == 0)
    def _():
        prev_scr[...] = jnp.zeros_like(prev_scr)

    x = h_ref[...]
    for x_ref, w_ref in zip(xs, ws):
        x = x + jnp.dot(x_ref[...], w_ref[...], preferred_element_type=F32)
    xn_scr[...] = _rms_rows(x, g_ref[...]).astype(BF16)
    o_ref[...] = x

    def conv(hh, prev8, cw, cb):
        p6, p7 = prev8[6:7, :], prev8[7:8, :]
        s1 = _shift_rows(hh, 1, [p7])
        s2 = _shift_rows(hh, 2, [p6, p7])
        return cw[0:1, :] * s2 + cw[1:2, :] * s1 + cw[2:3, :] * hh + cb

    def up(c):
        cs = slice(c * FF_CHUNK, (c + 1) * FF_CHUNK)
        xn = xn_scr[...]
        return (jnp.dot(xn, wg_ref[:, cs], preferred_element_type=F32),
                jnp.dot(xn, wv_ref[:, cs], preferred_element_type=F32))

    nxt = up(0)
    for c in range(n_ch):
        hg, hv = nxt
        if c + 1 < n_ch:
            nxt = up(c + 1)
        cg = conv(hg, prev_scr[0, c], cw_ref[0, c], cb_ref[0, c])
        cv = conv(hv, prev_scr[1, c], cw_ref[1, c], cb_ref[1, c])
        prev_scr[0, c] = hg[tl - 8:tl, :]
        prev_scr[1, c] = hv[tl - 8:tl, :]
        act = (cg * _sigmoid(cg) * cv).astype(BF16)
        o_ref[...] += jnp.dot(act, wd_ref[c * FF_CHUNK:(c + 1) * FF_CHUNK, :], preferred_element_type=F32)


def _mix_ffn(h, xs, ws, gain, wg, wv, cw, cb, wd, tl):
    b, lp, d = h.shape
    n_ch = D_FF // FF_CHUNK
    full = lambda a: pl.BlockSpec(a.shape, lambda i, t: (0,) * a.ndim, pipeline_mode=pl.Buffered(1))
    row = lambda n: pl.BlockSpec((None, tl, n), lambda i, t: (i, t, 0))
    consts = (*ws, gain, wg, wv, cw, cb, wd)
    return pl.pallas_call(
        functools.partial(_mix_ffn_kernel, tl=tl, n_in=len(xs)),
        grid=(b, lp // tl),
        in_specs=[row(d)] + [row(x.shape[-1]) for x in xs] + [full(a) for a in consts],
        out_specs=row(d),
        out_shape=jax.ShapeDtypeStruct((b, lp, d), F32),
        scratch_shapes=[pltpu.VMEM((tl, d), BF16),
                        pltpu.VMEM((2, n_ch, 8, FF_CHUNK), F32)],
        compiler_params=pltpu.CompilerParams(
            dimension_semantics=("parallel", "arbitrary"), vmem_limit_bytes=VMEM_LIMIT),
        name="mix_out_mlp",
    )(h, *xs, *consts)


def _fox_in_kernel(x_ref, g_ref, w_ref, qg_ref, kg_ref, bf_ref, avg_ref, tri_ref,
                   pq_ref, pk_ref, oq_ref, ok_ref, q_ref, k_ref, v_ref, carry_scr, *, tl):
    t = pl.program_id(1)

    @pl.when(t == 0)
    def _():
        carry_scr[...] = jnp.zeros_like(carry_scr)

    xn = _rms_rows(x_ref[...], g_ref[...]).astype(BF16)
    v_ref[...] = jnp.dot(xn, w_ref[:, 2 * D_MODEL:3 * D_MODEL], preferred_element_type=F32).astype(BF16)
    f = jnp.dot(xn, w_ref[:, 3 * D_MODEL:], preferred_element_type=F32) + bf_ref[...]
    log_f = jnp.minimum(f, 0.0) - jnp.log(1.0 + jnp.exp(-jnp.abs(f)))
    cum = _cumsum_rows(tri_ref[...], log_f) + carry_scr[0:1, :]
    carry_scr[0:1, :] = cum[tl - 1:tl, :]
    c = cum * LOG2E
    c1 = c.astype(BF16)
    r1 = c - c1.astype(F32)
    c2 = r1.astype(BF16)
    c3 = (r1 - c2.astype(F32)).astype(BF16)
    c123 = jnp.concatenate([c1, c2, c3], axis=1)
    feat_q = jnp.dot(c123, pq_ref[...], preferred_element_type=F32) + oq_ref[...]
    feat_k = jnp.dot(c123, pk_ref[...], preferred_element_type=F32) + ok_ref[...]

    avg = avg_ref[...]
    lane = lax.broadcasted_iota(jnp.int32, (1, LANES), 1)
    m0 = lane < HEAD_DIM
    for off, gain_ref, feat, dst in ((0, qg_ref, feat_q, q_ref), (D_MODEL, kg_ref, feat_k, k_ref)):
        z_all = jnp.dot(xn, w_ref[:, off:off + D_MODEL], preferred_element_type=F32)
        for s in range(PAIRS_FOX):
            z = z_all[:, s * LANES:(s + 1) * LANES]
            ms = _split2_mm(z * z, avg)
            zn = z * lax.rsqrt(ms + RMS_EPS) * gain_ref[...]
            for hh in range(2):
                hs = slice((2 * s + hh) * LANES, (2 * s + hh + 1) * LANES)
                own = m0 if hh == 0 else jnp.logical_not(m0)
                dst[:, hs] = jnp.where(own, zn, feat[:, hs]).astype(BF16)


def _fox_in(h, gain, w, qg, kg, bf, avg, tri, pq, pk, oq, ok, tl):
    b, lp, d = h.shape
    c2 = lambda a: pl.BlockSpec(a.shape, lambda i, t: (0,) * a.ndim)
    act = lambda n: (pl.BlockSpec((None, tl, n), lambda i, t: (i, t, 0)),
                     jax.ShapeDtypeStruct((b, lp, n), BF16))
    specs, shapes = zip(act(2 * d), act(2 * d), act(d))
    consts = (gain, w, qg, kg, bf, avg, tri, pq, pk, oq, ok)
    return pl.pallas_call(
        functools.partial(_fox_in_kernel, tl=tl),
        grid=(b, lp // tl),
        in_specs=[pl.BlockSpec((None, tl, d), lambda i, t: (i, t, 0))] + [c2(a) for a in consts],
        out_specs=list(specs),
        out_shape=list(shapes),
        scratch_shapes=[pltpu.VMEM((8, LANES), F32)],
        compiler_params=pltpu.CompilerParams(
            dimension_semantics=("parallel", "arbitrary"), vmem_limit_bytes=VMEM_LIMIT),
        name="fox_in_proj",
    )(h, *consts)


def _fox_feature_consts():
    pq = np.zeros((3 * LANES, FOX_HEADS * LANES), np.float32)
    pk = np.zeros_like(pq)
    oq = np.zeros((1, FOX_HEADS * LANES), np.float32)
    ok = np.zeros_like(oq)
    for h in range(FOX_HEADS):
        base = h * LANES + HEAD_DIM * (1 - h % 2)
        for i in range(3):
            pq[i * LANES + h, base + i] = 1.0
            pk[i * LANES + h, base + 3 + i] = -1.0
            oq[0, base + 3 + i] = 1.0
            ok[0, base + i] = 1.0
    return jnp.asarray(pq, BF16), jnp.asarray(pk, BF16), jnp.asarray(oq), jnp.asarray(ok)


def _fox_attn_kernel(q_ref, k_ref, v_ref, o_ref, m_scr, acc_scr, *, ta):
    qi = pl.program_id(2)
    m_scr[...] = jnp.full_like(m_scr, -1e30)
    acc_scr[...] = jnp.zeros_like(acc_scr)

    lane = lax.broadcasted_iota(jnp.int32, (1, LANES), 1)
    m0 = lane < HEAD_DIM
    ones_lane = (lane == HEAD_DIM, lane == 0)

    rt = min(ta, ATTN_ROWS)

    def step(k0s, masked):
        units = [(kb, r, h) for kb in range(len(k0s)) for r in range(ta // rt) for h in range(2)]

        def n_keys(r):
            return (r + 1) * rt if masked else ta

        va = []
        for k0 in k0s:
            v = v_ref[pl.ds(k0, ta), :]
            va.append([jnp.where(m0 if h == 0 else jnp.logical_not(m0), v,
                                 jnp.where(ones_lane[h], 1.0, 0.0).astype(BF16)) for h in range(2)])

        def scores(u):
            kb, r, h = u
            hs = slice(h * LANES, (h + 1) * LANES)
            return lax.dot_general(q_ref[r * rt:(r + 1) * rt, hs], k_ref[pl.ds(k0s[kb], n_keys(r)), hs],
                                   (((1,), (1,)), ((), ())), preferred_element_type=F32)

        def finish(u, sc):
            kb, r, h = u
            rows = slice(r * rt, (r + 1) * rt)
            if masked:
                rr = lax.broadcasted_iota(jnp.int32, sc.shape, 0) + r * rt
                cc = lax.broadcasted_iota(jnp.int32, sc.shape, 1)
                sc = jnp.where(cc <= rr, sc, -1e30)
            m_prev = m_scr[h, rows]
            m_new = jnp.maximum(m_prev, jnp.max(sc, axis=1, keepdims=True))
            p = jnp.exp2(sc - m_new).astype(BF16)
            alpha = jnp.exp2(m_prev - m_new)
            m_scr[h, rows] = m_new
            acc_scr[h, rows] = alpha * acc_scr[h, rows] + jnp.dot(p, va[kb][h][:n_keys(r)],
                                                                  preferred_element_type=F32)

        pending = [scores(u) for u in units[:ATTN_DEPTH]]
        for i, u in enumerate(units):
            sc = pending.pop(0)
            if i + ATTN_DEPTH < len(units):
                pending.append(scores(units[i + ATTN_DEPTH]))
            finish(u, sc)

    def body(kj, carry):
        k0 = pl.multiple_of(kj * (2 * ta), 2 * ta)
        step([k0, k0 + ta], False)
        return carry

    lax.fori_loop(0, qi // 2, body, 0)

    @pl.when(qi % 2 == 1)
    def _():
        step([pl.multiple_of((qi - 1) * ta, ta)], False)

    step([pl.multiple_of(qi * ta, ta)], True)
    a0, a1 = acc_scr[0], acc_scr[1]
    o0 = a0 / a0[:, HEAD_DIM:HEAD_DIM + 1]
    o1 = a1 / a1[:, 0:1]
    o_ref[...] = jnp.where(m0, o0, o1).astype(o_ref.dtype)


def _fox_attn(q, k, v, ta):
    b, lp, d = v.shape
    return pl.pallas_call(
        functools.partial(_fox_attn_kernel, ta=ta),
        grid=(b, PAIRS_FOX, lp // ta),
        in_specs=[pl.BlockSpec((None, ta, 2 * LANES), lambda i, h, t: (i, t, h)),
                  pl.BlockSpec((None, lp, 2 * LANES), lambda i, h, t: (i, 0, h)),
                  pl.BlockSpec((None, lp, LANES), lambda i, h, t: (i, 0, h))],
        out_specs=pl.BlockSpec((None, ta, LANES), lambda i, h, t: (i, t, h)),
        out_shape=jax.ShapeDtypeStruct((b, lp, d), BF16),
        scratch_shapes=[pltpu.VMEM((2, ta, 1), F32),
                        pltpu.VMEM((2, ta, LANES), F32)],
        compiler_params=pltpu.CompilerParams(
            dimension_semantics=("parallel", "parallel", "parallel"), vmem_limit_bytes=VMEM_LIMIT),
        name="fox_attention",
    )(q, k, v)


def _block_diag(blocks):
    g, r, c = blocks.shape
    eye = jnp.eye(g, dtype=blocks.dtype)
    return (eye[:, None, :, None] * blocks[:, :, None, :]).reshape(g * r, g * c)


def _cmul(ar, ai, br, bi):
    return ar * br - ai * bi, ar * bi + ai * br


def _s5_params(a_re, a_im, log_dt, b_re, b_im, c_re, c_im):
    dt = jnp.exp(log_dt)[:, None]
    mag = jnp.exp(a_re * dt)
    lam_re, lam_im = mag * jnp.cos(a_im * dt), mag * jnp.sin(a_im * dt)
    den = a_re * a_re + a_im * a_im
    z_re = ((lam_re - 1.0) * a_re + lam_im * a_im) / den
    z_im = (lam_im * a_re - (lam_re - 1.0) * a_im) / den
    bb_re = z_re[..., None] * b_re - z_im[..., None] * b_im
    bb_im = z_re[..., None] * b_im + z_im[..., None] * b_re
    to_in = lambda m: _block_diag(jnp.swapaxes(m, 1, 2))
    bm = jnp.concatenate([to_in(bb_re), to_in(bb_im)], axis=1).astype(BF16)
    to_out = lambda m: _block_diag(jnp.swapaxes(m, 1, 2))
    cm = jnp.concatenate([to_out(c_re), -to_out(c_im)], axis=0).astype(BF16)
    lr, li = lam_re.reshape(1, S5_LANES), lam_im.reshape(1, S5_LANES)
    n_lv = int(math.log2(S5_ROWS))
    pows, tr, ti = [], lr, li
    pr, pi = lr, li
    for _ in range(n_lv):
        pows += [pr, pi]
        nr, ni = _cmul(tr, ti, pr, pi)
        tr, ti = jnp.concatenate([tr, nr], axis=0), jnp.concatenate([ti, ni], axis=0)
        pr, pi = _cmul(pr, pi, pr, pi)
    pows = jnp.concatenate(pows + [jnp.zeros((16 - 2 * n_lv, S5_LANES), F32)], axis=0)
    return bm, cm, pows, jnp.stack([tr, ti])


def _rwkv_consts():
    idx = np.arange(LANES)
    hsum = (idx[:, None] // HEAD_DIM == idx[None, :] // HEAD_DIM).astype(np.float32)
    t = idx % CHUNK
    lv = [(t[:, None] // 2 == t[None, :] // 2)]
    b = 2
    while b < CHUNK:
        lv.append((t[:, None] // (2 * b) == t[None, :] // (2 * b)) & (t[:, None] // b != t[None, :] // b))
        b *= 2
    tri = (np.arange(CHUNK)[None, :] <= np.arange(CHUNK)[:, None]).astype(np.float32)
    return (jnp.asarray(hsum, BF16), jnp.asarray(tri, BF16), jnp.asarray(np.stack(lv).astype(np.float32)))


def _time_tile(length):
    return 768 if length > 2048 else 128


def kernel(x, meta_tokens, mix_norm, ffn_norm, even_w_in, even_w_out, rwkv_mu, rwkv_w0, rwkv_w2, rwkv_a0, rwkv_a2, rwkv_g2, rwkv_k_k, rwkv_k_a, rwkv_r_k, rwkv_ln_w, rwkv_ln_b, s5_a_re, s5_a_im, s5_log_dt, s5_b_re, s5_b_im, s5_c_re, s5_c_im, s5_d, s5_glu_w, s5_glu_b, odd_w_in, odd_w_out, fox_b_f, fox_q_gain, fox_k_gain, ffn_w_up, ffn_conv_w, ffn_conv_b, ffn_w_down):
    bsz, seq, d = x.shape
    length = N_META + seq
    tl = _time_tile(length)
    lp = -(-length // tl) * tl
    meta = jnp.broadcast_to(meta_tokens[None].astype(x.dtype), (bsz, N_META, d))
    h = jnp.concatenate([meta, x, jnp.zeros((bsz, lp - length, d), x.dtype)], axis=1)

    hsum, tri64, lvl = _rwkv_consts()
    n_ch = D_FF // FF_CHUNK

    def mix_ffn(h, xs, ws, layer):
        w_up = ffn_w_up[layer].astype(BF16)
        cw = ffn_conv_w[layer].reshape(3, 2, n_ch, FF_CHUNK).transpose(1, 2, 0, 3)
        cw = jnp.pad(cw, ((0, 0), (0, 0), (0, 5), (0, 0)))
        cb = ffn_conv_b[layer].reshape(2, n_ch, 1, FF_CHUNK)
        return _mix_ffn(h, xs, ws, ffn_norm[layer][None], w_up[:, :D_FF], w_up[:, D_FF:], cw, cb,
                        ffn_w_down[layer].astype(BF16), tl)

    p = _rms_matmul(h, mix_norm[0][None], even_w_in[0].astype(BF16), tl, BF16)
    pairs = lambda vec: vec.reshape(PAIRS_RWKV, 1, LANES)
    mu = rwkv_mu[0]
    prm = jnp.concatenate(
        [pairs(mu[:RWKV_DIM]), pairs(mu[RWKV_DIM:2 * RWKV_DIM]), pairs(mu[2 * RWKV_DIM:3 * RWKV_DIM]),
         pairs(rwkv_w0[0]), pairs(rwkv_a0[0]), pairs(rwkv_k_k[0]), pairs(rwkv_k_a[0]),
         pairs(rwkv_r_k[0].reshape(-1)), pairs(rwkv_ln_w[0]), pairs(rwkv_ln_b[0]),
         jnp.zeros((PAIRS_RWKV, 6, LANES), F32)], axis=1)
    zeros64 = jnp.zeros((64, RWKV_DIM), F32)
    w2p = jnp.concatenate([rwkv_w2[0], zeros64], axis=0).astype(BF16)
    a2p = jnp.concatenate([zeros64, rwkv_a2[0]], axis=0).astype(BF16)
    a_out = _rwkv(p, prm, mu[3 * RWKV_DIM:][None], w2p, a2p, rwkv_g2[0].astype(BF16),
                  hsum, tri64, lvl, tl, tl // CHUNK)
    bm, cm, pows, tab = _s5_params(s5_a_re[0], s5_a_im[0], s5_log_dt[0], s5_b_re[0], s5_b_im[0],
                                   s5_c_re[0], s5_c_im[0])
    b_out = _s5(p, bm, cm, pows, tab, s5_d[0][None], s5_glu_w[0].astype(BF16), s5_glu_b[0][None], tl)
    w_out = even_w_out[0].astype(BF16)
    h = mix_ffn(h, [a_out, b_out], [w_out[:RWKV_DIM], w_out[RWKV_DIM:]], 0)

    w_in = jnp.pad(odd_w_in[0], ((0, 0), (0, ODD_IN_PAD - odd_w_in.shape[-1]))).astype(BF16)
    scale = HEAD_DIM ** -0.5
    qg = jnp.tile(fox_q_gain[0], 2)[None] * (scale * LOG2E)
    kg = jnp.tile(fox_k_gain[0], 2)[None]
    bf = jnp.pad(fox_b_f[0], (0, LANES - FOX_HEADS))[None]
    tri_t = jnp.asarray(np.tril(np.ones((tl, tl), np.float32)), BF16)
    q, k, v = _fox_in(h, mix_norm[1][None], w_in, qg, kg, bf, hsum * (1.0 / HEAD_DIM), tri_t,
                      *_fox_feature_consts(), tl)
    o = _fox_attn(q, k, v, tl)
    h = mix_ffn(h, [o], [odd_w_out[0].astype(BF16)], 1)
    return h[:, N_META:N_META + seq]
```

```python
import functools
import math

import jax
import jax.numpy as jnp
import numpy as np
from jax import lax
from jax.experimental import pallas as pl
from jax.experimental.pallas import tpu as pltpu

F32 = jnp.float32
BF16 = jnp.bfloat16

D_MODEL = 1024
N_META = 16
RMS_EPS = 1e-6
GN_EPS = 64e-5

RWKV_HEADS = 12
HEAD_DIM = 64
RWKV_DIM = RWKV_HEADS * HEAD_DIM
LORA_COLS = 256
RWKV_COLS = 3 * RWKV_DIM + LORA_COLS
S5_DIM = 256
S5_GROUP = 16
S5_GROUPS = 16
S5_STATE = 64
S5_LANES = S5_GROUPS * S5_STATE
EVEN_IN = RWKV_COLS + S5_DIM
FOX_HEADS = 16
ODD_IN_PAD = 3 * D_MODEL + 128
D_FF = 2816
LANES = 128
PAIRS_RWKV = RWKV_DIM // LANES
PAIRS_FOX = D_MODEL // LANES
CHUNK = 64
S5_SUB = 128
S5_ROWS = 8
FF_CHUNK = 256
FF_DEPTH = 2
VMEM_LIMIT = 56 * 1024 * 1024
LOG2E = 1.4426950408889634
ATTN_ROWS = 256
ATTN_DEPTH = 3


def _mm(a, b):
    return jnp.dot(a.astype(BF16), b.astype(BF16), preferred_element_type=F32)


def _mm_nt(a, b):
    return lax.dot_general(a.astype(BF16), b.astype(BF16), (((1,), (1,)), ((), ())),
                           preferred_element_type=F32)


def _mm_tn(a, b):
    return lax.dot_general(a.astype(BF16), b.astype(BF16), (((0,), (0,)), ((), ())),
                           preferred_element_type=F32)


def _split2_mm(x, m):
    hi = x.astype(BF16)
    lo = (x - hi.astype(F32)).astype(BF16)
    return (jnp.dot(hi, m, preferred_element_type=F32)
            + jnp.dot(lo, m, preferred_element_type=F32))


def _cumsum_rows(tri, x):
    h1 = x.astype(BF16)
    r1 = x - h1.astype(F32)
    h2 = r1.astype(BF16)
    h3 = (r1 - h2.astype(F32)).astype(BF16)
    w = x.shape[1]
    res = jnp.dot(tri, jnp.concatenate([h1, h2, h3], axis=1), preferred_element_type=F32)
    return res[:, :w] + res[:, w:2 * w] + res[:, 2 * w:]


def _softplus(z):
    return jnp.maximum(z, 0.0) + jnp.log(1.0 + jnp.exp(-jnp.abs(z)))


def _sigmoid(z):
    return 1.0 / (1.0 + jnp.exp(-z))


def _rms_rows(x, gain):
    ms = jnp.mean(x * x, axis=-1, keepdims=True)
    return x * lax.rsqrt(ms + RMS_EPS) * gain


def _shift_rows(x, k, prev_rows):
    rolled = pltpu.roll(x, k, 0)
    row = lax.broadcasted_iota(jnp.int32, x.shape, 0)
    for i in range(k):
        rolled = jnp.where(row == i, prev_rows[i], rolled)
    return rolled


def _rms_matmul_kernel(x_ref, g_ref, w_ref, o_ref, *, n_chunk):
    xn = _rms_rows(x_ref[...], g_ref[...]).astype(BF16)
    n = o_ref.shape[-1]
    for j in range(0, n, n_chunk):
        o_ref[:, j:j + n_chunk] = jnp.dot(xn, w_ref[:, j:j + n_chunk],
                                          preferred_element_type=F32).astype(o_ref.dtype)


def _rms_matmul(h, gain, w, tl, out_dtype):
    b, lp, d = h.shape
    n = w.shape[1]
    return pl.pallas_call(
        functools.partial(_rms_matmul_kernel, n_chunk=256),
        grid=(b, lp // tl),
        in_specs=[pl.BlockSpec((None, tl, d), lambda i, t: (i, t, 0)),
                  pl.BlockSpec((1, d), lambda i, t: (0, 0)),
                  pl.BlockSpec((d, n), lambda i, t: (0, 0))],
        out_specs=pl.BlockSpec((None, tl, n), lambda i, t: (i, t, 0)),
        out_shape=jax.ShapeDtypeStruct((b, lp, n), out_dtype),
        compiler_params=pltpu.CompilerParams(
            dimension_semantics=("parallel", "parallel"), vmem_limit_bytes=VMEM_LIMIT),
        name="rms_in_proj",
    )(h, gain, w)


def _rwkv_kernel(r_ref, k_ref, v_ref, lo_ref, prm_ref, mul_ref, w2_ref, a2_ref, g2_ref,
                 hsum_ref, tri_ref, lvl_ref, o_ref,
                 s_scr, prev_scr, prevl_scr, y_scr, q_scr, *, tl, group):
    t = pl.program_id(2)

    @pl.when(t == 0)
    def _():
        s_scr[...] = jnp.zeros_like(s_scr)
        prev_scr[...] = jnp.zeros_like(prev_scr)
        prevl_scr[...] = jnp.zeros_like(prevl_scr)

    prm = prm_ref[...]
    row = lambda i: prm[i:i + 1, :]
    hsum = hsum_ref[...]

    def tshift(x, prev, mu):
        prevx = _shift_rows(x, 1, [prev])
        return x + (prevx - x) * mu

    r_raw, k_raw, v_raw, lo_raw = (ref[...].astype(F32) for ref in (r_ref, k_ref, v_ref, lo_ref))
    r = tshift(r_raw, prev_scr[0:1, :], row(0))
    k = tshift(k_raw, prev_scr[1:2, :], row(1))
    v = tshift(v_raw, prev_scr[2:3, :], row(2))
    lo = tshift(lo_raw, prevl_scr[0:1, :], mul_ref[...])
    prev_scr[0:1, :] = r_raw[tl - 1:tl, :]
    prev_scr[1:2, :] = k_raw[tl - 1:tl, :]
    prev_scr[2:3, :] = v_raw[tl - 1:tl, :]
    prevl_scr[0:1, :] = lo_raw[tl - 1:tl, :]

    lo01 = lo[:, :LANES]
    w_pre = row(3) + _mm(jnp.tanh(lo01), w2_ref[...])
    logw = -jnp.exp(-_softplus(-w_pre) - 0.5)
    a = _sigmoid(row(4) + _mm(lo01, a2_ref[...]))
    g = _mm(_sigmoid(lo[:, LANES:]), g2_ref[...])
    kk = k * row(5)
    kk = kk / jnp.maximum(jnp.sqrt(_split2_mm(kk * kk, hsum)), 1e-12)
    kf = k * (1.0 + (a - 1.0) * row(6))
    ka = kk * a
    bonus = _split2_mm(r * kf * row(7), hsum) * v

    q_scr[0] = r
    q_scr[1] = logw
    q_scr[2] = kf
    q_scr[3] = v
    q_scr[4] = kk
    q_scr[5] = ka

    lane = lax.broadcasted_iota(jnp.int32, (1, LANES), 1)
    m0 = lane < HEAD_DIM
    tri = tri_ref[...]
    ri = lax.broadcasted_iota(jnp.int32, (LANES, LANES), 0)
    ci = lax.broadcasted_iota(jnp.int32, (LANES, LANES), 1)
    rt_, ct_ = ri & (CHUNK - 1), ci & (CHUNK - 1)
    strict = ct_ < rt_
    incl = ct_ <= rt_
    eye = ri == ci

    def stack(x):
        return jnp.concatenate([jnp.where(m0, x, 0.0), jnp.where(m0, 0.0, x)], axis=0).astype(BF16)

    def group_pre(bases):
        rng = range(len(bases))
        ld = [[q_scr[i, pl.ds(b, CHUNK), :] for i in range(6)] for b in bases]
        cum = [_cumsum_rows(tri, ld[j][1]) for j in rng]
        ops = []
        for j in rng:
            rc, lw, kc, vc, kkc, kac = ld[j]
            cum_c = cum[j][CHUNK - 1:CHUNK, :]
            w_in = jnp.exp(cum[j])
            w_inv = jnp.exp(-cum[j])
            w_ex = jnp.exp(cum[j] - lw)
            w_end = jnp.exp(cum_c - cum[j])
            ops.append(dict(
                at=stack(-kkc * w_ex), rt=stack(rc * w_in), bt=stack(kac * w_inv), kt=stack(kc * w_inv),
                bh=stack(kac * w_end), kh=stack(kc * w_end), v=stack(vc), wc=jnp.exp(cum_c)))
        am = [_mm_nt(jnp.concatenate([o["at"], o["rt"]], axis=0), jnp.concatenate([o["bt"], o["kt"]], axis=0))
              for o in ops]
        l_ab = [jnp.where(strict, m[:LANES, :LANES], 0.0) for m in am]
        a_ak = [jnp.where(strict, m[:LANES, LANES:], 0.0).astype(BF16) for m in am]
        m_rb = [jnp.where(incl, m[LANES:, :LANES], 0.0).astype(BF16) for m in am]
        m_rk = [jnp.where(incl, m[LANES:, LANES:], 0.0).astype(BF16) for m in am]
        lv0 = lvl_ref[0]
        tm = [jnp.where(eye, 1.0, 0.0) + l * lv0 for l in l_ab]
        av = [_mm(a_ak[j], ops[j]["v"]) for j in rng]
        for lv in range(1, 6):
            b = 1 << lv
            lvm = lvl_ref[lv]
            if b < 8:
                z = [_mm(l_ab[j] * lvm, tm[j]) for j in rng]
                tm = [tm[j] + _mm(tm[j], z[j]) for j in rng]
            else:
                odd = lambda x: jnp.concatenate([x[s:s + b] for s in range(b, LANES, 2 * b)], axis=0)
                lvm_o = odd(lvm)
                z_o = [_mm(odd(l_ab[j]) * lvm_o, tm[j]) for j in rng]
                zero = jnp.zeros((b, LANES), F32)

                def spread(y, base=None):
                    out = []
                    for i, s in enumerate(range(0, LANES, b)):
                        piece = y[(i // 2) * b:(i // 2 + 1) * b] if i % 2 else None
                        if base is None:
                            out.append(piece if i % 2 else zero)
                        else:
                            out.append(base[s:s + b] + piece if i % 2 else base[s:s + b])
                    return jnp.concatenate(out, axis=0)

                upd = [_mm(odd(tm[j]), spread(z_o[j])) for j in rng]
                tm = [spread(upd[j], tm[j]) for j in rng]
        pq = [_mm(tm[j], jnp.concatenate([ops[j]["at"], av[j].astype(BF16)], axis=1)) for j in rng]
        p_s = [m[:, :LANES].astype(BF16) for m in pq]
        qv = [jnp.concatenate([pq[j][:, LANES:].astype(BF16), ops[j]["v"]], axis=0) for j in rng]
        rp = [ops[j]["rt"].astype(F32) + _mm(m_rb[j], p_s[j]) for j in rng]
        y0 = [_mm(jnp.concatenate([m_rb[j], m_rk[j]], axis=1), qv[j]) for j in rng]
        gm = [jnp.where(eye, ops[j]["wc"], 0.0) + _mm_tn(ops[j]["bh"], p_s[j]) for j in rng]
        hm = [_mm_tn(jnp.concatenate([ops[j]["bh"], ops[j]["kh"]], axis=0), qv[j]) for j in rng]
        return rp, y0, gm, hm

    def run_group(base):
        bases = [base + j * CHUNK for j in range(group)]
        rp, y0, gm, hm = group_pre(bases)
        s = s_scr[...]
        for j in range(group):
            ys = _mm(rp[j], s) + y0[j]
            y_scr[pl.ds(bases[j], CHUNK), :] = ys[:CHUNK] + ys[CHUNK:]
            s = _mm(gm[j], s) + hm[j]
        s_scr[...] = s

    n_groups = tl // (group * CHUNK)
    if n_groups == 1:
        run_group(0)
    else:
        def group_body(gi, carry):
            run_group(pl.multiple_of(gi * (group * CHUNK), group * CHUNK))
            return carry
        lax.fori_loop(0, n_groups, group_body, 0)

    y = y_scr[...]
    mean = _split2_mm(y, hsum) * (1.0 / HEAD_DIM)
    yc = y - mean
    var = _split2_mm(yc * yc, hsum) * (1.0 / HEAD_DIM)
    yn = yc * lax.rsqrt(var + GN_EPS) * row(8) + row(9)
    o_ref[...] = ((yn + bonus) * g).astype(o_ref.dtype)


def _rwkv(p, prm, mu_l, w2p, a2p, g2, hsum, tri, lvl, tl, group):
    b, lp, _ = p.shape
    nt = lp // tl
    blk = lambda off: pl.BlockSpec((None, tl, LANES), lambda i, h, t: (i, t, off + h))
    const2 = lambda shape: pl.BlockSpec(shape, lambda i, h, t: (0, 0))
    return pl.pallas_call(
        functools.partial(_rwkv_kernel, tl=tl, group=group),
        grid=(b, PAIRS_RWKV, nt),
        in_specs=[blk(0), blk(PAIRS_RWKV), blk(2 * PAIRS_RWKV),
                  pl.BlockSpec((None, tl, LORA_COLS), lambda i, h, t: (i, t, 3 * RWKV_DIM // LORA_COLS)),
                  pl.BlockSpec((None, 16, LANES), lambda i, h, t: (h, 0, 0)),
                  const2((1, LORA_COLS)),
                  pl.BlockSpec((LANES, LANES), lambda i, h, t: (0, h)),
                  pl.BlockSpec((LANES, LANES), lambda i, h, t: (0, h)),
                  pl.BlockSpec((LANES, LANES), lambda i, h, t: (0, h)),
                  const2((LANES, LANES)),
                  const2((CHUNK, CHUNK)),
                  pl.BlockSpec((6, LANES, LANES), lambda i, h, t: (0, 0, 0))],
        out_specs=pl.BlockSpec((None, tl, LANES), lambda i, h, t: (i, t, h)),
        out_shape=jax.ShapeDtypeStruct((b, lp, RWKV_DIM), BF16),
        scratch_shapes=[pltpu.VMEM((LANES, LANES), F32),
                        pltpu.VMEM((8, LANES), F32),
                        pltpu.VMEM((8, LORA_COLS), F32),
                        pltpu.VMEM((tl, LANES), F32),
                        pltpu.VMEM((6, tl, LANES), F32)],
        compiler_params=pltpu.CompilerParams(
            dimension_semantics=("parallel", "parallel", "arbitrary"), vmem_limit_bytes=VMEM_LIMIT),
        name="rwkv7_mix",
    )(p, p, p, p, prm, mu_l, w2p, a2p, g2, hsum, tri, lvl)


def _s5_kernel(u_ref, bm_ref, cm_ref, lam_ref, tab_ref, d_ref, gw_ref, gb_ref, o_ref,
               h0_scr, hs_scr, *, tl):
    t = pl.program_id(1)

    @pl.when(t == 0)
    def _():
        h0_scr[...] = jnp.zeros_like(h0_scr)

    n_lv = int(math.log2(S5_ROWS))
    n_grp = S5_SUB // S5_ROWS
    row_i = lax.broadcasted_iota(jnp.int32, (n_grp, S5_ROWS, LANES), 1)

    def sub_body(si, carry):
        r0 = pl.multiple_of(si * S5_SUB, S5_SUB)
        u = u_ref[pl.ds(r0, S5_SUB), :].astype(F32)
        bu = _mm(u, bm_ref[...])
        for lb in range(S5_LANES // LANES):
            ls = slice(lb * LANES, (lb + 1) * LANES)
            li = slice(S5_LANES + lb * LANES, S5_LANES + (lb + 1) * LANES)
            hr = bu[:, ls].reshape(n_grp, S5_ROWS, LANES)
            hi = bu[:, li].reshape(n_grp, S5_ROWS, LANES)
            for lv in range(n_lv):
                sh = 1 << lv
                lr = lam_ref[2 * lv:2 * lv + 1, ls]
                lm = lam_ref[2 * lv + 1:2 * lv + 2, ls]
                keep = row_i >= sh
                sr = jnp.where(keep, pltpu.roll(hr, sh, 1), 0.0)
                sm = jnp.where(keep, pltpu.roll(hi, sh, 1), 0.0)
                hr, hi = hr + (lr * sr - lm * sm), hi + (lr * sm + lm * sr)
            tr, tm = tab_ref[0, :, ls], tab_ref[1, :, ls]
            cr, ci = h0_scr[0:1, ls], h0_scr[1:2, ls]
            out_r, out_i = [], []
            for g in range(n_grp):
                gr = hr[g] + (tr * cr - tm * ci)
                gi = hi[g] + (tr * ci + tm * cr)
                cr, ci = gr[S5_ROWS - 1:S5_ROWS, :], gi[S5_ROWS - 1:S5_ROWS, :]
                out_r.append(gr)
                out_i.append(gi)
            h0_scr[0:1, ls] = cr
            h0_scr[1:2, ls] = ci
            hs_scr[:, ls] = jnp.concatenate(out_r, axis=0).astype(BF16)
            hs_scr[:, li] = jnp.concatenate(out_i, axis=0).astype(BF16)
        y = jnp.dot(hs_scr[...], cm_ref[...], preferred_element_type=F32) + d_ref[...] * u
        y = 0.5 * y * (1.0 + jnp.tanh(0.7978845608028654 * (y + 0.044715 * (y * y * y))))
        z = _mm(y, gw_ref[...]) + gb_ref[...]
        o_ref[pl.ds(r0, S5_SUB), :] = (y * _sigmoid(z)).astype(o_ref.dtype)
        return carry

    lax.fori_loop(0, tl // S5_SUB, sub_body, 0)


def _s5(p, bm, cm, lam_pows, tab, d_skip, glu_w, glu_b, tl):
    b, lp, _ = p.shape
    c2 = lambda a: pl.BlockSpec(a.shape, lambda i, t: (0,) * a.ndim)
    return pl.pallas_call(
        functools.partial(_s5_kernel, tl=tl),
        grid=(b, lp // tl),
        in_specs=[pl.BlockSpec((None, tl, S5_DIM), lambda i, t: (i, t, RWKV_COLS // S5_DIM)),
                  c2(bm), c2(cm), c2(lam_pows), c2(tab), c2(d_skip), c2(glu_w), c2(glu_b)],
        out_specs=pl.BlockSpec((None, tl, S5_DIM), lambda i, t: (i, t, 0)),
        out_shape=jax.ShapeDtypeStruct((b, lp, S5_DIM), BF16),
        scratch_shapes=[pltpu.VMEM((8, S5_LANES), F32),
                        pltpu.VMEM((S5_SUB, 2 * S5_LANES), BF16)],
        compiler_params=pltpu.CompilerParams(
            dimension_semantics=("parallel", "arbitrary"), vmem_limit_bytes=VMEM_LIMIT),
        name="s5_mix",
    )(p, bm, cm, lam_pows, tab, d_skip, glu_w, glu_b)


def _mix_ffn_kernel(*refs, tl, n_in):
    h_ref = refs[0]
    xs = refs[1:1 + n_in]
    ws = refs[1 + n_in:1 + 2 * n_in]
    g_ref, wg_ref, wv_ref, cw_ref, cb_ref, wd_ref, o_ref, xn_scr, prev_scr = refs[1 + 2 * n_in:]
    t = pl.program_id(1)
    n_ch = D_FF // FF_CHUNK

    @pl.when(t == 0)
    def _():
        prev_scr[...] = jnp.zeros_like(prev_scr)

    x = h_ref[...]
    for x_ref, w_ref in zip(xs, ws):
        x = x + jnp.dot(x_ref[...], w_ref[...], preferred_element_type=F32)
    xn_scr[...] = _rms_rows(x, g_ref[...]).astype(BF16)
    o_ref[...] = x

    def conv(hh, prev8, cw, cb):
        p6, p7 = prev8[6:7, :], prev8[7:8, :]
        s1 = _shift_rows(hh, 1, [p7])
        s2 = _shift_rows(hh, 2, [p6, p7])
        return cw[0:1, :] * s2 + cw[1:2, :] * s1 + cw[2:3, :] * hh + cb

    def up(c):
        cs = slice(c * FF_CHUNK, (c + 1) * FF_CHUNK)
        xn = xn_scr[...]
        return (jnp.dot(xn, wg_ref[:, cs], preferred_element_type=F32),
                jnp.dot(xn, wv_ref[:, cs], preferred_element_type=F32))

    pending = [up(c) for c in range(FF_DEPTH)]
    for c in range(n_ch):
        hg, hv = pending.pop(0)
        if c + FF_DEPTH < n_ch:
            pending.append(up(c + FF_DEPTH))
        cg = conv(hg, prev_scr[0, c], cw_ref[0, c], cb_ref[0, c])
        cv = conv(hv, prev_scr[1, c], cw_ref[1, c], cb_ref[1, c])
        prev_scr[0, c] = hg[tl - 8:tl, :]
        prev_scr[1, c] = hv[tl - 8:tl, :]
        act = (cg * _sigmoid(cg) * cv).astype(BF16)
        o_ref[...] += jnp.dot(act, wd_ref[c * FF_CHUNK:(c + 1) * FF_CHUNK, :], preferred_element_type=F32)


def _mix_ffn(h, xs, ws, gain, wg, wv, cw, cb, wd, tl):
    b, lp, d = h.shape
    n_ch = D_FF // FF_CHUNK
    full = lambda a: pl.BlockSpec(a.shape, lambda i, t: (0,) * a.ndim, pipeline_mode=pl.Buffered(1))
    row = lambda n: pl.BlockSpec((None, tl, n), lambda i, t: (i, t, 0))
    consts = (*ws, gain, wg, wv, cw, cb, wd)
    return pl.pallas_call(
        functools.partial(_mix_ffn_kernel, tl=tl, n_in=len(xs)),
        grid=(b, lp // tl),
        in_specs=[row(d)] + [row(x.shape[-1]) for x in xs] + [full(a) for a in consts],
        out_specs=row(d),
        out_shape=jax.ShapeDtypeStruct((b, lp, d), F32),
        scratch_shapes=[pltpu.VMEM((tl, d), BF16),
                        pltpu.VMEM((2, n_ch, 8, FF_CHUNK), F32)],
        compiler_params=pltpu.CompilerParams(
            dimension_semantics=("parallel", "arbitrary"), vmem_limit_bytes=VMEM_LIMIT),
        name="mix_out_mlp",
    )(h, *xs, *consts)


def _fox_in_kernel(x_ref, g_ref, w_ref, qg_ref, kg_ref, bf_ref, avg_ref, tri_ref,
                   pq_ref, pk_ref, oq_ref, ok_ref, q_ref, k_ref, v_ref, carry_scr, *, tl):
    t = pl.program_id(1)

    @pl.when(t == 0)
    def _():
        carry_scr[...] = jnp.zeros_like(carry_scr)

    xn = _rms_rows(x_ref[...], g_ref[...]).astype(BF16)
    v_ref[...] = jnp.dot(xn, w_ref[:, 2 * D_MODEL:3 * D_MODEL], preferred_element_type=F32).astype(BF16)
    f = jnp.dot(xn, w_ref[:, 3 * D_MODEL:], preferred_element_type=F32) + bf_ref[...]
    log_f = jnp.minimum(f, 0.0) - jnp.log(1.0 + jnp.exp(-jnp.abs(f)))
    cum = _cumsum_rows(tri_ref[...], log_f) + carry_scr[0:1, :]
    carry_scr[0:1, :] = cum[tl - 1:tl, :]
    lane = lax.broadcasted_iota(jnp.int32, (1, LANES), 1)
    c = jnp.where(lane < FOX_HEADS, cum * LOG2E, 0.0)
    c1 = c.astype(BF16).astype(F32)
    c2 = (c - c1).astype(BF16).astype(F32)
    c3 = (c - c1 - c2).astype(BF16).astype(F32)
    c123 = (c1 + pltpu.roll(c2, FOX_HEADS, 1) + pltpu.roll(c3, 2 * FOX_HEADS, 1)).astype(BF16)
    feat_q = jnp.dot(c123, pq_ref[...], preferred_element_type=F32) + oq_ref[...]
    feat_k = jnp.dot(c123, pk_ref[...], preferred_element_type=F32) + ok_ref[...]

    avg = avg_ref[...]
    m0 = lane < HEAD_DIM
    for off, gain_ref, feat, dst in ((0, qg_ref, feat_q, q_ref), (D_MODEL, kg_ref, feat_k, k_ref)):
        z_all = jnp.dot(xn, w_ref[:, off:off + D_MODEL], preferred_element_type=F32)
        for s in range(PAIRS_FOX):
            z = z_all[:, s * LANES:(s + 1) * LANES]
            ms = _mm(z * z, avg)
            zn = z * lax.rsqrt(ms + RMS_EPS) * gain_ref[...]
            for hh in range(2):
                hs = slice((2 * s + hh) * LANES, (2 * s + hh + 1) * LANES)
                own = m0 if hh == 0 else jnp.logical_not(m0)
                dst[:, hs] = jnp.where(own, zn, feat[:, hs]).astype(BF16)


def _fox_in(h, gain, w, qg, kg, bf, avg, tri, pq, pk, oq, ok, tl):
    b, lp, d = h.shape
    c2 = lambda a: pl.BlockSpec(a.shape, lambda i, t: (0,) * a.ndim)
    act = lambda n: (pl.BlockSpec((None, tl, n), lambda i, t: (i, t, 0)),
                     jax.ShapeDtypeStruct((b, lp, n), BF16))
    specs, shapes = zip(act(2 * d), act(2 * d), act(d))
    consts = (gain, w, qg, kg, bf, avg, tri, pq, pk, oq, ok)
    return pl.pallas_call(
        functools.partial(_fox_in_kernel, tl=tl),
        grid=(b, lp // tl),
        in_specs=[pl.BlockSpec((None, tl, d), lambda i, t: (i, t, 0))] + [c2(a) for a in consts],
        out_specs=list(specs),
        out_shape=list(shapes),
        scratch_shapes=[pltpu.VMEM((8, LANES), F32)],
        compiler_params=pltpu.CompilerParams(
            dimension_semantics=("parallel", "arbitrary"), vmem_limit_bytes=VMEM_LIMIT),
        name="fox_in_proj",
    )(h, *consts)


def _fox_feature_consts():
    pq = np.zeros((LANES, FOX_HEADS * LANES), np.float32)
    pk = np.zeros_like(pq)
    oq = np.zeros((1, FOX_HEADS * LANES), np.float32)
    ok = np.zeros_like(oq)
    for h in range(FOX_HEADS):
        base = h * LANES + HEAD_DIM * (1 - h % 2)
        for i in range(3):
            pq[i * FOX_HEADS + h, base + i] = 1.0
            pk[i * FOX_HEADS + h, base + 3 + i] = -1.0
            oq[0, base + 3 + i] = 1.0
            ok[0, base + i] = 1.0
    return jnp.asarray(pq, BF16), jnp.asarray(pk, BF16), jnp.asarray(oq), jnp.asarray(ok)


def _fox_attn_kernel(q_ref, k_ref, v_ref, o_ref, m_scr, acc_scr, *, ta):
    qi = pl.program_id(2)
    m_scr[...] = jnp.full_like(m_scr, -1e30)
    acc_scr[...] = jnp.zeros_like(acc_scr)

    lane = lax.broadcasted_iota(jnp.int32, (1, LANES), 1)
    m0 = lane < HEAD_DIM
    ones_lane = (lane == HEAD_DIM, lane == 0)

    rt = min(ta, ATTN_ROWS)

    def step(k0s, masked):
        units = [(kb, r, h) for kb in range(len(k0s)) for r in range(ta // rt) for h in range(2)]

        def n_keys(r):
            return (r + 1) * rt if masked else ta

        va = []
        for k0 in k0s:
            v = v_ref[pl.ds(k0, ta), :]
            va.append([jnp.where(m0 if h == 0 else jnp.logical_not(m0), v,
                                 jnp.where(ones_lane[h], 1.0, 0.0).astype(BF16)) for h in range(2)])

        def scores(u):
            kb, r, h = u
            hs = slice(h * LANES, (h + 1) * LANES)
            return lax.dot_general(q_ref[r * rt:(r + 1) * rt, hs], k_ref[pl.ds(k0s[kb], n_keys(r)), hs],
                                   (((1,), (1,)), ((), ())), preferred_element_type=F32)

        def finish(u, sc):
            kb, r, h = u
            rows = slice(r * rt, (r + 1) * rt)
            if masked:
                rr = lax.broadcasted_iota(jnp.int32, sc.shape, 0) + r * rt
                cc = lax.broadcasted_iota(jnp.int32, sc.shape, 1)
                sc = jnp.where(cc <= rr, sc, -1e30)
            m_prev = m_scr[h, rows]
            m_new = jnp.maximum(m_prev, jnp.max(sc, axis=1, keepdims=True))
            p = jnp.exp2(sc - m_new).astype(BF16)
            alpha = jnp.exp2(m_prev - m_new)
            m_scr[h, rows] = m_new
            acc_scr[h, rows] = alpha * acc_scr[h, rows] + jnp.dot(p, va[kb][h][:n_keys(r)],
                                                                  preferred_element_type=F32)

        pending = [scores(u) for u in units[:ATTN_DEPTH]]
        for i, u in enumerate(units):
            sc = pending.pop(0)
            if i + ATTN_DEPTH < len(units):
                pending.append(scores(units[i + ATTN_DEPTH]))
            finish(u, sc)

    def body(kj, carry):
        k0 = pl.multiple_of(kj * (2 * ta), 2 * ta)
        step([k0, k0 + ta], False)
        return carry

    lax.fori_loop(0, qi // 2, body, 0)

    @pl.when(qi % 2 == 1)
    def _():
        step([pl.multiple_of((qi - 1) * ta, ta)], False)

    step([pl.multiple_of(qi * ta, ta)], True)
    a0, a1 = acc_scr[0], acc_scr[1]
    o0 = a0 / a0[:, HEAD_DIM:HEAD_DIM + 1]
    o1 = a1 / a1[:, 0:1]
    o_ref[...] = jnp.where(m0, o0, o1).astype(o_ref.dtype)


def _fox_attn(q, k, v, ta):
    b, lp, d = v.shape
    return pl.pallas_call(
        functools.partial(_fox_attn_kernel, ta=ta),
        grid=(b, PAIRS_FOX, lp // ta),
        in_specs=[pl.BlockSpec((None, ta, 2 * LANES), lambda i, h, t: (i, t, h)),
                  pl.BlockSpec((None, lp, 2 * LANES), lambda i, h, t: (i, 0, h)),
                  pl.BlockSpec((None, lp, LANES), lambda i, h, t: (i, 0, h))],
        out_specs=pl.BlockSpec((None, ta, LANES), lambda i, h, t: (i, t, h)),
        out_shape=jax.ShapeDtypeStruct((b, lp, d), BF16),
        scratch_shapes=[pltpu.VMEM((2, ta, 1), F32),
                        pltpu.VMEM((2, ta, LANES), F32)],
        compiler_params=pltpu.CompilerParams(
            dimension_semantics=("parallel", "parallel", "parallel"), vmem_limit_bytes=VMEM_LIMIT),
        name="fox_attention",
    )(q, k, v)


def _block_diag(blocks):
    g, r, c = blocks.shape
    eye = jnp.eye(g, dtype=blocks.dtype)
    return (eye[:, None, :, None] * blocks[:, :, None, :]).reshape(g * r, g * c)


def _cmul(ar, ai, br, bi):
    return ar * br - ai * bi, ar * bi + ai * br


def _s5_params(a_re, a_im, log_dt, b_re, b_im, c_re, c_im):
    dt = jnp.exp(log_dt)[:, None]
    mag = jnp.exp(a_re * dt)
    lam_re, lam_im = mag * jnp.cos(a_im * dt), mag * jnp.sin(a_im * dt)
    den = a_re * a_re + a_im * a_im
    z_re = ((lam_re - 1.0) * a_re + lam_im * a_im) / den
    z_im = (lam_im * a_re - (lam_re - 1.0) * a_im) / den
    bb_re = z_re[..., None] * b_re - z_im[..., None] * b_im
    bb_im = z_re[..., None] * b_im + z_im[..., None] * b_re
    to_in = lambda m: _block_diag(jnp.swapaxes(m, 1, 2))
    bm = jnp.concatenate([to_in(bb_re), to_in(bb_im)], axis=1).astype(BF16)
    to_out = lambda m: _block_diag(jnp.swapaxes(m, 1, 2))
    cm = jnp.concatenate([to_out(c_re), -to_out(c_im)], axis=0).astype(BF16)
    lr, li = lam_re.reshape(1, S5_LANES), lam_im.reshape(1, S5_LANES)
    n_lv = int(math.log2(S5_ROWS))
    pows, tr, ti = [], lr, li
    pr, pi = lr, li
    for _ in range(n_lv):
        pows += [pr, pi]
        nr, ni = _cmul(tr, ti, pr, pi)
        tr, ti = jnp.concatenate([tr, nr], axis=0), jnp.concatenate([ti, ni], axis=0)
        pr, pi = _cmul(pr, pi, pr, pi)
    pows = jnp.concatenate(pows + [jnp.zeros((16 - 2 * n_lv, S5_LANES), F32)], axis=0)
    return bm, cm, pows, jnp.stack([tr, ti])


def _rwkv_consts():
    idx = np.arange(LANES)
    hsum = (idx[:, None] // HEAD_DIM == idx[None, :] // HEAD_DIM).astype(np.float32)
    t = idx % CHUNK
    lv = [(t[:, None] // 2 == t[None, :] // 2)]
    b = 2
    while b < CHUNK:
        lv.append((t[:, None] // (2 * b) == t[None, :] // (2 * b)) & (t[:, None] // b != t[None, :] // b))
        b *= 2
    tri = (np.arange(CHUNK)[None, :] <= np.arange(CHUNK)[:, None]).astype(np.float32)
    return (jnp.asarray(hsum, BF16), jnp.asarray(tri, BF16), jnp.asarray(np.stack(lv).astype(np.float32)))


def _time_tile(length):
    return 768 if length > 2048 else 128


def kernel(x, meta_tokens, mix_norm, ffn_norm, even_w_in, even_w_out, rwkv_mu, rwkv_w0, rwkv_w2, rwkv_a0, rwkv_a2, rwkv_g2, rwkv_k_k, rwkv_k_a, rwkv_r_k, rwkv_ln_w, rwkv_ln_b, s5_a_re, s5_a_im, s5_log_dt, s5_b_re, s5_b_im, s5_c_re, s5_c_im, s5_d, s5_glu_w, s5_glu_b, odd_w_in, odd_w_out, fox_b_f, fox_q_gain, fox_k_gain, ffn_w_up, ffn_conv_w, ffn_conv_b, ffn_w_down):
    bsz, seq, d = x.shape
    length = N_META + seq
    tl = _time_tile(length)
    lp = -(-length // tl) * tl
    meta = jnp.broadcast_to(meta_tokens[None].astype(x.dtype), (bsz, N_META, d))
    h = jnp.concatenate([meta, x, jnp.zeros((bsz, lp - length, d), x.dtype)], axis=1)

    hsum, tri64, lvl = _rwkv_consts()
    n_ch = D_FF // FF_CHUNK

    def mix_ffn(h, xs, ws, layer):
        w_up = ffn_w_up[layer].astype(BF16)
        cw = ffn_conv_w[layer].reshape(3, 2, n_ch, FF_CHUNK).transpose(1, 2, 0, 3)
        cw = jnp.pad(cw, ((0, 0), (0, 0), (0, 5), (0, 0)))
        cb = ffn_conv_b[layer].reshape(2, n_ch, 1, FF_CHUNK)
        return _mix_ffn(h, xs, ws, ffn_norm[layer][None], w_up[:, :D_FF], w_up[:, D_FF:], cw, cb,
                        ffn_w_down[layer].astype(BF16), tl)

    p = _rms_matmul(h, mix_norm[0][None], even_w_in[0].astype(BF16), tl, BF16)
    pairs = lambda vec: vec.reshape(PAIRS_RWKV, 1, LANES)
    mu = rwkv_mu[0]
    prm = jnp.concatenate(
        [pairs(mu[:RWKV_DIM]), pairs(mu[RWKV_DIM:2 * RWKV_DIM]), pairs(mu[2 * RWKV_DIM:3 * RWKV_DIM]),
         pairs(rwkv_w0[0]), pairs(rwkv_a0[0]), pairs(rwkv_k_k[0]), pairs(rwkv_k_a[0]),
         pairs(rwkv_r_k[0].reshape(-1)), pairs(rwkv_ln_w[0]), pairs(rwkv_ln_b[0]),
         jnp.zeros((PAIRS_RWKV, 6, LANES), F32)], axis=1)
    zeros64 = jnp.zeros((64, RWKV_DIM), F32)
    w2p = jnp.concatenate([rwkv_w2[0], zeros64], axis=0).astype(BF16)
    a2p = jnp.concatenate([zeros64, rwkv_a2[0]], axis=0).astype(BF16)
    a_out = _rwkv(p, prm, mu[3 * RWKV_DIM:][None], w2p, a2p, rwkv_g2[0].astype(BF16),
                  hsum, tri64, lvl, tl, tl // CHUNK)
    bm, cm, pows, tab = _s5_params(s5_a_re[0], s5_a_im[0], s5_log_dt[0], s5_b_re[0], s5_b_im[0],
                                   s5_c_re[0], s5_c_im[0])
    b_out = _s5(p, bm, cm, pows, tab, s5_d[0][None], s5_glu_w[0].astype(BF16), s5_glu_b[0][None], tl)
    w_out = even_w_out[0].astype(BF16)
    h = mix_ffn(h, [a_out, b_out], [w_out[:RWKV_DIM], w_out[RWKV_DIM:]], 0)

    w_in = jnp.pad(odd_w_in[0], ((0, 0), (0, ODD_IN_PAD - odd_w_in.shape[-1]))).astype(BF16)
    scale = HEAD_DIM ** -0.5
    qg = jnp.tile(fox_q_gain[0], 2)[None] * (scale * LOG2E)
    kg = jnp.tile(fox_k_gain[0], 2)[None]
    bf = jnp.pad(fox_b_f[0], (0, LANES - FOX_HEADS))[None]
    tri_t = jnp.asarray(np.tril(np.ones((tl, tl), np.float32)), BF16)
    q, k, v = _fox_in(h, mix_norm[1][None], w_in, qg, kg, bf, hsum * (1.0 / HEAD_DIM), tri_t,
                      *_fox_feature_consts(), tl)
    o = _fox_attn(q, k, v, tl)
    h = mix_ffn(h, [o], [odd_w_out[0].astype(BF16)], 1)
    return h[:, N_META:N_META + seq]
```

```python
import functools
import math

import jax
import jax.numpy as jnp
import numpy as np
from jax import lax
from jax.experimental import pallas as pl
from jax.experimental.pallas import tpu as pltpu

F32 = jnp.float32
BF16 = jnp.bfloat16

D_MODEL = 1024
N_META = 16
RMS_EPS = 1e-6
GN_EPS = 64e-5

RWKV_HEADS = 12
HEAD_DIM = 64
RWKV_DIM = RWKV_HEADS * HEAD_DIM
LORA_COLS = 256
RWKV_COLS = 3 * RWKV_DIM + LORA_COLS
S5_DIM = 256
S5_GROUP = 16
S5_GROUPS = 16
S5_STATE = 64
S5_LANES = S5_GROUPS * S5_STATE
EVEN_IN = RWKV_COLS + S5_DIM
FOX_HEADS = 16
ODD_IN_PAD = 3 * D_MODEL + 128
D_FF = 2816
LANES = 128
PAIRS_RWKV = RWKV_DIM // LANES
PAIRS_FOX = D_MODEL // LANES
CHUNK = 64
S5_SUB = 128
S5_ROWS = 8
FF_CHUNK = 256
FF_DEPTH = 2
VMEM_LIMIT = 56 * 1024 * 1024
LOG2E = 1.4426950408889634
ATTN_ROWS = 256
ATTN_DEPTH = 3
EXP2_ZERO = 160.0


def _mm(a, b):
    return jnp.dot(a.astype(BF16), b.astype(BF16), preferred_element_type=F32)


def _mm_nt(a, b):
    return lax.dot_general(a.astype(BF16), b.astype(BF16), (((1,), (1,)), ((), ())),
                           preferred_element_type=F32)


def _mm_tn(a, b):
    return lax.dot_general(a.astype(BF16), b.astype(BF16), (((0,), (0,)), ((), ())),
                           preferred_element_type=F32)


def _split2_mm(x, m):
    hi = x.astype(BF16)
    lo = (x - hi.astype(F32)).astype(BF16)
    return (jnp.dot(hi, m, preferred_element_type=F32)
            + jnp.dot(lo, m, preferred_element_type=F32))


def _cumsum_rows(tri, x):
    h1 = x.astype(BF16)
    r1 = x - h1.astype(F32)
    h2 = r1.astype(BF16)
    h3 = (r1 - h2.astype(F32)).astype(BF16)
    w = x.shape[1]
    res = jnp.dot(tri, jnp.concatenate([h1, h2, h3], axis=1), preferred_element_type=F32)
    return res[:, :w] + res[:, w:2 * w] + res[:, 2 * w:]


def _softplus(z):
    return jnp.maximum(z, 0.0) + jnp.log(1.0 + jnp.exp(-jnp.abs(z)))


def _sigmoid(z):
    return 1.0 / (1.0 + jnp.exp(-z))


def _rms_rows(x, gain):
    ms = jnp.mean(x * x, axis=-1, keepdims=True)
    return x * lax.rsqrt(ms + RMS_EPS) * gain


def _shift_rows(x, k, prev_rows):
    rolled = pltpu.roll(x, k, 0)
    row = lax.broadcasted_iota(jnp.int32, x.shape, 0)
    for i in range(k):
        rolled = jnp.where(row == i, prev_rows[i], rolled)
    return rolled


def _rms_matmul_kernel(x_ref, g_ref, w_ref, o_ref, *, n_chunk):
    xn = _rms_rows(x_ref[...], g_ref[...]).astype(BF16)
    n = o_ref.shape[-1]
    for j in range(0, n, n_chunk):
        o_ref[:, j:j + n_chunk] = jnp.dot(xn, w_ref[:, j:j + n_chunk],
                                          preferred_element_type=F32).astype(o_ref.dtype)


def _rms_matmul(h, gain, w, tl, out_dtype):
    b, lp, d = h.shape
    n = w.shape[1]
    return pl.pallas_call(
        functools.partial(_rms_matmul_kernel, n_chunk=256),
        grid=(b, lp // tl),
        in_specs=[pl.BlockSpec((None, tl, d), lambda i, t: (i, t, 0)),
                  pl.BlockSpec((1, d), lambda i, t: (0, 0)),
                  pl.BlockSpec((d, n), lambda i, t: (0, 0))],
        out_specs=pl.BlockSpec((None, tl, n), lambda i, t: (i, t, 0)),
        out_shape=jax.ShapeDtypeStruct((b, lp, n), out_dtype),
        compiler_params=pltpu.CompilerParams(
            dimension_semantics=("parallel", "parallel"), vmem_limit_bytes=VMEM_LIMIT),
        name="rms_in_proj",
    )(h, gain, w)


def _rwkv_kernel(r_ref, k_ref, v_ref, lo_ref, prm_ref, mul_ref, w2_ref, a2_ref, g2_ref,
                 hsum_ref, tri_ref, lvl_ref, o_ref,
                 s_scr, prev_scr, prevl_scr, y_scr, q_scr, *, tl, group):
    t = pl.program_id(2)

    @pl.when(t == 0)
    def _():
        s_scr[...] = jnp.zeros_like(s_scr)
        prev_scr[...] = jnp.zeros_like(prev_scr)
        prevl_scr[...] = jnp.zeros_like(prevl_scr)

    prm = prm_ref[...]
    row = lambda i: prm[i:i + 1, :]
    hsum = hsum_ref[...]

    def tshift(x, prev, mu):
        prevx = _shift_rows(x, 1, [prev])
        return x + (prevx - x) * mu

    r_raw, k_raw, v_raw, lo_raw = (ref[...].astype(F32) for ref in (r_ref, k_ref, v_ref, lo_ref))
    r = tshift(r_raw, prev_scr[0:1, :], row(0))
    k = tshift(k_raw, prev_scr[1:2, :], row(1))
    v = tshift(v_raw, prev_scr[2:3, :], row(2))
    lo = tshift(lo_raw, prevl_scr[0:1, :], mul_ref[...])
    prev_scr[0:1, :] = r_raw[tl - 1:tl, :]
    prev_scr[1:2, :] = k_raw[tl - 1:tl, :]
    prev_scr[2:3, :] = v_raw[tl - 1:tl, :]
    prevl_scr[0:1, :] = lo_raw[tl - 1:tl, :]

    lo01 = lo[:, :LANES]
    w_pre = row(3) + _mm(jnp.tanh(lo01), w2_ref[...])
    logw = -jnp.exp(-_softplus(-w_pre) - 0.5)
    a = _sigmoid(row(4) + _mm(lo01, a2_ref[...]))
    g = _mm(_sigmoid(lo[:, LANES:]), g2_ref[...])
    kk = k * row(5)
    kk = kk / jnp.maximum(jnp.sqrt(_split2_mm(kk * kk, hsum)), 1e-12)
    kf = k * (1.0 + (a - 1.0) * row(6))
    ka = kk * a
    bonus = _split2_mm(r * kf * row(7), hsum) * v

    q_scr[0] = r
    q_scr[1] = logw
    q_scr[2] = kf
    q_scr[3] = v
    q_scr[4] = kk
    q_scr[5] = ka

    lane = lax.broadcasted_iota(jnp.int32, (1, LANES), 1)
    m0 = lane < HEAD_DIM
    tri = tri_ref[...]
    ri = lax.broadcasted_iota(jnp.int32, (LANES, LANES), 0)
    ci = lax.broadcasted_iota(jnp.int32, (LANES, LANES), 1)
    rt_, ct_ = ri & (CHUNK - 1), ci & (CHUNK - 1)
    strict = ct_ < rt_
    incl = ct_ <= rt_
    eye = ri == ci

    def stack(x):
        return jnp.concatenate([jnp.where(m0, x, 0.0), jnp.where(m0, 0.0, x)], axis=0).astype(BF16)

    def group_pre(bases):
        rng = range(len(bases))
        ld = [[q_scr[i, pl.ds(b, CHUNK), :] for i in range(6)] for b in bases]
        cum = [_cumsum_rows(tri, ld[j][1]) for j in rng]
        ops = []
        for j in rng:
            rc, lw, kc, vc, kkc, kac = ld[j]
            cum_c = cum[j][CHUNK - 1:CHUNK, :]
            w_in = jnp.exp(cum[j])
            w_inv = jnp.exp(-cum[j])
            w_ex = jnp.exp(cum[j] - lw)
            w_end = jnp.exp(cum_c - cum[j])
            ops.append(dict(
                at=stack(-kkc * w_ex), rt=stack(rc * w_in), bt=stack(kac * w_inv), kt=stack(kc * w_inv),
                bh=stack(kac * w_end), kh=stack(kc * w_end), v=stack(vc), wc=jnp.exp(cum_c)))
        am = [_mm_nt(jnp.concatenate([o["at"], o["rt"]], axis=0), jnp.concatenate([o["bt"], o["kt"]], axis=0))
              for o in ops]
        l_ab = [jnp.where(strict, m[:LANES, :LANES], 0.0) for m in am]
        a_ak = [jnp.where(strict, m[:LANES, LANES:], 0.0).astype(BF16) for m in am]
        m_rb = [jnp.where(incl, m[LANES:, :LANES], 0.0).astype(BF16) for m in am]
        m_rk = [jnp.where(incl, m[LANES:, LANES:], 0.0).astype(BF16) for m in am]
        lv0 = lvl_ref[0]
        tm = [jnp.where(eye, 1.0, 0.0) + l * lv0 for l in l_ab]
        av = [_mm(a_ak[j], ops[j]["v"]) for j in rng]
        for lv in range(1, 6):
            b = 1 << lv
            lvm = lvl_ref[lv]
            if b < 8:
                z = [_mm(l_ab[j] * lvm, tm[j]) for j in rng]
                tm = [tm[j] + _mm(tm[j], z[j]) for j in rng]
            else:
                odd = lambda x: jnp.concatenate([x[s:s + b] for s in range(b, LANES, 2 * b)], axis=0)
                lvm_o = odd(lvm)
                z_o = [_mm(odd(l_ab[j]) * lvm_o, tm[j]) for j in rng]
                zero = jnp.zeros((b, LANES), F32)

                def spread(y, base=None):
                    out = []
                    for i, s in enumerate(range(0, LANES, b)):
                        piece = y[(i // 2) * b:(i // 2 + 1) * b] if i % 2 else None
                        if base is None:
                            out.append(piece if i % 2 else zero)
                        else:
                            out.append(base[s:s + b] + piece if i % 2 else base[s:s + b])
                    return jnp.concatenate(out, axis=0)

                upd = [_mm(odd(tm[j]), spread(z_o[j])) for j in rng]
                tm = [spread(upd[j], tm[j]) for j in rng]
        pq = [_mm(tm[j], jnp.concatenate([ops[j]["at"], av[j].astype(BF16)], axis=1)) for j in rng]
        p_s = [m[:, :LANES].astype(BF16) for m in pq]
        qv = [jnp.concatenate([pq[j][:, LANES:].astype(BF16), ops[j]["v"]], axis=0) for j in rng]
        rp = [ops[j]["rt"].astype(F32) + _mm(m_rb[j], p_s[j]) for j in rng]
        y0 = [_mm(jnp.concatenate([m_rb[j], m_rk[j]], axis=1), qv[j]) for j in rng]
        gm = [jnp.where(eye, ops[j]["wc"], 0.0) + _mm_tn(ops[j]["bh"], p_s[j]) for j in rng]
        hm = [_mm_tn(jnp.concatenate([ops[j]["bh"], ops[j]["kh"]], axis=0), qv[j]) for j in rng]
        return rp, y0, gm, hm

    def run_group(base):
        bases = [base + j * CHUNK for j in range(group)]
        rp, y0, gm, hm = group_pre(bases)
        s = s_scr[...]
        for j in range(group):
            ys = _mm(rp[j], s) + y0[j]
            y_scr[pl.ds(bases[j], CHUNK), :] = ys[:CHUNK] + ys[CHUNK:]
            s = _mm(gm[j], s) + hm[j]
        s_scr[...] = s

    n_groups = tl // (group * CHUNK)
    if n_groups == 1:
        run_group(0)
    else:
        def group_body(gi, carry):
            run_group(pl.multiple_of(gi * (group * CHUNK), group * CHUNK))
            return carry
        lax.fori_loop(0, n_groups, group_body, 0)

    y = y_scr[...]
    mean = _split2_mm(y, hsum) * (1.0 / HEAD_DIM)
    yc = y - mean
    var = _split2_mm(yc * yc, hsum) * (1.0 / HEAD_DIM)
    yn = yc * lax.rsqrt(var + GN_EPS) * row(8) + row(9)
    o_ref[...] = ((yn + bonus) * g).astype(o_ref.dtype)


def _rwkv(p, prm, mu_l, w2p, a2p, g2, hsum, tri, lvl, tl, group):
    b, lp, _ = p.shape
    nt = lp // tl
    blk = lambda off: pl.BlockSpec((None, tl, LANES), lambda i, h, t: (i, t, off + h))
    const2 = lambda shape: pl.BlockSpec(shape, lambda i, h, t: (0, 0))
    return pl.pallas_call(
        functools.partial(_rwkv_kernel, tl=tl, group=group),
        grid=(b, PAIRS_RWKV, nt),
        in_specs=[blk(0), blk(PAIRS_RWKV), blk(2 * PAIRS_RWKV),
                  pl.BlockSpec((None, tl, LORA_COLS), lambda i, h, t: (i, t, 3 * RWKV_DIM // LORA_COLS)),
                  pl.BlockSpec((None, 16, LANES), lambda i, h, t: (h, 0, 0)),
                  const2((1, LORA_COLS)),
                  pl.BlockSpec((LANES, LANES), lambda i, h, t: (0, h)),
                  pl.BlockSpec((LANES, LANES), lambda i, h, t: (0, h)),
                  pl.BlockSpec((LANES, LANES), lambda i, h, t: (0, h)),
                  const2((LANES, LANES)),
                  const2((CHUNK, CHUNK)),
                  pl.BlockSpec((6, LANES, LANES), lambda i, h, t: (0, 0, 0))],
        out_specs=pl.BlockSpec((None, tl, LANES), lambda i, h, t: (i, t, h)),
        out_shape=jax.ShapeDtypeStruct((b, lp, RWKV_DIM), BF16),
        scratch_shapes=[pltpu.VMEM((LANES, LANES), F32),
                        pltpu.VMEM((8, LANES), F32),
                        pltpu.VMEM((8, LORA_COLS), F32),
                        pltpu.VMEM((tl, LANES), F32),
                        pltpu.VMEM((6, tl, LANES), F32)],
        compiler_params=pltpu.CompilerParams(
            dimension_semantics=("parallel", "parallel", "arbitrary"), vmem_limit_bytes=VMEM_LIMIT),
        name="rwkv7_mix",
    )(p, p, p, p, prm, mu_l, w2p, a2p, g2, hsum, tri, lvl)


def _s5_kernel(u_ref, bm_ref, cm_ref, lam_ref, tab_ref, d_ref, gw_ref, gb_ref, o_ref,
               h0_scr, hs_scr, *, tl):
    t = pl.program_id(1)

    @pl.when(t == 0)
    def _():
        h0_scr[...] = jnp.zeros_like(h0_scr)

    n_lv = int(math.log2(S5_ROWS))
    n_grp = S5_SUB // S5_ROWS
    row_i = lax.broadcasted_iota(jnp.int32, (n_grp, S5_ROWS, LANES), 1)

    def sub_body(si, carry):
        r0 = pl.multiple_of(si * S5_SUB, S5_SUB)
        u = u_ref[pl.ds(r0, S5_SUB), :].astype(F32)
        bu = _mm(u, bm_ref[...])
        for lb in range(S5_LANES // LANES):
            ls = slice(lb * LANES, (lb + 1) * LANES)
            li = slice(S5_LANES + lb * LANES, S5_LANES + (lb + 1) * LANES)
            hr = bu[:, ls].reshape(n_grp, S5_ROWS, LANES)
            hi = bu[:, li].reshape(n_grp, S5_ROWS, LANES)
            for lv in range(n_lv):
                sh = 1 << lv
                lr = lam_ref[2 * lv:2 * lv + 1, ls]
                lm = lam_ref[2 * lv + 1:2 * lv + 2, ls]
                keep = row_i >= sh
                sr = jnp.where(keep, pltpu.roll(hr, sh, 1), 0.0)
                sm = jnp.where(keep, pltpu.roll(hi, sh, 1), 0.0)
                hr, hi = hr + (lr * sr - lm * sm), hi + (lr * sm + lm * sr)
            tr, tm = tab_ref[0, :, ls], tab_ref[1, :, ls]
            cr, ci = h0_scr[0:1, ls], h0_scr[1:2, ls]
            out_r, out_i = [], []
            for g in range(n_grp):
                gr = hr[g] + (tr * cr - tm * ci)
                gi = hi[g] + (tr * ci + tm * cr)
                cr, ci = gr[S5_ROWS - 1:S5_ROWS, :], gi[S5_ROWS - 1:S5_ROWS, :]
                out_r.append(gr)
                out_i.append(gi)
            h0_scr[0:1, ls] = cr
            h0_scr[1:2, ls] = ci
            hs_scr[:, ls] = jnp.concatenate(out_r, axis=0).astype(BF16)
            hs_scr[:, li] = jnp.concatenate(out_i, axis=0).astype(BF16)
        y = jnp.dot(hs_scr[...], cm_ref[...], preferred_element_type=F32) + d_ref[...] * u
        y = 0.5 * y * (1.0 + jnp.tanh(0.7978845608028654 * (y + 0.044715 * (y * y * y))))
        z = _mm(y, gw_ref[...]) + gb_ref[...]
        o_ref[pl.ds(r0, S5_SUB), :] = (y * _sigmoid(z)).astype(o_ref.dtype)
        return carry

    lax.fori_loop(0, tl // S5_SUB, sub_body, 0)


def _s5(p, bm, cm, lam_pows, tab, d_skip, glu_w, glu_b, tl):
    b, lp, _ = p.shape
    c2 = lambda a: pl.BlockSpec(a.shape, lambda i, t: (0,) * a.ndim)
    return pl.pallas_call(
        functools.partial(_s5_kernel, tl=tl),
        grid=(b, lp // tl),
        in_specs=[pl.BlockSpec((None, tl, S5_DIM), lambda i, t: (i, t, RWKV_COLS // S5_DIM)),
                  c2(bm), c2(cm), c2(lam_pows), c2(tab), c2(d_skip), c2(glu_w), c2(glu_b)],
        out_specs=pl.BlockSpec((None, tl, S5_DIM), lambda i, t: (i, t, 0)),
        out_shape=jax.ShapeDtypeStruct((b, lp, S5_DIM), BF16),
        scratch_shapes=[pltpu.VMEM((8, S5_LANES), F32),
                        pltpu.VMEM((S5_SUB, 2 * S5_LANES), BF16)],
        compiler_params=pltpu.CompilerParams(
            dimension_semantics=("parallel", "arbitrary"), vmem_limit_bytes=VMEM_LIMIT),
        name="s5_mix",
    )(p, bm, cm, lam_pows, tab, d_skip, glu_w, glu_b)


def _mix_ffn_kernel(*refs, tl, n_in):
    h_ref = refs[0]
    xs = refs[1:1 + n_in]
    ws = refs[1 + n_in:1 + 2 * n_in]
    g_ref, wg_ref, wv_ref, cw_ref, cb_ref, wd_ref, o_ref, xn_scr, prev_scr = refs[1 + 2 * n_in:]
    t = pl.program_id(1)
    n_ch = D_FF // FF_CHUNK

    @pl.when(t == 0)
    def _():
        prev_scr[...] = jnp.zeros_like(prev_scr)

    x = h_ref[...]
    for x_ref, w_ref in zip(xs, ws):
        x = x + jnp.dot(x_ref[...], w_ref[...], preferred_element_type=F32)
    xn_scr[...] = _rms_rows(x, g_ref[...]).astype(BF16)
    o_ref[...] = x

    def conv(hh, prev8, cw, cb):
        p6, p7 = prev8[6:7, :], prev8[7:8, :]
        s1 = _shift_rows(hh, 1, [p7])
        s2 = _shift_rows(hh, 2, [p6, p7])
        return cw[0:1, :] * s2 + cw[1:2, :] * s1 + cw[2:3, :] * hh + cb

    def up(c):
        cs = slice(c * FF_CHUNK, (c + 1) * FF_CHUNK)
        xn = xn_scr[...]
        return (jnp.dot(xn, wg_ref[:, cs], preferred_element_type=F32),
                jnp.dot(xn, wv_ref[:, cs], preferred_element_type=F32))

    pending = [up(c) for c in range(FF_DEPTH)]
    for c in range(n_ch):
        hg, hv = pending.pop(0)
        if c + FF_DEPTH < n_ch:
            pending.append(up(c + FF_DEPTH))
        cg = conv(hg, prev_scr[0, c], cw_ref[0, c], cb_ref[0, c])
        cv = conv(hv, prev_scr[1, c], cw_ref[1, c], cb_ref[1, c])
        prev_scr[0, c] = hg[tl - 8:tl, :]
        prev_scr[1, c] = hv[tl - 8:tl, :]
        act = (cg * _sigmoid(cg) * cv).astype(BF16)
        o_ref[...] += jnp.dot(act, wd_ref[c * FF_CHUNK:(c + 1) * FF_CHUNK, :], preferred_element_type=F32)


def _mix_ffn(h, xs, ws, gain, wg, wv, cw, cb, wd, tl):
    b, lp, d = h.shape
    n_ch = D_FF // FF_CHUNK
    full = lambda a: pl.BlockSpec(a.shape, lambda i, t: (0,) * a.ndim, pipeline_mode=pl.Buffered(1))
    row = lambda n: pl.BlockSpec((None, tl, n), lambda i, t: (i, t, 0))
    consts = (*ws, gain, wg, wv, cw, cb, wd)
    return pl.pallas_call(
        functools.partial(_mix_ffn_kernel, tl=tl, n_in=len(xs)),
        grid=(b, lp // tl),
        in_specs=[row(d)] + [row(x.shape[-1]) for x in xs] + [full(a) for a in consts],
        out_specs=row(d),
        out_shape=jax.ShapeDtypeStruct((b, lp, d), F32),
        scratch_shapes=[pltpu.VMEM((tl, d), BF16),
                        pltpu.VMEM((2, n_ch, 8, FF_CHUNK), F32)],
        compiler_params=pltpu.CompilerParams(
            dimension_semantics=("parallel", "arbitrary"), vmem_limit_bytes=VMEM_LIMIT),
        name="mix_out_mlp",
    )(h, *xs, *consts)


def _fox_in_kernel(x_ref, g_ref, w_ref, qg_ref, kg_ref, bf_ref, avg_ref, tri_ref,
                   pq_ref, pk_ref, oq_ref, ok_ref, q_ref, k_ref, v_ref, cb_ref, carry_scr, *, tl):
    t = pl.program_id(1)

    @pl.when(t == 0)
    def _():
        carry_scr[...] = jnp.zeros_like(carry_scr)

    xn = _rms_rows(x_ref[...], g_ref[...]).astype(BF16)
    v_ref[...] = jnp.dot(xn, w_ref[:, 2 * D_MODEL:3 * D_MODEL], preferred_element_type=F32).astype(BF16)
    f = jnp.dot(xn, w_ref[:, 3 * D_MODEL:], preferred_element_type=F32) + bf_ref[...]
    log_f = jnp.minimum(f, 0.0) - jnp.log(1.0 + jnp.exp(-jnp.abs(f)))
    cum = _cumsum_rows(tri_ref[...], log_f) + carry_scr[0:1, :]
    carry_scr[0:1, :] = cum[tl - 1:tl, :]
    lane = lax.broadcasted_iota(jnp.int32, (1, LANES), 1)
    c = jnp.where(lane < FOX_HEADS, cum * LOG2E, 0.0)
    cb_ref[...] = jnp.concatenate([c[0:1, :], c[tl - 1:tl, :], jnp.zeros((6, LANES), F32)], axis=0)
    c1 = c.astype(BF16).astype(F32)
    c2 = (c - c1).astype(BF16).astype(F32)
    c3 = (c - c1 - c2).astype(BF16).astype(F32)
    c123 = (c1 + pltpu.roll(c2, FOX_HEADS, 1) + pltpu.roll(c3, 2 * FOX_HEADS, 1)).astype(BF16)
    feat_q = jnp.dot(c123, pq_ref[...], preferred_element_type=F32) + oq_ref[...]
    feat_k = jnp.dot(c123, pk_ref[...], preferred_element_type=F32) + ok_ref[...]

    avg = avg_ref[...]
    m0 = lane < HEAD_DIM
    for off, gain_ref, feat, dst in ((0, qg_ref, feat_q, q_ref), (D_MODEL, kg_ref, feat_k, k_ref)):
        z_all = jnp.dot(xn, w_ref[:, off:off + D_MODEL], preferred_element_type=F32)
        for s in range(PAIRS_FOX):
            z = z_all[:, s * LANES:(s + 1) * LANES]
            ms = _mm(z * z, avg)
            zn = z * lax.rsqrt(ms + RMS_EPS) * gain_ref[...]
            for hh in range(2):
                hs = slice((2 * s + hh) * LANES, (2 * s + hh + 1) * LANES)
                own = m0 if hh == 0 else jnp.logical_not(m0)
                dst[:, hs] = jnp.where(own, zn, feat[:, hs]).astype(BF16)


def _fox_in(h, gain, w, qg, kg, bf, avg, tri, pq, pk, oq, ok, tl):
    b, lp, d = h.shape
    c2 = lambda a: pl.BlockSpec(a.shape, lambda i, t: (0,) * a.ndim)
    act = lambda n: (pl.BlockSpec((None, tl, n), lambda i, t: (i, t, 0)),
                     jax.ShapeDtypeStruct((b, lp, n), BF16))
    bounds = (pl.BlockSpec((None, None, 8, LANES), lambda i, t: (i, t, 0, 0)),
              jax.ShapeDtypeStruct((b, lp // tl, 8, LANES), F32))
    specs, shapes = zip(act(2 * d), act(2 * d), act(d), bounds)
    consts = (gain, w, qg, kg, bf, avg, tri, pq, pk, oq, ok)
    return pl.pallas_call(
        functools.partial(_fox_in_kernel, tl=tl),
        grid=(b, lp // tl),
        in_specs=[pl.BlockSpec((None, tl, d), lambda i, t: (i, t, 0))] + [c2(a) for a in consts],
        out_specs=list(specs),
        out_shape=list(shapes),
        scratch_shapes=[pltpu.VMEM((8, LANES), F32)],
        compiler_params=pltpu.CompilerParams(
            dimension_semantics=("parallel", "arbitrary"), vmem_limit_bytes=VMEM_LIMIT),
        name="fox_in_proj",
    )(h, *consts)


def _fox_feature_consts():
    pq = np.zeros((LANES, FOX_HEADS * LANES), np.float32)
    pk = np.zeros_like(pq)
    oq = np.zeros((1, FOX_HEADS * LANES), np.float32)
    ok = np.zeros_like(oq)
    for h in range(FOX_HEADS):
        base = h * LANES + HEAD_DIM * (1 - h % 2)
        for i in range(3):
            pq[i * FOX_HEADS + h, base + i] = 1.0
            pk[i * FOX_HEADS + h, base + 3 + i] = -1.0
            oq[0, base + 3 + i] = 1.0
            ok[0, base + i] = 1.0
    return jnp.asarray(pq, BF16), jnp.asarray(pk, BF16), jnp.asarray(oq), jnp.asarray(ok)


def _fox_attn_kernel(first_ref, q_ref, k_ref, v_ref, o_ref, m_scr, acc_scr, *, ta):
    qi = pl.program_id(2)
    m_scr[...] = jnp.full_like(m_scr, -1e30)
    acc_scr[...] = jnp.zeros_like(acc_scr)

    lane = lax.broadcasted_iota(jnp.int32, (1, LANES), 1)
    m0 = lane < HEAD_DIM
    ones_lane = (lane == HEAD_DIM, lane == 0)

    rt = min(ta, ATTN_ROWS)

    def step(k0s, masked):
        units = [(kb, r, h) for kb in range(len(k0s)) for r in range(ta // rt) for h in range(2)]

        def n_keys(r):
            return (r + 1) * rt if masked else ta

        va = []
        for k0 in k0s:
            v = v_ref[pl.ds(k0, ta), :]
            va.append([jnp.where(m0 if h == 0 else jnp.logical_not(m0), v,
                                 jnp.where(ones_lane[h], 1.0, 0.0).astype(BF16)) for h in range(2)])

        def scores(u):
            kb, r, h = u
            hs = slice(h * LANES, (h + 1) * LANES)
            return lax.dot_general(q_ref[r * rt:(r + 1) * rt, hs], k_ref[pl.ds(k0s[kb], n_keys(r)), hs],
                                   (((1,), (1,)), ((), ())), preferred_element_type=F32)

        def finish(u, sc):
            kb, r, h = u
            rows = slice(r * rt, (r + 1) * rt)
            if masked:
                rr = lax.broadcasted_iota(jnp.int32, sc.shape, 0) + r * rt
                cc = lax.broadcasted_iota(jnp.int32, sc.shape, 1)
                sc = jnp.where(cc <= rr, sc, -1e30)
            m_prev = m_scr[h, rows]
            m_new = jnp.maximum(m_prev, jnp.max(sc, axis=1, keepdims=True))
            p = jnp.exp2(sc - m_new).astype(BF16)
            alpha = jnp.exp2(m_prev - m_new)
            m_scr[h, rows] = m_new
            acc_scr[h, rows] = alpha * acc_scr[h, rows] + jnp.dot(p, va[kb][h][:n_keys(r)],
                                                                  preferred_element_type=F32)

        pending = [scores(u) for u in units[:ATTN_DEPTH]]
        for i, u in enumerate(units):
            sc = pending.pop(0)
            if i + ATTN_DEPTH < len(units):
                pending.append(scores(units[i + ATTN_DEPTH]))
            finish(u, sc)

    step([pl.multiple_of(qi * ta, ta)], True)
    first = first_ref[(pl.program_id(0) * PAIRS_FOX + pl.program_id(1)) * pl.num_programs(2) + qi]
    count = qi - first

    def body(kj, carry):
        k0 = pl.multiple_of((first + 2 * kj) * ta, ta)
        step([k0, k0 + ta], False)
        return carry

    lax.fori_loop(0, count // 2, body, 0)

    @pl.when(count % 2 == 1)
    def _():
        step([pl.multiple_of((qi - 1) * ta, ta)], False)

    a0, a1 = acc_scr[0], acc_scr[1]
    o0 = a0 / a0[:, HEAD_DIM:HEAD_DIM + 1]
    o1 = a1 / a1[:, 0:1]
    o_ref[...] = jnp.where(m0, o0, o1).astype(o_ref.dtype)


def _fox_first_block(cb, qg, kg):
    bsz, nq = cb.shape[:2]
    c_first = cb[:, :, 0, :FOX_HEADS]
    c_last = cb[:, :, 1, :FOX_HEADS]
    bound = 64.0 * jnp.max(jnp.abs(qg)) * jnp.max(jnp.abs(kg))
    reach = 2.0 * bound + c_first[:, :, None, :] - c_last[:, None, :, :]
    needed = (reach >= -EXP2_ZERO).reshape(bsz, nq, nq, PAIRS_FOX, 2).any(-1)
    earlier = jnp.arange(nq)[None, :] < jnp.arange(nq)[:, None]
    count = jnp.sum(needed & earlier[None, :, :, None], axis=2)
    first = jnp.arange(nq)[None, :, None] - count
    return jnp.transpose(first, (0, 2, 1)).reshape(-1).astype(jnp.int32)


def _fox_attn(first, q, k, v, ta):
    b, lp, d = v.shape
    grid_spec = pltpu.PrefetchScalarGridSpec(
        num_scalar_prefetch=1,
        grid=(b, PAIRS_FOX, lp // ta),
        in_specs=[pl.BlockSpec((None, ta, 2 * LANES), lambda i, h, t, f: (i, t, h)),
                  pl.BlockSpec((None, lp, 2 * LANES), lambda i, h, t, f: (i, 0, h)),
                  pl.BlockSpec((None, lp, LANES), lambda i, h, t, f: (i, 0, h))],
        out_specs=pl.BlockSpec((None, ta, LANES), lambda i, h, t, f: (i, t, h)),
        scratch_shapes=[pltpu.VMEM((2, ta, 1), F32),
                        pltpu.VMEM((2, ta, LANES), F32)],
    )
    return pl.pallas_call(
        functools.partial(_fox_attn_kernel, ta=ta),
        grid_spec=grid_spec,
        out_shape=jax.ShapeDtypeStruct((b, lp, d), BF16),
        compiler_params=pltpu.CompilerParams(
            dimension_semantics=("parallel", "parallel", "parallel"), vmem_limit_bytes=VMEM_LIMIT),
        name="fox_attention",
    )(first, q, k, v)


def _block_diag(blocks):
    g, r, c = blocks.shape
    eye = jnp.eye(g, dtype=blocks.dtype)
    return (eye[:, None, :, None] * blocks[:, :, None, :]).reshape(g * r, g * c)


def _cmul(ar, ai, br, bi):
    return ar * br - ai * bi, ar * bi + ai * br


def _s5_params(a_re, a_im, log_dt, b_re, b_im, c_re, c_im):
    dt = jnp.exp(log_dt)[:, None]
    mag = jnp.exp(a_re * dt)
    lam_re, lam_im = mag * jnp.cos(a_im * dt), mag * jnp.sin(a_im * dt)
    den = a_re * a_re + a_im * a_im
    z_re = ((lam_re - 1.0) * a_re + lam_im * a_im) / den
    z_im = (lam_im * a_re - (lam_re - 1.0) * a_im) / den
    bb_re = z_re[..., None] * b_re - z_im[..., None] * b_im
    bb_im = z_re[..., None] * b_im + z_im[..., None] * b_re
    to_in = lambda m: _block_diag(jnp.swapaxes(m, 1, 2))
    bm = jnp.concatenate([to_in(bb_re), to_in(bb_im)], axis=1).astype(BF16)
    to_out = lambda m: _block_diag(jnp.swapaxes(m, 1, 2))
    cm = jnp.concatenate([to_out(c_re), -to_out(c_im)], axis=0).astype(BF16)
    lr, li = lam_re.reshape(1, S5_LANES), lam_im.reshape(1, S5_LANES)
    n_lv = int(math.log2(S5_ROWS))
    pows, tr, ti = [], lr, li
    pr, pi = lr, li
    for _ in range(n_lv):
        pows += [pr, pi]
        nr, ni = _cmul(tr, ti, pr, pi)
        tr, ti = jnp.concatenate([tr, nr], axis=0), jnp.concatenate([ti, ni], axis=0)
        pr, pi = _cmul(pr, pi, pr, pi)
    pows = jnp.concatenate(pows + [jnp.zeros((16 - 2 * n_lv, S5_LANES), F32)], axis=0)
    return bm, cm, pows, jnp.stack([tr, ti])


def _rwkv_consts():
    idx = np.arange(LANES)
    hsum = (idx[:, None] // HEAD_DIM == idx[None, :] // HEAD_DIM).astype(np.float32)
    t = idx % CHUNK
    lv = [(t[:, None] // 2 == t[None, :] // 2)]
    b = 2
    while b < CHUNK:
        lv.append((t[:, None] // (2 * b) == t[None, :] // (2 * b)) & (t[:, None] // b != t[None, :] // b))
        b *= 2
    tri = (np.arange(CHUNK)[None, :] <= np.arange(CHUNK)[:, None]).astype(np.float32)
    return (jnp.asarray(hsum, BF16), jnp.asarray(tri, BF16), jnp.asarray(np.stack(lv).astype(np.float32)))


def _time_tile(length):
    return 768 if length > 2048 else 128


def kernel(x, meta_tokens, mix_norm, ffn_norm, even_w_in, even_w_out, rwkv_mu, rwkv_w0, rwkv_w2, rwkv_a0, rwkv_a2, rwkv_g2, rwkv_k_k, rwkv_k_a, rwkv_r_k, rwkv_ln_w, rwkv_ln_b, s5_a_re, s5_a_im, s5_log_dt, s5_b_re, s5_b_im, s5_c_re, s5_c_im, s5_d, s5_glu_w, s5_glu_b, odd_w_in, odd_w_out, fox_b_f, fox_q_gain, fox_k_gain, ffn_w_up, ffn_conv_w, ffn_conv_b, ffn_w_down):
    bsz, seq, d = x.shape
    length = N_META + seq
    tl = _time_tile(length)
    lp = -(-length // tl) * tl
    meta = jnp.broadcast_to(meta_tokens[None].astype(x.dtype), (bsz, N_META, d))
    h = jnp.concatenate([meta, x, jnp.zeros((bsz, lp - length, d), x.dtype)], axis=1)

    hsum, tri64, lvl = _rwkv_consts()
    n_ch = D_FF // FF_CHUNK

    def mix_ffn(h, xs, ws, layer):
        w_up = ffn_w_up[layer].astype(BF16)
        cw = ffn_conv_w[layer].reshape(3, 2, n_ch, FF_CHUNK).transpose(1, 2, 0, 3)
        cw = jnp.pad(cw, ((0, 0), (0, 0), (0, 5), (0, 0)))
        cb = ffn_conv_b[layer].reshape(2, n_ch, 1, FF_CHUNK)
        return _mix_ffn(h, xs, ws, ffn_norm[layer][None], w_up[:, :D_FF], w_up[:, D_FF:], cw, cb,
                        ffn_w_down[layer].astype(BF16), tl)

    p = _rms_matmul(h, mix_norm[0][None], even_w_in[0].astype(BF16), tl, BF16)
    pairs = lambda vec: vec.reshape(PAIRS_RWKV, 1, LANES)
    mu = rwkv_mu[0]
    prm = jnp.concatenate(
        [pairs(mu[:RWKV_DIM]), pairs(mu[RWKV_DIM:2 * RWKV_DIM]), pairs(mu[2 * RWKV_DIM:3 * RWKV_DIM]),
         pairs(rwkv_w0[0]), pairs(rwkv_a0[0]), pairs(rwkv_k_k[0]), pairs(rwkv_k_a[0]),
         pairs(rwkv_r_k[0].reshape(-1)), pairs(rwkv_ln_w[0]), pairs(rwkv_ln_b[0]),
         jnp.zeros((PAIRS_RWKV, 6, LANES), F32)], axis=1)
    zeros64 = jnp.zeros((64, RWKV_DIM), F32)
    w2p = jnp.concatenate([rwkv_w2[0], zeros64], axis=0).astype(BF16)
    a2p = jnp.concatenate([zeros64, rwkv_a2[0]], axis=0).astype(BF16)
    a_out = _rwkv(p, prm, mu[3 * RWKV_DIM:][None], w2p, a2p, rwkv_g2[0].astype(BF16),
                  hsum, tri64, lvl, tl, tl // CHUNK)
    bm, cm, pows, tab = _s5_params(s5_a_re[0], s5_a_im[0], s5_log_dt[0], s5_b_re[0], s5_b_im[0],
                                   s5_c_re[0], s5_c_im[0])
    b_out = _s5(p, bm, cm, pows, tab, s5_d[0][None], s5_glu_w[0].astype(BF16), s5_glu_b[0][None], tl)
    w_out = even_w_out[0].astype(BF16)
    h = mix_ffn(h, [a_out, b_out], [w_out[:RWKV_DIM], w_out[RWKV_DIM:]], 0)

    w_in = jnp.pad(odd_w_in[0], ((0, 0), (0, ODD_IN_PAD - odd_w_in.shape[-1]))).astype(BF16)
    scale = HEAD_DIM ** -0.5
    qg = jnp.tile(fox_q_gain[0], 2)[None] * (scale * LOG2E)
    kg = jnp.tile(fox_k_gain[0], 2)[None]
    bf = jnp.pad(fox_b_f[0], (0, LANES - FOX_HEADS))[None]
    tri_t = jnp.asarray(np.tril(np.ones((tl, tl), np.float32)), BF16)
    q, k, v, cb = _fox_in(h, mix_norm[1][None], w_in, qg, kg, bf, hsum * (1.0 / HEAD_DIM), tri_t,
                      *_fox_feature_consts(), tl)
    o = _fox_attn(_fox_first_block(cb, qg, kg), q, k, v, tl)
    h = mix_ffn(h, [o], [odd_w_out[0].astype(BF16)], 1)
    return h[:, N_META:N_META + seq]
```

```python
import functools
import math

import jax
import jax.numpy as jnp
import numpy as np
from jax import lax
from jax.experimental import pallas as pl
from jax.experimental.pallas import tpu as pltpu

F32 = jnp.float32
BF16 = jnp.bfloat16

D_MODEL = 1024
N_META = 16
RMS_EPS = 1e-6
GN_EPS = 64e-5

RWKV_HEADS = 12
HEAD_DIM = 64
RWKV_DIM = RWKV_HEADS * HEAD_DIM
LORA_COLS = 256
RWKV_COLS = 3 * RWKV_DIM + LORA_COLS
S5_DIM = 256
S5_GROUP = 16
S5_GROUPS = 16
S5_STATE = 64
S5_LANES = S5_GROUPS * S5_STATE
EVEN_IN = RWKV_COLS + S5_DIM
FOX_HEADS = 16
ODD_IN_PAD = 3 * D_MODEL + 128
D_FF = 2816
LANES = 128
PAIRS_RWKV = RWKV_DIM // LANES
PAIRS_FOX = D_MODEL // LANES
CHUNK = 64
S5_SUB = 128
S5_ROWS = 8
FF_CHUNK = 256
FF_DEPTH = 2
VMEM_LIMIT = 56 * 1024 * 1024
LOG2E = 1.4426950408889634
ATTN_ROWS = 256
ATTN_DEPTH = 3
EXP2_ZERO = 160.0


def _mm(a, b):
    return jnp.dot(a.astype(BF16), b.astype(BF16), preferred_element_type=F32)


def _mm_nt(a, b):
    return lax.dot_general(a.astype(BF16), b.astype(BF16), (((1,), (1,)), ((), ())),
                           preferred_element_type=F32)


def _mm_tn(a, b):
    return lax.dot_general(a.astype(BF16), b.astype(BF16), (((0,), (0,)), ((), ())),
                           preferred_element_type=F32)


def _split2_mm(x, m):
    hi = x.astype(BF16)
    lo = (x - hi.astype(F32)).astype(BF16)
    return (jnp.dot(hi, m, preferred_element_type=F32)
            + jnp.dot(lo, m, preferred_element_type=F32))


def _cumsum_rows(tri, x):
    h1 = x.astype(BF16)
    r1 = x - h1.astype(F32)
    h2 = r1.astype(BF16)
    h3 = (r1 - h2.astype(F32)).astype(BF16)
    w = x.shape[1]
    res = jnp.dot(tri, jnp.concatenate([h1, h2, h3], axis=1), preferred_element_type=F32)
    return res[:, :w] + res[:, w:2 * w] + res[:, 2 * w:]


def _softplus(z):
    return jnp.maximum(z, 0.0) + jnp.log(1.0 + jnp.exp(-jnp.abs(z)))


def _sigmoid(z):
    return 1.0 / (1.0 + jnp.exp(-z))


def _rms_rows(x, gain):
    ms = jnp.mean(x * x, axis=-1, keepdims=True)
    return x * lax.rsqrt(ms + RMS_EPS) * gain


def _shift_rows(x, k, prev_rows):
    rolled = pltpu.roll(x, k, 0)
    row = lax.broadcasted_iota(jnp.int32, x.shape, 0)
    for i in range(k):
        rolled = jnp.where(row == i, prev_rows[i], rolled)
    return rolled


def _rms_matmul_kernel(x_ref, g_ref, w_ref, o_ref, *, n_chunk):
    xn = _rms_rows(x_ref[...], g_ref[...]).astype(BF16)
    n = o_ref.shape[-1]
    for j in range(0, n, n_chunk):
        o_ref[:, j:j + n_chunk] = jnp.dot(xn, w_ref[:, j:j + n_chunk],
                                          preferred_element_type=F32).astype(o_ref.dtype)


def _rms_matmul(h, gain, w, tl, out_dtype):
    b, lp, d = h.shape
    n = w.shape[1]
    return pl.pallas_call(
        functools.partial(_rms_matmul_kernel, n_chunk=256),
        grid=(b, lp // tl),
        in_specs=[pl.BlockSpec((None, tl, d), lambda i, t: (i, t, 0)),
                  pl.BlockSpec((1, d), lambda i, t: (0, 0)),
                  pl.BlockSpec((d, n), lambda i, t: (0, 0))],
        out_specs=pl.BlockSpec((None, tl, n), lambda i, t: (i, t, 0)),
        out_shape=jax.ShapeDtypeStruct((b, lp, n), out_dtype),
        compiler_params=pltpu.CompilerParams(
            dimension_semantics=("parallel", "parallel"), vmem_limit_bytes=VMEM_LIMIT),
        name="rms_in_proj",
    )(h, gain, w)


def _rwkv_kernel(r_ref, k_ref, v_ref, lo_ref, prm_ref, mul_ref, w2_ref, a2_ref, g2_ref,
                 hsum_ref, tri_ref, lvl_ref, o_ref,
                 s_scr, prev_scr, prevl_scr, y_scr, q_scr, gh_scr, bg_scr, *, tl, group):
    t = pl.program_id(2)

    @pl.when(t == 0)
    def _():
        s_scr[...] = jnp.zeros_like(s_scr)
        prev_scr[...] = jnp.zeros_like(prev_scr)
        prevl_scr[...] = jnp.zeros_like(prevl_scr)
        gh_scr[...] = jnp.zeros_like(gh_scr)
        bg_scr[...] = jnp.zeros_like(bg_scr)

    prm = prm_ref[...]
    row = lambda i: prm[i:i + 1, :]
    hsum = hsum_ref[...]

    def tshift(x, prev, mu):
        prevx = _shift_rows(x, 1, [prev])
        return x + (prevx - x) * mu

    r_raw, k_raw, v_raw, lo_raw = (ref[...].astype(F32) for ref in (r_ref, k_ref, v_ref, lo_ref))
    r = tshift(r_raw, prev_scr[0:1, :], row(0))
    k = tshift(k_raw, prev_scr[1:2, :], row(1))
    v = tshift(v_raw, prev_scr[2:3, :], row(2))
    lo = tshift(lo_raw, prevl_scr[0:1, :], mul_ref[...])
    prev_scr[0:1, :] = r_raw[tl - 1:tl, :]
    prev_scr[1:2, :] = k_raw[tl - 1:tl, :]
    prev_scr[2:3, :] = v_raw[tl - 1:tl, :]
    prevl_scr[0:1, :] = lo_raw[tl - 1:tl, :]

    lo01 = lo[:, :LANES]
    w_pre = row(3) + _mm(jnp.tanh(lo01), w2_ref[...])
    logw = -jnp.exp(-_softplus(-w_pre) - 0.5)
    a = _sigmoid(row(4) + _mm(lo01, a2_ref[...]))
    g = _mm(_sigmoid(lo[:, LANES:]), g2_ref[...])
    kk = k * row(5)
    kk = kk / jnp.maximum(jnp.sqrt(_split2_mm(kk * kk, hsum)), 1e-12)
    kf = k * (1.0 + (a - 1.0) * row(6))
    ka = kk * a
    bonus = _split2_mm(r * kf * row(7), hsum) * v

    q_scr[0] = r
    q_scr[1] = logw
    q_scr[2] = kf
    q_scr[3] = v
    q_scr[4] = kk
    q_scr[5] = ka

    lane = lax.broadcasted_iota(jnp.int32, (1, LANES), 1)
    m0 = lane < HEAD_DIM
    tri = tri_ref[...]
    ri = lax.broadcasted_iota(jnp.int32, (LANES, LANES), 0)
    ci = lax.broadcasted_iota(jnp.int32, (LANES, LANES), 1)
    rt_, ct_ = ri & (CHUNK - 1), ci & (CHUNK - 1)
    strict = ct_ < rt_
    incl = ct_ <= rt_
    eye = ri == ci

    def stack(x):
        return jnp.concatenate([jnp.where(m0, x, 0.0), jnp.where(m0, 0.0, x)], axis=0).astype(BF16)

    def group_pre(bases, between):
        rng = range(len(bases))
        ld = [[q_scr[i, pl.ds(b, CHUNK), :] for i in range(6)] for b in bases]
        cum = [_cumsum_rows(tri, ld[j][1]) for j in rng]
        ops = []
        for j in rng:
            rc, lw, kc, vc, kkc, kac = ld[j]
            cum_c = cum[j][CHUNK - 1:CHUNK, :]
            w_in = jnp.exp(cum[j])
            w_inv = jnp.exp(-cum[j])
            w_ex = jnp.exp(cum[j] - lw)
            w_end = jnp.exp(cum_c - cum[j])
            ops.append(dict(
                at=stack(-kkc * w_ex), rt=stack(rc * w_in), bt=stack(kac * w_inv), kt=stack(kc * w_inv),
                bh=stack(kac * w_end), kh=stack(kc * w_end), v=stack(vc), wc=jnp.exp(cum_c)))
        am = [_mm_nt(jnp.concatenate([o["at"], o["rt"]], axis=0), jnp.concatenate([o["bt"], o["kt"]], axis=0))
              for o in ops]
        between()
        l_ab = [jnp.where(strict, m[:LANES, :LANES], 0.0) for m in am]
        a_ak = [jnp.where(strict, m[:LANES, LANES:], 0.0).astype(BF16) for m in am]
        m_rb = [jnp.where(incl, m[LANES:, :LANES], 0.0).astype(BF16) for m in am]
        m_rk = [jnp.where(incl, m[LANES:, LANES:], 0.0).astype(BF16) for m in am]
        lv0 = lvl_ref[0]
        tm = [jnp.where(eye, 1.0, 0.0) + l * lv0 for l in l_ab]
        av = [_mm(a_ak[j], ops[j]["v"]) for j in rng]
        between()
        for lv in range(1, 6):
            b = 1 << lv
            lvm = lvl_ref[lv]
            if b < 8:
                z = [_mm(l_ab[j] * lvm, tm[j]) for j in rng]
                between()
                tm = [tm[j] + _mm(tm[j], z[j]) for j in rng]
                between()
            else:
                odd = lambda x: jnp.concatenate([x[s:s + b] for s in range(b, LANES, 2 * b)], axis=0)
                lvm_o = odd(lvm)
                z_o = [_mm(odd(l_ab[j]) * lvm_o, tm[j]) for j in rng]
                between()
                zero = jnp.zeros((b, LANES), F32)

                def spread(y, base=None):
                    out = []
                    for i, s in enumerate(range(0, LANES, b)):
                        piece = y[(i // 2) * b:(i // 2 + 1) * b] if i % 2 else None
                        if base is None:
                            out.append(piece if i % 2 else zero)
                        else:
                            out.append(base[s:s + b] + piece if i % 2 else base[s:s + b])
                    return jnp.concatenate(out, axis=0)

                upd = [_mm(odd(tm[j]), spread(z_o[j])) for j in rng]
                tm = [spread(upd[j], tm[j]) for j in rng]
                between()
        pq = [_mm(tm[j], jnp.concatenate([ops[j]["at"], av[j].astype(BF16)], axis=1)) for j in rng]
        between()
        p_s = [m[:, :LANES].astype(BF16) for m in pq]
        qv = [jnp.concatenate([pq[j][:, LANES:].astype(BF16), ops[j]["v"]], axis=0) for j in rng]
        rp = [ops[j]["rt"].astype(F32) + _mm(m_rb[j], p_s[j]) for j in rng]
        y0 = [_mm(jnp.concatenate([m_rb[j], m_rk[j]], axis=1), qv[j]) for j in rng]
        gm = [jnp.where(eye, ops[j]["wc"], 0.0) + _mm_tn(ops[j]["bh"], p_s[j]) for j in rng]
        hm = [_mm_tn(jnp.concatenate([ops[j]["bh"], ops[j]["kh"]], axis=0), qv[j]) for j in rng]
        return rp, y0, gm, hm

    chain = {"j": 0, "s": s_scr[...]}

    def chain_step():
        j, s = chain["j"], chain["s"]
        if j == group:
            return
        ys = _mm(gh_scr[j, 0], s) + gh_scr[j, 1]
        y_scr[j * CHUNK:(j + 1) * CHUNK, :] = ys[:CHUNK] + ys[CHUNK:]
        chain["s"] = _mm(gh_scr[j, 2], s) + gh_scr[j, 3]
        chain["j"] = j + 1

    new_ops = group_pre([j * CHUNK for j in range(group)], chain_step)
    while chain["j"] < group:
        chain_step()
    s_scr[...] = chain["s"]

    y = y_scr[...]
    mean = _split2_mm(y, hsum) * (1.0 / HEAD_DIM)
    yc = y - mean
    var = _split2_mm(yc * yc, hsum) * (1.0 / HEAD_DIM)
    yn = yc * lax.rsqrt(var + GN_EPS) * row(8) + row(9)
    o_ref[...] = ((yn + bg_scr[0]) * bg_scr[1]).astype(o_ref.dtype)

    for i, vals in enumerate(new_ops):
        for j in range(group):
            gh_scr[j, i] = vals[j]
    bg_scr[0] = bonus
    bg_scr[1] = g


def _rwkv(p, prm, mu_l, w2p, a2p, g2, hsum, tri, lvl, tl, group):
    b, lp, _ = p.shape
    nt = lp // tl
    cur = lambda t: jnp.minimum(t, nt - 1)
    blk = lambda off: pl.BlockSpec((None, tl, LANES), lambda i, h, t: (i, cur(t), off + h))
    const2 = lambda shape: pl.BlockSpec(shape, lambda i, h, t: (0, 0))
    return pl.pallas_call(
        functools.partial(_rwkv_kernel, tl=tl, group=group),
        grid=(b, PAIRS_RWKV, nt + 1),
        in_specs=[blk(0), blk(PAIRS_RWKV), blk(2 * PAIRS_RWKV),
                  pl.BlockSpec((None, tl, LORA_COLS), lambda i, h, t: (i, cur(t), 3 * RWKV_DIM // LORA_COLS)),
                  pl.BlockSpec((None, 16, LANES), lambda i, h, t: (h, 0, 0)),
                  const2((1, LORA_COLS)),
                  pl.BlockSpec((LANES, LANES), lambda i, h, t: (0, h)),
                  pl.BlockSpec((LANES, LANES), lambda i, h, t: (0, h)),
                  pl.BlockSpec((LANES, LANES), lambda i, h, t: (0, h)),
                  const2((LANES, LANES)),
                  const2((CHUNK, CHUNK)),
                  pl.BlockSpec((6, LANES, LANES), lambda i, h, t: (0, 0, 0))],
        out_specs=pl.BlockSpec((None, tl, LANES), lambda i, h, t: (i, jnp.maximum(t - 1, 0), h)),
        out_shape=jax.ShapeDtypeStruct((b, lp, RWKV_DIM), BF16),
        scratch_shapes=[pltpu.VMEM((LANES, LANES), F32),
                        pltpu.VMEM((8, LANES), F32),
                        pltpu.VMEM((8, LORA_COLS), F32),
                        pltpu.VMEM((tl, LANES), F32),
                        pltpu.VMEM((6, tl, LANES), F32),
                        pltpu.VMEM((group, 4, LANES, LANES), F32),
                        pltpu.VMEM((2, tl, LANES), F32)],
        compiler_params=pltpu.CompilerParams(
            dimension_semantics=("parallel", "parallel", "arbitrary"), vmem_limit_bytes=VMEM_LIMIT),
        name="rwkv7_mix",
    )(p, p, p, p, prm, mu_l, w2p, a2p, g2, hsum, tri, lvl)


def _s5_kernel(u_ref, bm_ref, cm_ref, lam_ref, tab_ref, d_ref, gw_ref, gb_ref, o_ref,
               h0_scr, hs_scr, *, tl):
    t = pl.program_id(1)

    @pl.when(t == 0)
    def _():
        h0_scr[...] = jnp.zeros_like(h0_scr)

    n_lv = int(math.log2(S5_ROWS))
    n_grp = S5_SUB // S5_ROWS
    row_i = lax.broadcasted_iota(jnp.int32, (n_grp, S5_ROWS, LANES), 1)

    def sub_body(si, carry):
        r0 = pl.multiple_of(si * S5_SUB, S5_SUB)
        u = u_ref[pl.ds(r0, S5_SUB), :].astype(F32)
        bu = _mm(u, bm_ref[...])
        for lb in range(S5_LANES // LANES):
            ls = slice(lb * LANES, (lb + 1) * LANES)
            li = slice(S5_LANES + lb * LANES, S5_LANES + (lb + 1) * LANES)
            hr = bu[:, ls].reshape(n_grp, S5_ROWS, LANES)
            hi = bu[:, li].reshape(n_grp, S5_ROWS, LANES)
            for lv in range(n_lv):
                sh = 1 << lv
                lr = lam_ref[2 * lv:2 * lv + 1, ls]
                lm = lam_ref[2 * lv + 1:2 * lv + 2, ls]
                keep = row_i >= sh
                sr = jnp.where(keep, pltpu.roll(hr, sh, 1), 0.0)
                sm = jnp.where(keep, pltpu.roll(hi, sh, 1), 0.0)
                hr, hi = hr + (lr * sr - lm * sm), hi + (lr * sm + lm * sr)
            tr, tm = tab_ref[0, :, ls], tab_ref[1, :, ls]
            cr, ci = h0_scr[0:1, ls], h0_scr[1:2, ls]
            out_r, out_i = [], []
            for g in range(n_grp):
                gr = hr[g] + (tr * cr - tm * ci)
                gi = hi[g] + (tr * ci + tm * cr)
                cr, ci = gr[S5_ROWS - 1:S5_ROWS, :], gi[S5_ROWS - 1:S5_ROWS, :]
                out_r.append(gr)
                out_i.append(gi)
            h0_scr[0:1, ls] = cr
            h0_scr[1:2, ls] = ci
            hs_scr[:, ls] = jnp.concatenate(out_r, axis=0).astype(BF16)
            hs_scr[:, li] = jnp.concatenate(out_i, axis=0).astype(BF16)
        y = jnp.dot(hs_scr[...], cm_ref[...], preferred_element_type=F32) + d_ref[...] * u
        y = 0.5 * y * (1.0 + jnp.tanh(0.7978845608028654 * (y + 0.044715 * (y * y * y))))
        z = _mm(y, gw_ref[...]) + gb_ref[...]
        o_ref[pl.ds(r0, S5_SUB), :] = (y * _sigmoid(z)).astype(o_ref.dtype)
        return carry

    lax.fori_loop(0, tl // S5_SUB, sub_body, 0)


def _s5(p, bm, cm, lam_pows, tab, d_skip, glu_w, glu_b, tl):
    b, lp, _ = p.shape
    c2 = lambda a: pl.BlockSpec(a.shape, lambda i, t: (0,) * a.ndim)
    return pl.pallas_call(
        functools.partial(_s5_kernel, tl=tl),
        grid=(b, lp // tl),
        in_specs=[pl.BlockSpec((None, tl, S5_DIM), lambda i, t: (i, t, RWKV_COLS // S5_DIM)),
                  c2(bm), c2(cm), c2(lam_pows), c2(tab), c2(d_skip), c2(glu_w), c2(glu_b)],
        out_specs=pl.BlockSpec((None, tl, S5_DIM), lambda i, t: (i, t, 0)),
        out_shape=jax.ShapeDtypeStruct((b, lp, S5_DIM), BF16),
        scratch_shapes=[pltpu.VMEM((8, S5_LANES), F32),
                        pltpu.VMEM((S5_SUB, 2 * S5_LANES), BF16)],
        compiler_params=pltpu.CompilerParams(
            dimension_semantics=("parallel", "arbitrary"), vmem_limit_bytes=VMEM_LIMIT),
        name="s5_mix",
    )(p, bm, cm, lam_pows, tab, d_skip, glu_w, glu_b)


def _mix_ffn_kernel(*refs, tl, n_in):
    h_ref = refs[0]
    xs = refs[1:1 + n_in]
    ws = refs[1 + n_in:1 + 2 * n_in]
    g_ref, wg_ref, wv_ref, cw_ref, cb_ref, wd_ref, o_ref, xn_scr, prev_scr = refs[1 + 2 * n_in:]
    t = pl.program_id(1)
    n_ch = D_FF // FF_CHUNK

    @pl.when(t == 0)
    def _():
        prev_scr[...] = jnp.zeros_like(prev_scr)

    x = h_ref[...]
    for x_ref, w_ref in zip(xs, ws):
        x = x + jnp.dot(x_ref[...], w_ref[...], preferred_element_type=F32)
    xn_scr[...] = _rms_rows(x, g_ref[...]).astype(BF16)
    o_ref[...] = x

    def conv(hh, prev8, cw, cb):
        p6, p7 = prev8[6:7, :], prev8[7:8, :]
        s1 = _shift_rows(hh, 1, [p7])
        s2 = _shift_rows(hh, 2, [p6, p7])
        return cw[0:1, :] * s2 + cw[1:2, :] * s1 + cw[2:3, :] * hh + cb

    def up(c):
        cs = slice(c * FF_CHUNK, (c + 1) * FF_CHUNK)
        xn = xn_scr[...]
        return (jnp.dot(xn, wg_ref[:, cs], preferred_element_type=F32),
                jnp.dot(xn, wv_ref[:, cs], preferred_element_type=F32))

    pending = [up(c) for c in range(FF_DEPTH)]
    for c in range(n_ch):
        hg, hv = pending.pop(0)
        if c + FF_DEPTH < n_ch:
            pending.append(up(c + FF_DEPTH))
        cg = conv(hg, prev_scr[0, c], cw_ref[0, c], cb_ref[0, c])
        cv = conv(hv, prev_scr[1, c], cw_ref[1, c], cb_ref[1, c])
        prev_scr[0, c] = hg[tl - 8:tl, :]
        prev_scr[1, c] = hv[tl - 8:tl, :]
        act = (cg * _sigmoid(cg) * cv).astype(BF16)
        o_ref[...] += jnp.dot(act, wd_ref[c * FF_CHUNK:(c + 1) * FF_CHUNK, :], preferred_element_type=F32)


def _mix_ffn(h, xs, ws, gain, wg, wv, cw, cb, wd, tl):
    b, lp, d = h.shape
    n_ch = D_FF // FF_CHUNK
    full = lambda a: pl.BlockSpec(a.shape, lambda i, t: (0,) * a.ndim, pipeline_mode=pl.Buffered(1))
    row = lambda n: pl.BlockSpec((None, tl, n), lambda i, t: (i, t, 0))
    consts = (*ws, gain, wg, wv, cw, cb, wd)
    return pl.pallas_call(
        functools.partial(_mix_ffn_kernel, tl=tl, n_in=len(xs)),
        grid=(b, lp // tl),
        in_specs=[row(d)] + [row(x.shape[-1]) for x in xs] + [full(a) for a in consts],
        out_specs=row(d),
        out_shape=jax.ShapeDtypeStruct((b, lp, d), F32),
        scratch_shapes=[pltpu.VMEM((tl, d), BF16),
                        pltpu.VMEM((2, n_ch, 8, FF_CHUNK), F32)],
        compiler_params=pltpu.CompilerParams(
            dimension_semantics=("parallel", "arbitrary"), vmem_limit_bytes=VMEM_LIMIT),
        name="mix_out_mlp",
    )(h, *xs, *consts)


def _fox_in_kernel(x_ref, g_ref, w_ref, qg_ref, kg_ref, bf_ref, avg_ref, tri_ref,
                   pq_ref, pk_ref, oq_ref, ok_ref, q_ref, k_ref, v_ref, cb_ref, carry_scr, *, tl):
    t = pl.program_id(1)

    @pl.when(t == 0)
    def _():
        carry_scr[...] = jnp.zeros_like(carry_scr)

    xn = _rms_rows(x_ref[...], g_ref[...]).astype(BF16)
    v_ref[...] = jnp.dot(xn, w_ref[:, 2 * D_MODEL:3 * D_MODEL], preferred_element_type=F32).astype(BF16)
    f = jnp.dot(xn, w_ref[:, 3 * D_MODEL:], preferred_element_type=F32) + bf_ref[...]
    log_f = jnp.minimum(f, 0.0) - jnp.log(1.0 + jnp.exp(-jnp.abs(f)))
    cum = _cumsum_rows(tri_ref[...], log_f) + carry_scr[0:1, :]
    carry_scr[0:1, :] = cum[tl - 1:tl, :]
    lane = lax.broadcasted_iota(jnp.int32, (1, LANES), 1)
    c = jnp.where(lane < FOX_HEADS, cum * LOG2E, 0.0)
    cb_ref[...] = jnp.concatenate([c[0:1, :], c[tl - 1:tl, :], jnp.zeros((6, LANES), F32)], axis=0)
    c1 = c.astype(BF16).astype(F32)
    c2 = (c - c1).astype(BF16).astype(F32)
    c3 = (c - c1 - c2).astype(BF16).astype(F32)
    c123 = (c1 + pltpu.roll(c2, FOX_HEADS, 1) + pltpu.roll(c3, 2 * FOX_HEADS, 1)).astype(BF16)
    feat_q = jnp.dot(c123, pq_ref[...], preferred_element_type=F32) + oq_ref[...]
    feat_k = jnp.dot(c123, pk_ref[...], preferred_element_type=F32) + ok_ref[...]

    avg = avg_ref[...]
    m0 = lane < HEAD_DIM
    for off, gain_ref, feat, dst in ((0, qg_ref, feat_q, q_ref), (D_MODEL, kg_ref, feat_k, k_ref)):
        z_all = jnp.dot(xn, w_ref[:, off:off + D_MODEL], preferred_element_type=F32)
        for s in range(PAIRS_FOX):
            z = z_all[:, s * LANES:(s + 1) * LANES]
            ms = _mm(z * z, avg)
            zn = z * lax.rsqrt(ms + RMS_EPS) * gain_ref[...]
            for hh in range(2):
                hs = slice((2 * s + hh) * LANES, (2 * s + hh + 1) * LANES)
                own = m0 if hh == 0 else jnp.logical_not(m0)
                dst[:, hs] = jnp.where(own, zn, feat[:, hs]).astype(BF16)


def _fox_in(h, gain, w, qg, kg, bf, avg, tri, pq, pk, oq, ok, tl):
    b, lp, d = h.shape
    c2 = lambda a: pl.BlockSpec(a.shape, lambda i, t: (0,) * a.ndim)
    act = lambda n: (pl.BlockSpec((None, tl, n), lambda i, t: (i, t, 0)),
                     jax.ShapeDtypeStruct((b, lp, n), BF16))
    bounds = (pl.BlockSpec((None, None, 8, LANES), lambda i, t: (i, t, 0, 0)),
              jax.ShapeDtypeStruct((b, lp // tl, 8, LANES), F32))
    specs, shapes = zip(act(2 * d), act(2 * d), act(d), bounds)
    consts = (gain, w, qg, kg, bf, avg, tri, pq, pk, oq, ok)
    return pl.pallas_call(
        functools.partial(_fox_in_kernel, tl=tl),
        grid=(b, lp // tl),
        in_specs=[pl.BlockSpec((None, tl, d), lambda i, t: (i, t, 0))] + [c2(a) for a in consts],
        out_specs=list(specs),
        out_shape=list(shapes),
        scratch_shapes=[pltpu.VMEM((8, LANES), F32)],
        compiler_params=pltpu.CompilerParams(
            dimension_semantics=("parallel", "arbitrary"), vmem_limit_bytes=VMEM_LIMIT),
        name="fox_in_proj",
    )(h, *consts)


def _fox_feature_consts():
    pq = np.zeros((LANES, FOX_HEADS * LANES), np.float32)
    pk = np.zeros_like(pq)
    oq = np.zeros((1, FOX_HEADS * LANES), np.float32)
    ok = np.zeros_like(oq)
    for h in range(FOX_HEADS):
        base = h * LANES + HEAD_DIM * (1 - h % 2)
        for i in range(3):
            pq[i * FOX_HEADS + h, base + i] = 1.0
            pk[i * FOX_HEADS + h, base + 3 + i] = -1.0
            oq[0, base + 3 + i] = 1.0
            ok[0, base + i] = 1.0
    return jnp.asarray(pq, BF16), jnp.asarray(pk, BF16), jnp.asarray(oq), jnp.asarray(ok)


def _fox_attn_kernel(first_ref, q_ref, k_ref, v_ref, o_ref, m_scr, acc_scr, *, ta):
    qi = pl.program_id(2)
    m_scr[...] = jnp.full_like(m_scr, -1e30)
    acc_scr[...] = jnp.zeros_like(acc_scr)

    lane = lax.broadcasted_iota(jnp.int32, (1, LANES), 1)
    m0 = lane < HEAD_DIM
    ones_lane = (lane == HEAD_DIM, lane == 0)

    rt = min(ta, ATTN_ROWS)

    def step(k0s, masked):
        units = [(kb, r, h) for kb in range(len(k0s)) for r in range(ta // rt) for h in range(2)]

        def n_keys(r):
            return (r + 1) * rt if masked else ta

        va = []
        for k0 in k0s:
            v = v_ref[pl.ds(k0, ta), :]
            va.append([jnp.where(m0 if h == 0 else jnp.logical_not(m0), v,
                                 jnp.where(ones_lane[h], 1.0, 0.0).astype(BF16)) for h in range(2)])

        def scores(u):
            kb, r, h = u
            hs = slice(h * LANES, (h + 1) * LANES)
            return lax.dot_general(q_ref[r * rt:(r + 1) * rt, hs], k_ref[pl.ds(k0s[kb], n_keys(r)), hs],
                                   (((1,), (1,)), ((), ())), preferred_element_type=F32)

        def finish(u, sc):
            kb, r, h = u
            rows = slice(r * rt, (r + 1) * rt)
            if masked:
                rr = lax.broadcasted_iota(jnp.int32, sc.shape, 0) + r * rt
                cc = lax.broadcasted_iota(jnp.int32, sc.shape, 1)
                sc = jnp.where(cc <= rr, sc, -1e30)
            m_prev = m_scr[h, rows]
            m_new = jnp.maximum(m_prev, jnp.max(sc, axis=1, keepdims=True))
            p = jnp.exp2(sc - m_new).astype(BF16)
            alpha = jnp.exp2(m_prev - m_new)
            m_scr[h, rows] = m_new
            acc_scr[h, rows] = alpha * acc_scr[h, rows] + jnp.dot(p, va[kb][h][:n_keys(r)],
                                                                  preferred_element_type=F32)

        pending = [scores(u) for u in units[:ATTN_DEPTH]]
        for i, u in enumerate(units):
            sc = pending.pop(0)
            if i + ATTN_DEPTH < len(units):
                pending.append(scores(units[i + ATTN_DEPTH]))
            finish(u, sc)

    step([pl.multiple_of(qi * ta, ta)], True)
    first = first_ref[(pl.program_id(0) * PAIRS_FOX + pl.program_id(1)) * pl.num_programs(2) + qi]
    count = qi - first

    def body(kj, carry):
        k0 = pl.multiple_of((first + 2 * kj) * ta, ta)
        step([k0, k0 + ta], False)
        return carry

    lax.fori_loop(0, count // 2, body, 0)

    @pl.when(count % 2 == 1)
    def _():
        step([pl.multiple_of((qi - 1) * ta, ta)], False)

    a0, a1 = acc_scr[0], acc_scr[1]
    o0 = a0 / a0[:, HEAD_DIM:HEAD_DIM + 1]
    o1 = a1 / a1[:, 0:1]
    o_ref[...] = jnp.where(m0, o0, o1).astype(o_ref.dtype)


def _fox_first_block(cb, qg, kg):
    bsz, nq = cb.shape[:2]
    c_first = cb[:, :, 0, :FOX_HEADS]
    c_last = cb[:, :, 1, :FOX_HEADS]
    bound = 64.0 * jnp.max(jnp.abs(qg)) * jnp.max(jnp.abs(kg))
    reach = 2.0 * bound + c_first[:, :, None, :] - c_last[:, None, :, :]
    needed = (reach >= -EXP2_ZERO).reshape(bsz, nq, nq, PAIRS_FOX, 2).any(-1)
    earlier = jnp.arange(nq)[None, :] < jnp.arange(nq)[:, None]
    count = jnp.sum(needed & earlier[None, :, :, None], axis=2)
    first = jnp.arange(nq)[None, :, None] - count
    return jnp.transpose(first, (0, 2, 1)).reshape(-1).astype(jnp.int32)


def _fox_attn(first, q, k, v, ta):
    b, lp, d = v.shape
    grid_spec = pltpu.PrefetchScalarGridSpec(
        num_scalar_prefetch=1,
        grid=(b, PAIRS_FOX, lp // ta),
        in_specs=[pl.BlockSpec((None, ta, 2 * LANES), lambda i, h, t, f: (i, t, h)),
                  pl.BlockSpec((None, lp, 2 * LANES), lambda i, h, t, f: (i, 0, h)),
                  pl.BlockSpec((None, lp, LANES), lambda i, h, t, f: (i, 0, h))],
        out_specs=pl.BlockSpec((None, ta, LANES), lambda i, h, t, f: (i, t, h)),
        scratch_shapes=[pltpu.VMEM((2, ta, 1), F32),
                        pltpu.VMEM((2, ta, LANES), F32)],
    )
    return pl.pallas_call(
        functools.partial(_fox_attn_kernel, ta=ta),
        grid_spec=grid_spec,
        out_shape=jax.ShapeDtypeStruct((b, lp, d), BF16),
        compiler_params=pltpu.CompilerParams(
            dimension_semantics=("parallel", "parallel", "parallel"), vmem_limit_bytes=VMEM_LIMIT),
        name="fox_attention",
    )(first, q, k, v)


def _block_diag(blocks):
    g, r, c = blocks.shape
    eye = jnp.eye(g, dtype=blocks.dtype)
    return (eye[:, None, :, None] * blocks[:, :, None, :]).reshape(g * r, g * c)


def _cmul(ar, ai, br, bi):
    return ar * br - ai * bi, ar * bi + ai * br


def _s5_params(a_re, a_im, log_dt, b_re, b_im, c_re, c_im):
    dt = jnp.exp(log_dt)[:, None]
    mag = jnp.exp(a_re * dt)
    lam_re, lam_im = mag * jnp.cos(a_im * dt), mag * jnp.sin(a_im * dt)
    den = a_re * a_re + a_im * a_im
    z_re = ((lam_re - 1.0) * a_re + lam_im * a_im) / den
    z_im = (lam_im * a_re - (lam_re - 1.0) * a_im) / den
    bb_re = z_re[..., None] * b_re - z_im[..., None] * b_im
    bb_im = z_re[..., None] * b_im + z_im[..., None] * b_re
    to_in = lambda m: _block_diag(jnp.swapaxes(m, 1, 2))
    bm = jnp.concatenate([to_in(bb_re), to_in(bb_im)], axis=1).astype(BF16)
    to_out = lambda m: _block_diag(jnp.swapaxes(m, 1, 2))
    cm = jnp.concatenate([to_out(c_re), -to_out(c_im)], axis=0).astype(BF16)
    lr, li = lam_re.reshape(1, S5_LANES), lam_im.reshape(1, S5_LANES)
    n_lv = int(math.log2(S5_ROWS))
    pows, tr, ti = [], lr, li
    pr, pi = lr, li
    for _ in range(n_lv):
        pows += [pr, pi]
        nr, ni = _cmul(tr, ti, pr, pi)
        tr, ti = jnp.concatenate([tr, nr], axis=0), jnp.concatenate([ti, ni], axis=0)
        pr, pi = _cmul(pr, pi, pr, pi)
    pows = jnp.concatenate(pows + [jnp.zeros((16 - 2 * n_lv, S5_LANES), F32)], axis=0)
    return bm, cm, pows, jnp.stack([tr, ti])


def _rwkv_consts():
    idx = np.arange(LANES)
    hsum = (idx[:, None] // HEAD_DIM == idx[None, :] // HEAD_DIM).astype(np.float32)
    t = idx % CHUNK
    lv = [(t[:, None] // 2 == t[None, :] // 2)]
    b = 2
    while b < CHUNK:
        lv.append((t[:, None] // (2 * b) == t[None, :] // (2 * b)) & (t[:, None] // b != t[None, :] // b))
        b *= 2
    tri = (np.arange(CHUNK)[None, :] <= np.arange(CHUNK)[:, None]).astype(np.float32)
    return (jnp.asarray(hsum, BF16), jnp.asarray(tri, BF16), jnp.asarray(np.stack(lv).astype(np.float32)))


def _time_tile(length):
    return 768 if length > 2048 else 128


def kernel(x, meta_tokens, mix_norm, ffn_norm, even_w_in, even_w_out, rwkv_mu, rwkv_w0, rwkv_w2, rwkv_a0, rwkv_a2, rwkv_g2, rwkv_k_k, rwkv_k_a, rwkv_r_k, rwkv_ln_w, rwkv_ln_b, s5_a_re, s5_a_im, s5_log_dt, s5_b_re, s5_b_im, s5_c_re, s5_c_im, s5_d, s5_glu_w, s5_glu_b, odd_w_in, odd_w_out, fox_b_f, fox_q_gain, fox_k_gain, ffn_w_up, ffn_conv_w, ffn_conv_b, ffn_w_down):
    bsz, seq, d = x.shape
    length = N_META + seq
    tl = _time_tile(length)
    lp = -(-length // tl) * tl
    meta = jnp.broadcast_to(meta_tokens[None].astype(x.dtype), (bsz, N_META, d))
    h = jnp.concatenate([meta, x, jnp.zeros((bsz, lp - length, d), x.dtype)], axis=1)

    hsum, tri64, lvl = _rwkv_consts()
    n_ch = D_FF // FF_CHUNK

    def mix_ffn(h, xs, ws, layer):
        w_up = ffn_w_up[layer].astype(BF16)
        cw = ffn_conv_w[layer].reshape(3, 2, n_ch, FF_CHUNK).transpose(1, 2, 0, 3)
        cw = jnp.pad(cw, ((0, 0), (0, 0), (0, 5), (0, 0)))
        cb = ffn_conv_b[layer].reshape(2, n_ch, 1, FF_CHUNK)
        return _mix_ffn(h, xs, ws, ffn_norm[layer][None], w_up[:, :D_FF], w_up[:, D_FF:], cw, cb,
                        ffn_w_down[layer].astype(BF16), tl)

    p = _rms_matmul(h, mix_norm[0][None], even_w_in[0].astype(BF16), tl, BF16)
    pairs = lambda vec: vec.reshape(PAIRS_RWKV, 1, LANES)
    mu = rwkv_mu[0]
    prm = jnp.concatenate(
        [pairs(mu[:RWKV_DIM]), pairs(mu[RWKV_DIM:2 * RWKV_DIM]), pairs(mu[2 * RWKV_DIM:3 * RWKV_DIM]),
         pairs(rwkv_w0[0]), pairs(rwkv_a0[0]), pairs(rwkv_k_k[0]), pairs(rwkv_k_a[0]),
         pairs(rwkv_r_k[0].reshape(-1)), pairs(rwkv_ln_w[0]), pairs(rwkv_ln_b[0]),
         jnp.zeros((PAIRS_RWKV, 6, LANES), F32)], axis=1)
    zeros64 = jnp.zeros((64, RWKV_DIM), F32)
    w2p = jnp.concatenate([rwkv_w2[0], zeros64], axis=0).astype(BF16)
    a2p = jnp.concatenate([zeros64, rwkv_a2[0]], axis=0).astype(BF16)
    a_out = _rwkv(p, prm, mu[3 * RWKV_DIM:][None], w2p, a2p, rwkv_g2[0].astype(BF16),
                  hsum, tri64, lvl, tl, tl // CHUNK)
    bm, cm, pows, tab = _s5_params(s5_a_re[0], s5_a_im[0], s5_log_dt[0], s5_b_re[0], s5_b_im[0],
                                   s5_c_re[0], s5_c_im[0])
    b_out = _s5(p, bm, cm, pows, tab, s5_d[0][None], s5_glu_w[0].astype(BF16), s5_glu_b[0][None], tl)
    w_out = even_w_out[0].astype(BF16)
    h = mix_ffn(h, [a_out, b_out], [w_out[:RWKV_DIM], w_out[RWKV_DIM:]], 0)

    w_in = jnp.pad(odd_w_in[0], ((0, 0), (0, ODD_IN_PAD - odd_w_in.shape[-1]))).astype(BF16)
    scale = HEAD_DIM ** -0.5
    qg = jnp.tile(fox_q_gain[0], 2)[None] * (scale * LOG2E)
    kg = jnp.tile(fox_k_gain[0], 2)[None]
    bf = jnp.pad(fox_b_f[0], (0, LANES - FOX_HEADS))[None]
    tri_t = jnp.asarray(np.tril(np.ones((tl, tl), np.float32)), BF16)
    q, k, v, cb = _fox_in(h, mix_norm[1][None], w_in, qg, kg, bf, hsum * (1.0 / HEAD_DIM), tri_t,
                      *_fox_feature_consts(), tl)
    o = _fox_attn(_fox_first_block(cb, qg, kg), q, k, v, tl)
    h = mix_ffn(h, [o], [odd_w_out[0].astype(BF16)], 1)
    return h[:, N_META:N_META + seq]
```

```python
import functools
import math

import jax
import jax.numpy as jnp
import numpy as np
from jax import lax
from jax.experimental import pallas as pl
from jax.experimental.pallas import tpu as pltpu

F32 = jnp.float32
BF16 = jnp.bfloat16

D_MODEL = 1024
N_META = 16
RMS_EPS = 1e-6
GN_EPS = 64e-5

RWKV_HEADS = 12
HEAD_DIM = 64
RWKV_DIM = RWKV_HEADS * HEAD_DIM
LORA_COLS = 256
RWKV_COLS = 3 * RWKV_DIM + LORA_COLS
S5_DIM = 256
S5_GROUP = 16
S5_GROUPS = 16
S5_STATE = 64
S5_LANES = S5_GROUPS * S5_STATE
EVEN_IN = RWKV_COLS + S5_DIM
FOX_HEADS = 16
ODD_IN_PAD = 3 * D_MODEL + 128
D_FF = 2816
LANES = 128
PAIRS_RWKV = RWKV_DIM // LANES
PAIRS_FOX = D_MODEL // LANES
CHUNK = 64
S5_SUB = 128
S5_ROWS = 8
FF_CHUNK = 256
FF_DEPTH = 2
VMEM_LIMIT = 56 * 1024 * 1024
LOG2E = 1.4426950408889634
ATTN_ROWS = 256
ATTN_DEPTH = 3
EXP2_ZERO = 160.0


def _mm(a, b):
    return jnp.dot(a.astype(BF16), b.astype(BF16), preferred_element_type=F32)


def _mm_nt(a, b):
    return lax.dot_general(a.astype(BF16), b.astype(BF16), (((1,), (1,)), ((), ())),
                           preferred_element_type=F32)


def _mm_tn(a, b):
    return lax.dot_general(a.astype(BF16), b.astype(BF16), (((0,), (0,)), ((), ())),
                           preferred_element_type=F32)


def _split2_mm(x, m):
    hi = x.astype(BF16)
    lo = (x - hi.astype(F32)).astype(BF16)
    return (jnp.dot(hi, m, preferred_element_type=F32)
            + jnp.dot(lo, m, preferred_element_type=F32))


def _cumsum_rows(tri, x):
    h1 = x.astype(BF16)
    r1 = x - h1.astype(F32)
    h2 = r1.astype(BF16)
    h3 = (r1 - h2.astype(F32)).astype(BF16)
    w = x.shape[1]
    res = jnp.dot(tri, jnp.concatenate([h1, h2, h3], axis=1), preferred_element_type=F32)
    return res[:, :w] + res[:, w:2 * w] + res[:, 2 * w:]


def _softplus(z):
    return jnp.maximum(z, 0.0) + jnp.log(1.0 + jnp.exp(-jnp.abs(z)))


def _sigmoid(z):
    return 1.0 / (1.0 + jnp.exp(-z))


def _rms_rows(x, gain):
    ms = jnp.mean(x * x, axis=-1, keepdims=True)
    return x * lax.rsqrt(ms + RMS_EPS) * gain


def _shift_rows(x, k, prev_rows):
    rolled = pltpu.roll(x, k, 0)
    row = lax.broadcasted_iota(jnp.int32, x.shape, 0)
    for i in range(k):
        rolled = jnp.where(row == i, prev_rows[i], rolled)
    return rolled


def _rms_matmul_kernel(x_ref, g_ref, w_ref, o_ref, *, n_chunk):
    xn = _rms_rows(x_ref[...], g_ref[...]).astype(BF16)
    n = o_ref.shape[-1]
    for j in range(0, n, n_chunk):
        o_ref[:, j:j + n_chunk] = jnp.dot(xn, w_ref[:, j:j + n_chunk],
                                          preferred_element_type=F32).astype(o_ref.dtype)


def _rms_matmul(h, gain, w, tl, out_dtype):
    b, lp, d = h.shape
    n = w.shape[1]
    return pl.pallas_call(
        functools.partial(_rms_matmul_kernel, n_chunk=256),
        grid=(b, lp // tl),
        in_specs=[pl.BlockSpec((None, tl, d), lambda i, t: (i, t, 0)),
                  pl.BlockSpec((1, d), lambda i, t: (0, 0)),
                  pl.BlockSpec((d, n), lambda i, t: (0, 0))],
        out_specs=pl.BlockSpec((None, tl, n), lambda i, t: (i, t, 0)),
        out_shape=jax.ShapeDtypeStruct((b, lp, n), out_dtype),
        compiler_params=pltpu.CompilerParams(
            dimension_semantics=("parallel", "parallel"), vmem_limit_bytes=VMEM_LIMIT),
        name="rms_in_proj",
    )(h, gain, w)


def _rwkv_kernel(r_ref, k_ref, v_ref, lo_ref, prm_ref, mul_ref, w2_ref, a2_ref, g2_ref,
                 hsum_ref, tri_ref, lvl_ref, o_ref,
                 s_scr, prev_scr, prevl_scr, y_scr, q_scr, gh_scr, bg_scr, *, tl, group):
    t = pl.program_id(2)

    @pl.when(t == 0)
    def _():
        s_scr[...] = jnp.zeros_like(s_scr)
        prev_scr[...] = jnp.zeros_like(prev_scr)
        prevl_scr[...] = jnp.zeros_like(prevl_scr)
        gh_scr[...] = jnp.zeros_like(gh_scr)
        bg_scr[...] = jnp.zeros_like(bg_scr)

    prm = prm_ref[...]
    row = lambda i: prm[i:i + 1, :]
    hsum = hsum_ref[...]

    def tshift(x, prev, mu):
        prevx = _shift_rows(x, 1, [prev])
        return x + (prevx - x) * mu

    r_raw, k_raw, v_raw, lo_raw = (ref[...].astype(F32) for ref in (r_ref, k_ref, v_ref, lo_ref))
    r = tshift(r_raw, prev_scr[0:1, :], row(0))
    k = tshift(k_raw, prev_scr[1:2, :], row(1))
    v = tshift(v_raw, prev_scr[2:3, :], row(2))
    lo = tshift(lo_raw, prevl_scr[0:1, :], mul_ref[...])
    prev_scr[0:1, :] = r_raw[tl - 1:tl, :]
    prev_scr[1:2, :] = k_raw[tl - 1:tl, :]
    prev_scr[2:3, :] = v_raw[tl - 1:tl, :]
    prevl_scr[0:1, :] = lo_raw[tl - 1:tl, :]

    lo01 = lo[:, :LANES]
    w_pre = row(3) + _mm(jnp.tanh(lo01), w2_ref[...])
    logw = -jnp.exp(-_softplus(-w_pre) - 0.5)
    a = _sigmoid(row(4) + _mm(lo01, a2_ref[...]))
    g = _mm(_sigmoid(lo[:, LANES:]), g2_ref[...])
    kk = k * row(5)
    kk = kk / jnp.maximum(jnp.sqrt(_split2_mm(kk * kk, hsum)), 1e-12)
    kf = k * (1.0 + (a - 1.0) * row(6))
    ka = kk * a
    bonus = _split2_mm(r * kf * row(7), hsum) * v

    q_scr[0] = r
    q_scr[1] = logw
    q_scr[2] = kf
    q_scr[3] = v
    q_scr[4] = kk
    q_scr[5] = ka

    lane = lax.broadcasted_iota(jnp.int32, (1, LANES), 1)
    m0 = lane < HEAD_DIM
    tri = tri_ref[...]
    ri = lax.broadcasted_iota(jnp.int32, (LANES, LANES), 0)
    ci = lax.broadcasted_iota(jnp.int32, (LANES, LANES), 1)
    rt_, ct_ = ri & (CHUNK - 1), ci & (CHUNK - 1)
    strict = ct_ < rt_
    incl = ct_ <= rt_
    eye = ri == ci

    def stack(x):
        return jnp.concatenate([jnp.where(m0, x, 0.0), jnp.where(m0, 0.0, x)], axis=0).astype(BF16)

    def group_pre(bases, between):
        rng = range(len(bases))
        ld = [[q_scr[i, pl.ds(b, CHUNK), :] for i in range(6)] for b in bases]
        cum = [_cumsum_rows(tri, ld[j][1]) for j in rng]
        ops = []
        for j in rng:
            rc, lw, kc, vc, kkc, kac = ld[j]
            cum_c = cum[j][CHUNK - 1:CHUNK, :]
            w_in = jnp.exp(cum[j])
            w_inv = jnp.exp(-cum[j])
            w_ex = jnp.exp(cum[j] - lw)
            w_end = jnp.exp(cum_c - cum[j])
            ops.append(dict(
                at=stack(-kkc * w_ex), rt=stack(rc * w_in), bt=stack(kac * w_inv), kt=stack(kc * w_inv),
                bh=stack(kac * w_end), kh=stack(kc * w_end), v=stack(vc), wc=jnp.exp(cum_c)))
        am = [_mm_nt(jnp.concatenate([o["at"], o["rt"]], axis=0), jnp.concatenate([o["bt"], o["kt"]], axis=0))
              for o in ops]
        between()
        l_ab = [jnp.where(strict, m[:LANES, :LANES], 0.0) for m in am]
        a_ak = [jnp.where(strict, m[:LANES, LANES:], 0.0).astype(BF16) for m in am]
        m_rb = [jnp.where(incl, m[LANES:, :LANES], 0.0).astype(BF16) for m in am]
        m_rk = [jnp.where(incl, m[LANES:, LANES:], 0.0).astype(BF16) for m in am]
        lv0 = lvl_ref[0]
        tm = [jnp.where(eye, 1.0, 0.0) + l * lv0 for l in l_ab]
        av = [_mm(a_ak[j], ops[j]["v"]) for j in rng]
        between()
        for lv in range(1, 6):
            b = 1 << lv
            lvm = lvl_ref[lv]
            if b < 8:
                z = [_mm(l_ab[j] * lvm, tm[j]) for j in rng]
                between()
                tm = [tm[j] + _mm(tm[j], z[j]) for j in rng]
                between()
            else:
                odd = lambda x: jnp.concatenate([x[s:s + b] for s in range(b, LANES, 2 * b)], axis=0)
                lvm_o = odd(lvm)
                z_o = [_mm(odd(l_ab[j]) * lvm_o, tm[j]) for j in rng]
                between()
                zero = jnp.zeros((b, LANES), F32)

                def spread(y, base=None):
                    out = []
                    for i, s in enumerate(range(0, LANES, b)):
                        piece = y[(i // 2) * b:(i // 2 + 1) * b] if i % 2 else None
                        if base is None:
                            out.append(piece if i % 2 else zero)
                        else:
                            out.append(base[s:s + b] + piece if i % 2 else base[s:s + b])
                    return jnp.concatenate(out, axis=0)

                upd = [_mm(odd(tm[j]), spread(z_o[j])) for j in rng]
                tm = [spread(upd[j], tm[j]) for j in rng]
                between()
        pq = [_mm(tm[j], jnp.concatenate([ops[j]["at"], av[j].astype(BF16)], axis=1)) for j in rng]
        between()
        p_s = [m[:, :LANES].astype(BF16) for m in pq]
        qv = [jnp.concatenate([pq[j][:, LANES:].astype(BF16), ops[j]["v"]], axis=0) for j in rng]
        rp = [ops[j]["rt"].astype(F32) + _mm(m_rb[j], p_s[j]) for j in rng]
        y0 = [_mm(jnp.concatenate([m_rb[j], m_rk[j]], axis=1), qv[j]) for j in rng]
        gm = [jnp.where(eye, ops[j]["wc"], 0.0) + _mm_tn(ops[j]["bh"], p_s[j]) for j in rng]
        hm = [_mm_tn(jnp.concatenate([ops[j]["bh"], ops[j]["kh"]], axis=0), qv[j]) for j in rng]
        return rp, y0, gm, hm

    chain = {"j": 0, "s": s_scr[...]}

    def chain_step():
        j, s = chain["j"], chain["s"]
        if j == group:
            return
        ys = _mm(gh_scr[j, 0], s) + gh_scr[j, 1]
        y_scr[j * CHUNK:(j + 1) * CHUNK, :] = ys[:CHUNK] + ys[CHUNK:]
        chain["s"] = _mm(gh_scr[j, 2], s) + gh_scr[j, 3]
        chain["j"] = j + 1

    new_ops = group_pre([j * CHUNK for j in range(group)], chain_step)
    while chain["j"] < group:
        chain_step()
    s_scr[...] = chain["s"]

    y = y_scr[...]
    mean = _split2_mm(y, hsum) * (1.0 / HEAD_DIM)
    yc = y - mean
    var = _split2_mm(yc * yc, hsum) * (1.0 / HEAD_DIM)
    yn = yc * lax.rsqrt(var + GN_EPS) * row(8) + row(9)
    o_ref[...] = ((yn + bg_scr[0]) * bg_scr[1]).astype(o_ref.dtype)

    for i, vals in enumerate(new_ops):
        for j in range(group):
            gh_scr[j, i] = vals[j]
    bg_scr[0] = bonus
    bg_scr[1] = g


def _rwkv(p, prm, mu_l, w2p, a2p, g2, hsum, tri, lvl, tl, group):
    b, lp, _ = p.shape
    nt = lp // tl
    cur = lambda t: jnp.minimum(t, nt - 1)
    blk = lambda off: pl.BlockSpec((None, tl, LANES), lambda i, h, t: (i, cur(t), off + h))
    const2 = lambda shape: pl.BlockSpec(shape, lambda i, h, t: (0, 0))
    return pl.pallas_call(
        functools.partial(_rwkv_kernel, tl=tl, group=group),
        grid=(b, PAIRS_RWKV, nt + 1),
        in_specs=[blk(0), blk(PAIRS_RWKV), blk(2 * PAIRS_RWKV),
                  pl.BlockSpec((None, tl, LORA_COLS), lambda i, h, t: (i, cur(t), 3 * RWKV_DIM // LORA_COLS)),
                  pl.BlockSpec((None, 16, LANES), lambda i, h, t: (h, 0, 0)),
                  const2((1, LORA_COLS)),
                  pl.BlockSpec((LANES, LANES), lambda i, h, t: (0, h)),
                  pl.BlockSpec((LANES, LANES), lambda i, h, t: (0, h)),
                  pl.BlockSpec((LANES, LANES), lambda i, h, t: (0, h)),
                  const2((LANES, LANES)),
                  const2((CHUNK, CHUNK)),
                  pl.BlockSpec((6, LANES, LANES), lambda i, h, t: (0, 0, 0))],
        out_specs=pl.BlockSpec((None, tl, LANES), lambda i, h, t: (i, jnp.maximum(t - 1, 0), h)),
        out_shape=jax.ShapeDtypeStruct((b, lp, RWKV_DIM), BF16),
        scratch_shapes=[pltpu.VMEM((LANES, LANES), F32),
                        pltpu.VMEM((8, LANES), F32),
                        pltpu.VMEM((8, LORA_COLS), F32),
                        pltpu.VMEM((tl, LANES), F32),
                        pltpu.VMEM((6, tl, LANES), F32),
                        pltpu.VMEM((group, 4, LANES, LANES), F32),
                        pltpu.VMEM((2, tl, LANES), F32)],
        compiler_params=pltpu.CompilerParams(
            dimension_semantics=("parallel", "parallel", "arbitrary"), vmem_limit_bytes=VMEM_LIMIT),
        name="rwkv7_mix",
    )(p, p, p, p, prm, mu_l, w2p, a2p, g2, hsum, tri, lvl)


def _s5_kernel(u_ref, bm_ref, cm_ref, lam_ref, tab_ref, d_ref, gw_ref, gb_ref, o_ref,
               h0_scr, hs_scr, *, tl):
    t = pl.program_id(1)

    @pl.when(t == 0)
    def _():
        h0_scr[...] = jnp.zeros_like(h0_scr)

    n_lv = int(math.log2(S5_ROWS))
    n_grp = S5_SUB // S5_ROWS
    row_i = lax.broadcasted_iota(jnp.int32, (n_grp, S5_ROWS, LANES), 1)

    def sub_body(si, carry):
        r0 = pl.multiple_of(si * S5_SUB, S5_SUB)
        u = u_ref[pl.ds(r0, S5_SUB), :].astype(F32)
        bu = _mm(u, bm_ref[...])
        for lb in range(S5_LANES // LANES):
            ls = slice(lb * LANES, (lb + 1) * LANES)
            li = slice(S5_LANES + lb * LANES, S5_LANES + (lb + 1) * LANES)
            hr = bu[:, ls].reshape(n_grp, S5_ROWS, LANES)
            hi = bu[:, li].reshape(n_grp, S5_ROWS, LANES)
            for lv in range(n_lv):
                sh = 1 << lv
                lr = lam_ref[2 * lv:2 * lv + 1, ls]
                lm = lam_ref[2 * lv + 1:2 * lv + 2, ls]
                keep = row_i >= sh
                sr = jnp.where(keep, pltpu.roll(hr, sh, 1), 0.0)
                sm = jnp.where(keep, pltpu.roll(hi, sh, 1), 0.0)
                hr, hi = hr + (lr * sr - lm * sm), hi + (lr * sm + lm * sr)
            tr, tm = tab_ref[0, :, ls], tab_ref[1, :, ls]
            cr, ci = h0_scr[0:1, ls], h0_scr[1:2, ls]
            out_r, out_i = [], []
            for g in range(n_grp):
                gr = hr[g] + (tr * cr - tm * ci)
                gi = hi[g] + (tr * ci + tm * cr)
                cr, ci = gr[S5_ROWS - 1:S5_ROWS, :], gi[S5_ROWS - 1:S5_ROWS, :]
                out_r.append(gr)
                out_i.append(gi)
            h0_scr[0:1, ls] = cr
            h0_scr[1:2, ls] = ci
            hs_scr[:, ls] = jnp.concatenate(out_r, axis=0).astype(BF16)
            hs_scr[:, li] = jnp.concatenate(out_i, axis=0).astype(BF16)
        y = jnp.dot(hs_scr[...], cm_ref[...], preferred_element_type=F32) + d_ref[...] * u
        y = 0.5 * y * (1.0 + jnp.tanh(0.7978845608028654 * (y + 0.044715 * (y * y * y))))
        z = _mm(y, gw_ref[...]) + gb_ref[...]
        o_ref[pl.ds(r0, S5_SUB), :] = (y * _sigmoid(z)).astype(o_ref.dtype)
        return carry

    lax.fori_loop(0, tl // S5_SUB, sub_body, 0)


def _s5(p, bm, cm, lam_pows, tab, d_skip, glu_w, glu_b, tl):
    b, lp, _ = p.shape
    c2 = lambda a: pl.BlockSpec(a.shape, lambda i, t: (0,) * a.ndim)
    return pl.pallas_call(
        functools.partial(_s5_kernel, tl=tl),
        grid=(b, lp // tl),
        in_specs=[pl.BlockSpec((None, tl, S5_DIM), lambda i, t: (i, t, RWKV_COLS // S5_DIM)),
                  c2(bm), c2(cm), c2(lam_pows), c2(tab), c2(d_skip), c2(glu_w), c2(glu_b)],
        out_specs=pl.BlockSpec((None, tl, S5_DIM), lambda i, t: (i, t, 0)),
        out_shape=jax.ShapeDtypeStruct((b, lp, S5_DIM), BF16),
        scratch_shapes=[pltpu.VMEM((8, S5_LANES), F32),
                        pltpu.VMEM((S5_SUB, 2 * S5_LANES), BF16)],
        compiler_params=pltpu.CompilerParams(
            dimension_semantics=("parallel", "arbitrary"), vmem_limit_bytes=VMEM_LIMIT),
        name="s5_mix",
    )(p, bm, cm, lam_pows, tab, d_skip, glu_w, glu_b)


def _mix_ffn_kernel(*refs, tl, n_in):
    h_ref = refs[0]
    xs = refs[1:1 + n_in]
    ws = refs[1 + n_in:1 + 2 * n_in]
    g_ref, wg_ref, wv_ref, cw_ref, cb_ref, wd_ref, o_ref, xn_scr, prev_scr = refs[1 + 2 * n_in:]
    t = pl.program_id(1)
    n_ch = D_FF // FF_CHUNK

    @pl.when(t == 0)
    def _():
        prev_scr[...] = jnp.zeros_like(prev_scr)

    x = h_ref[...]
    for x_ref, w_ref in zip(xs, ws):
        x = x + jnp.dot(x_ref[...], w_ref[...], preferred_element_type=F32)
    xn_scr[...] = _rms_rows(x, g_ref[...]).astype(BF16)
    o_ref[...] = x

    def conv(hh, prev8, cw, cb):
        p6, p7 = prev8[6:7, :], prev8[7:8, :]
        s1 = _shift_rows(hh, 1, [p7])
        s2 = _shift_rows(hh, 2, [p6, p7])
        return cw[0:1, :] * s2 + cw[1:2, :] * s1 + cw[2:3, :] * hh + cb

    def up(c):
        cs = slice(c * FF_CHUNK, (c + 1) * FF_CHUNK)
        xn = xn_scr[...]
        return (jnp.dot(xn, wg_ref[:, cs], preferred_element_type=F32),
                jnp.dot(xn, wv_ref[:, cs], preferred_element_type=F32))

    pending = [up(c) for c in range(FF_DEPTH)]
    for c in range(n_ch):
        hg, hv = pending.pop(0)
        if c + FF_DEPTH < n_ch:
            pending.append(up(c + FF_DEPTH))
        cg = conv(hg, prev_scr[0, c], cw_ref[0, c], cb_ref[0, c])
        cv = conv(hv, prev_scr[1, c], cw_ref[1, c], cb_ref[1, c])
        prev_scr[0, c] = hg[tl - 8:tl, :]
        prev_scr[1, c] = hv[tl - 8:tl, :]
        act = (cg * _sigmoid(cg) * cv).astype(BF16)
        o_ref[...] += jnp.dot(act, wd_ref[c * FF_CHUNK:(c + 1) * FF_CHUNK, :], preferred_element_type=F32)


def _mix_ffn(h, xs, ws, gain, wg, wv, cw, cb, wd, tl):
    b, lp, d = h.shape
    n_ch = D_FF // FF_CHUNK
    full = lambda a: pl.BlockSpec(a.shape, lambda i, t: (0,) * a.ndim, pipeline_mode=pl.Buffered(1))
    row = lambda n: pl.BlockSpec((None, tl, n), lambda i, t: (i, t, 0))
    consts = (*ws, gain, wg, wv, cw, cb, wd)
    return pl.pallas_call(
        functools.partial(_mix_ffn_kernel, tl=tl, n_in=len(xs)),
        grid=(b, lp // tl),
        in_specs=[row(d)] + [row(x.shape[-1]) for x in xs] + [full(a) for a in consts],
        out_specs=row(d),
        out_shape=jax.ShapeDtypeStruct((b, lp, d), F32),
        scratch_shapes=[pltpu.VMEM((tl, d), BF16),
                        pltpu.VMEM((2, n_ch, 8, FF_CHUNK), F32)],
        compiler_params=pltpu.CompilerParams(
            dimension_semantics=("parallel", "arbitrary"), vmem_limit_bytes=VMEM_LIMIT),
        name="mix_out_mlp",
    )(h, *xs, *consts)


def _fox_in_kernel(x_ref, g_ref, w_ref, qg_ref, kg_ref, bf_ref, avg_ref, tri_ref,
                   pq_ref, pk_ref, oq_ref, ok_ref, q_ref, k_ref, v_ref, cb_ref, carry_scr, *, tl):
    t = pl.program_id(1)

    @pl.when(t == 0)
    def _():
        carry_scr[...] = jnp.zeros_like(carry_scr)

    xn = _rms_rows(x_ref[...], g_ref[...]).astype(BF16)
    v_ref[...] = jnp.dot(xn, w_ref[:, 2 * D_MODEL:3 * D_MODEL], preferred_element_type=F32).astype(BF16)
    f = jnp.dot(xn, w_ref[:, 3 * D_MODEL:], preferred_element_type=F32) + bf_ref[...]
    log_f = jnp.minimum(f, 0.0) - jnp.log(1.0 + jnp.exp(-jnp.abs(f)))
    cum = _cumsum_rows(tri_ref[...], log_f) + carry_scr[0:1, :]
    carry_scr[0:1, :] = cum[tl - 1:tl, :]
    lane = lax.broadcasted_iota(jnp.int32, (1, LANES), 1)
    c = jnp.where(lane < FOX_HEADS, cum * LOG2E, 0.0)
    n_sub = tl // min(tl, ATTN_ROWS)
    sub_last = [c[(s + 1) * (tl // n_sub) - 1:(s + 1) * (tl // n_sub), :] for s in range(n_sub)]
    cb_ref[...] = jnp.concatenate([c[0:1, :]] + sub_last + [jnp.zeros((7 - n_sub, LANES), F32)], axis=0)
    c1 = c.astype(BF16).astype(F32)
    c2 = (c - c1).astype(BF16).astype(F32)
    c3 = (c - c1 - c2).astype(BF16).astype(F32)
    c123 = (c1 + pltpu.roll(c2, FOX_HEADS, 1) + pltpu.roll(c3, 2 * FOX_HEADS, 1)).astype(BF16)
    feat_q = jnp.dot(c123, pq_ref[...], preferred_element_type=F32) + oq_ref[...]
    feat_k = jnp.dot(c123, pk_ref[...], preferred_element_type=F32) + ok_ref[...]

    avg = avg_ref[...]
    m0 = lane < HEAD_DIM
    for off, gain_ref, feat, dst in ((0, qg_ref, feat_q, q_ref), (D_MODEL, kg_ref, feat_k, k_ref)):
        z_all = jnp.dot(xn, w_ref[:, off:off + D_MODEL], preferred_element_type=F32)
        for s in range(PAIRS_FOX):
            z = z_all[:, s * LANES:(s + 1) * LANES]
            ms = _mm(z * z, avg)
            zn = z * lax.rsqrt(ms + RMS_EPS) * gain_ref[...]
            for hh in range(2):
                hs = slice((2 * s + hh) * LANES, (2 * s + hh + 1) * LANES)
                own = m0 if hh == 0 else jnp.logical_not(m0)
                dst[:, hs] = jnp.where(own, zn, feat[:, hs]).astype(BF16)


def _fox_in(h, gain, w, qg, kg, bf, avg, tri, pq, pk, oq, ok, tl):
    b, lp, d = h.shape
    c2 = lambda a: pl.BlockSpec(a.shape, lambda i, t: (0,) * a.ndim)
    act = lambda n: (pl.BlockSpec((None, tl, n), lambda i, t: (i, t, 0)),
                     jax.ShapeDtypeStruct((b, lp, n), BF16))
    bounds = (pl.BlockSpec((None, None, 8, LANES), lambda i, t: (i, t, 0, 0)),
              jax.ShapeDtypeStruct((b, lp // tl, 8, LANES), F32))
    specs, shapes = zip(act(2 * d), act(2 * d), act(d), bounds)
    consts = (gain, w, qg, kg, bf, avg, tri, pq, pk, oq, ok)
    return pl.pallas_call(
        functools.partial(_fox_in_kernel, tl=tl),
        grid=(b, lp // tl),
        in_specs=[pl.BlockSpec((None, tl, d), lambda i, t: (i, t, 0))] + [c2(a) for a in consts],
        out_specs=list(specs),
        out_shape=list(shapes),
        scratch_shapes=[pltpu.VMEM((8, LANES), F32)],
        compiler_params=pltpu.CompilerParams(
            dimension_semantics=("parallel", "arbitrary"), vmem_limit_bytes=VMEM_LIMIT),
        name="fox_in_proj",
    )(h, *consts)


def _fox_feature_consts():
    pq = np.zeros((LANES, FOX_HEADS * LANES), np.float32)
    pk = np.zeros_like(pq)
    oq = np.zeros((1, FOX_HEADS * LANES), np.float32)
    ok = np.zeros_like(oq)
    for h in range(FOX_HEADS):
        base = h * LANES + HEAD_DIM * (1 - h % 2)
        for i in range(3):
            pq[i * FOX_HEADS + h, base + i] = 1.0
            pk[i * FOX_HEADS + h, base + 3 + i] = -1.0
            oq[0, base + 3 + i] = 1.0
            ok[0, base + i] = 1.0
    return jnp.asarray(pq, BF16), jnp.asarray(pk, BF16), jnp.asarray(oq), jnp.asarray(ok)


def _fox_attn_kernel(first_ref, q_ref, k_ref, v_ref, o_ref, m_scr, acc_scr, *, ta):
    qi = pl.program_id(2)
    m_scr[...] = jnp.full_like(m_scr, -1e30)
    acc_scr[...] = jnp.zeros_like(acc_scr)

    lane = lax.broadcasted_iota(jnp.int32, (1, LANES), 1)
    m0 = lane < HEAD_DIM
    ones_lane = (lane == HEAD_DIM, lane == 0)

    rt = min(ta, ATTN_ROWS)

    def step(k0s, masked, nk=ta):
        units = [(kb, r, h) for kb in range(len(k0s)) for r in range(ta // rt) for h in range(2)]

        def n_keys(r):
            return (r + 1) * rt if masked else nk

        va = []
        for k0 in k0s:
            v = v_ref[pl.ds(k0, nk), :]
            va.append([jnp.where(m0 if h == 0 else jnp.logical_not(m0), v,
                                 jnp.where(ones_lane[h], 1.0, 0.0).astype(BF16)) for h in range(2)])

        def scores(u):
            kb, r, h = u
            hs = slice(h * LANES, (h + 1) * LANES)
            return lax.dot_general(q_ref[r * rt:(r + 1) * rt, hs], k_ref[pl.ds(k0s[kb], n_keys(r)), hs],
                                   (((1,), (1,)), ((), ())), preferred_element_type=F32)

        def finish(u, sc):
            kb, r, h = u
            rows = slice(r * rt, (r + 1) * rt)
            if masked:
                rr = lax.broadcasted_iota(jnp.int32, sc.shape, 0) + r * rt
                cc = lax.broadcasted_iota(jnp.int32, sc.shape, 1)
                sc = jnp.where(cc <= rr, sc, -1e30)
            m_prev = m_scr[h, rows]
            m_new = jnp.maximum(m_prev, jnp.max(sc, axis=1, keepdims=True))
            p = jnp.exp2(sc - m_new).astype(BF16)
            alpha = jnp.exp2(m_prev - m_new)
            m_scr[h, rows] = m_new
            acc_scr[h, rows] = alpha * acc_scr[h, rows] + jnp.dot(p, va[kb][h][:n_keys(r)],
                                                                  preferred_element_type=F32)

        pending = [scores(u) for u in units[:ATTN_DEPTH]]
        for i, u in enumerate(units):
            sc = pending.pop(0)
            if i + ATTN_DEPTH < len(units):
                pending.append(scores(units[i + ATTN_DEPTH]))
            finish(u, sc)

    step([pl.multiple_of(qi * ta, ta)], True)
    n_sub = ta // rt
    sub0 = first_ref[(pl.program_id(0) * PAIRS_FOX + pl.program_id(1)) * pl.num_programs(2) + qi]
    first = (sub0 + (n_sub - 1)) // n_sub
    part = first * n_sub - sub0
    count = qi - first

    for pn in range(1, n_sub):
        @pl.when(part == pn)
        def _():
            step([pl.multiple_of((first * n_sub - pn) * rt, rt)], False, nk=pn * rt)

    def body(kj, carry):
        k0 = pl.multiple_of((first + 2 * kj) * ta, ta)
        step([k0, k0 + ta], False)
        return carry

    lax.fori_loop(0, count // 2, body, 0)

    @pl.when(count % 2 == 1)
    def _():
        step([pl.multiple_of((qi - 1) * ta, ta)], False)

    a0, a1 = acc_scr[0], acc_scr[1]
    o0 = a0 / a0[:, HEAD_DIM:HEAD_DIM + 1]
    o1 = a1 / a1[:, 0:1]
    o_ref[...] = jnp.where(m0, o0, o1).astype(o_ref.dtype)


def _fox_first_sub(cb, qg, kg, n_sub):
    bsz, nq = cb.shape[:2]
    c_first = cb[:, :, 0, :FOX_HEADS]
    c_last = cb[:, :, 1:1 + n_sub, :FOX_HEADS].reshape(bsz, nq * n_sub, FOX_HEADS)
    bound = 64.0 * jnp.max(jnp.abs(qg)) * jnp.max(jnp.abs(kg))
    reach = 2.0 * bound + c_first[:, :, None, :] - c_last[:, None, :, :]
    needed = (reach >= -EXP2_ZERO).reshape(bsz, nq, nq * n_sub, PAIRS_FOX, 2).any(-1)
    earlier = jnp.arange(nq * n_sub)[None, :] < n_sub * jnp.arange(nq)[:, None]
    count = jnp.sum(needed & earlier[None, :, :, None], axis=2)
    sub0 = n_sub * jnp.arange(nq)[None, :, None] - count
    return jnp.transpose(sub0, (0, 2, 1)).reshape(-1).astype(jnp.int32)


def _fox_attn(first, q, k, v, ta):
    b, lp, d = v.shape
    grid_spec = pltpu.PrefetchScalarGridSpec(
        num_scalar_prefetch=1,
        grid=(b, PAIRS_FOX, lp // ta),
        in_specs=[pl.BlockSpec((None, ta, 2 * LANES), lambda i, h, t, f: (i, t, h)),
                  pl.BlockSpec((None, lp, 2 * LANES), lambda i, h, t, f: (i, 0, h)),
                  pl.BlockSpec((None, lp, LANES), lambda i, h, t, f: (i, 0, h))],
        out_specs=pl.BlockSpec((None, ta, LANES), lambda i, h, t, f: (i, t, h)),
        scratch_shapes=[pltpu.VMEM((2, ta, 1), F32),
                        pltpu.VMEM((2, ta, LANES), F32)],
    )
    return pl.pallas_call(
        functools.partial(_fox_attn_kernel, ta=ta),
        grid_spec=grid_spec,
        out_shape=jax.ShapeDtypeStruct((b, lp, d), BF16),
        compiler_params=pltpu.CompilerParams(
            dimension_semantics=("parallel", "parallel", "parallel"), vmem_limit_bytes=VMEM_LIMIT),
        name="fox_attention",
    )(first, q, k, v)


def _block_diag(blocks):
    g, r, c = blocks.shape
    eye = jnp.eye(g, dtype=blocks.dtype)
    return (eye[:, None, :, None] * blocks[:, :, None, :]).reshape(g * r, g * c)


def _cmul(ar, ai, br, bi):
    return ar * br - ai * bi, ar * bi + ai * br


def _s5_params(a_re, a_im, log_dt, b_re, b_im, c_re, c_im):
    dt = jnp.exp(log_dt)[:, None]
    mag = jnp.exp(a_re * dt)
    lam_re, lam_im = mag * jnp.cos(a_im * dt), mag * jnp.sin(a_im * dt)
    den = a_re * a_re + a_im * a_im
    z_re = ((lam_re - 1.0) * a_re + lam_im * a_im) / den
    z_im = (lam_im * a_re - (lam_re - 1.0) * a_im) / den
    bb_re = z_re[..., None] * b_re - z_im[..., None] * b_im
    bb_im = z_re[..., None] * b_im + z_im[..., None] * b_re
    to_in = lambda m: _block_diag(jnp.swapaxes(m, 1, 2))
    bm = jnp.concatenate([to_in(bb_re), to_in(bb_im)], axis=1).astype(BF16)
    to_out = lambda m: _block_diag(jnp.swapaxes(m, 1, 2))
    cm = jnp.concatenate([to_out(c_re), -to_out(c_im)], axis=0).astype(BF16)
    lr, li = lam_re.reshape(1, S5_LANES), lam_im.reshape(1, S5_LANES)
    n_lv = int(math.log2(S5_ROWS))
    pows, tr, ti = [], lr, li
    pr, pi = lr, li
    for _ in range(n_lv):
        pows += [pr, pi]
        nr, ni = _cmul(tr, ti, pr, pi)
        tr, ti = jnp.concatenate([tr, nr], axis=0), jnp.concatenate([ti, ni], axis=0)
        pr, pi = _cmul(pr, pi, pr, pi)
    pows = jnp.concatenate(pows + [jnp.zeros((16 - 2 * n_lv, S5_LANES), F32)], axis=0)
    return bm, cm, pows, jnp.stack([tr, ti])


def _rwkv_consts():
    idx = np.arange(LANES)
    hsum = (idx[:, None] // HEAD_DIM == idx[None, :] // HEAD_DIM).astype(np.float32)
    t = idx % CHUNK
    lv = [(t[:, None] // 2 == t[None, :] // 2)]
    b = 2
    while b < CHUNK:
        lv.append((t[:, None] // (2 * b) == t[None, :] // (2 * b)) & (t[:, None] // b != t[None, :] // b))
        b *= 2
    tri = (np.arange(CHUNK)[None, :] <= np.arange(CHUNK)[:, None]).astype(np.float32)
    return (jnp.asarray(hsum, BF16), jnp.asarray(tri, BF16), jnp.asarray(np.stack(lv).astype(np.float32)))


def _time_tile(length):
    return 768 if length > 2048 else 128


def kernel(x, meta_tokens, mix_norm, ffn_norm, even_w_in, even_w_out, rwkv_mu, rwkv_w0, rwkv_w2, rwkv_a0, rwkv_a2, rwkv_g2, rwkv_k_k, rwkv_k_a, rwkv_r_k, rwkv_ln_w, rwkv_ln_b, s5_a_re, s5_a_im, s5_log_dt, s5_b_re, s5_b_im, s5_c_re, s5_c_im, s5_d, s5_glu_w, s5_glu_b, odd_w_in, odd_w_out, fox_b_f, fox_q_gain, fox_k_gain, ffn_w_up, ffn_conv_w, ffn_conv_b, ffn_w_down):
    bsz, seq, d = x.shape
    length = N_META + seq
    tl = _time_tile(length)
    lp = -(-length // tl) * tl
    meta = jnp.broadcast_to(meta_tokens[None].astype(x.dtype), (bsz, N_META, d))
    h = jnp.concatenate([meta, x, jnp.zeros((bsz, lp - length, d), x.dtype)], axis=1)

    hsum, tri64, lvl = _rwkv_consts()
    n_ch = D_FF // FF_CHUNK

    def mix_ffn(h, xs, ws, layer):
        w_up = ffn_w_up[layer].astype(BF16)
        cw = ffn_conv_w[layer].reshape(3, 2, n_ch, FF_CHUNK).transpose(1, 2, 0, 3)
        cw = jnp.pad(cw, ((0, 0), (0, 0), (0, 5), (0, 0)))
        cb = ffn_conv_b[layer].reshape(2, n_ch, 1, FF_CHUNK)
        return _mix_ffn(h, xs, ws, ffn_norm[layer][None], w_up[:, :D_FF], w_up[:, D_FF:], cw, cb,
                        ffn_w_down[layer].astype(BF16), tl)

    p = _rms_matmul(h, mix_norm[0][None], even_w_in[0].astype(BF16), tl, BF16)
    pairs = lambda vec: vec.reshape(PAIRS_RWKV, 1, LANES)
    mu = rwkv_mu[0]
    prm = jnp.concatenate(
        [pairs(mu[:RWKV_DIM]), pairs(mu[RWKV_DIM:2 * RWKV_DIM]), pairs(mu[2 * RWKV_DIM:3 * RWKV_DIM]),
         pairs(rwkv_w0[0]), pairs(rwkv_a0[0]), pairs(rwkv_k_k[0]), pairs(rwkv_k_a[0]),
         pairs(rwkv_r_k[0].reshape(-1)), pairs(rwkv_ln_w[0]), pairs(rwkv_ln_b[0]),
         jnp.zeros((PAIRS_RWKV, 6, LANES), F32)], axis=1)
    zeros64 = jnp.zeros((64, RWKV_DIM), F32)
    w2p = jnp.concatenate([rwkv_w2[0], zeros64], axis=0).astype(BF16)
    a2p = jnp.concatenate([zeros64, rwkv_a2[0]], axis=0).astype(BF16)
    a_out = _rwkv(p, prm, mu[3 * RWKV_DIM:][None], w2p, a2p, rwkv_g2[0].astype(BF16),
                  hsum, tri64, lvl, tl, tl // CHUNK)
    bm, cm, pows, tab = _s5_params(s5_a_re[0], s5_a_im[0], s5_log_dt[0], s5_b_re[0], s5_b_im[0],
                                   s5_c_re[0], s5_c_im[0])
    b_out = _s5(p, bm, cm, pows, tab, s5_d[0][None], s5_glu_w[0].astype(BF16), s5_glu_b[0][None], tl)
    w_out = even_w_out[0].astype(BF16)
    h = mix_ffn(h, [a_out, b_out], [w_out[:RWKV_DIM], w_out[RWKV_DIM:]], 0)

    w_in = jnp.pad(odd_w_in[0], ((0, 0), (0, ODD_IN_PAD - odd_w_in.shape[-1]))).astype(BF16)
    scale = HEAD_DIM ** -0.5
    qg = jnp.tile(fox_q_gain[0], 2)[None] * (scale * LOG2E)
    kg = jnp.tile(fox_k_gain[0], 2)[None]
    bf = jnp.pad(fox_b_f[0], (0, LANES - FOX_HEADS))[None]
    tri_t = jnp.asarray(np.tril(np.ones((tl, tl), np.float32)), BF16)
    q, k, v, cb = _fox_in(h, mix_norm[1][None], w_in, qg, kg, bf, hsum * (1.0 / HEAD_DIM), tri_t,
                      *_fox_feature_consts(), tl)
    o = _fox_attn(_fox_first_sub(cb, qg, kg, tl // min(tl, ATTN_ROWS)), q, k, v, tl)
    h = mix_ffn(h, [o], [odd_w_out[0].astype(BF16)], 1)
    return h[:, N_META:N_META + seq]
```

```python
import functools
import math

import jax
import jax.numpy as jnp
import numpy as np
from jax import lax
from jax.experimental import pallas as pl
from jax.experimental.pallas import tpu as pltpu

F32 = jnp.float32
BF16 = jnp.bfloat16

D_MODEL = 1024
N_META = 16
RMS_EPS = 1e-6
GN_EPS = 64e-5

RWKV_HEADS = 12
HEAD_DIM = 64
RWKV_DIM = RWKV_HEADS * HEAD_DIM
LORA_COLS = 256
RWKV_COLS = 3 * RWKV_DIM + LORA_COLS
S5_DIM = 256
S5_GROUP = 16
S5_GROUPS = 16
S5_STATE = 64
S5_LANES = S5_GROUPS * S5_STATE
EVEN_IN = RWKV_COLS + S5_DIM
FOX_HEADS = 16
ODD_IN_PAD = 3 * D_MODEL + 128
D_FF = 2816
LANES = 128
PAIRS_RWKV = RWKV_DIM // LANES
PAIRS_FOX = D_MODEL // LANES
CHUNK = 64
S5_SUB = 128
S5_ROWS = 8
FF_CHUNK = 256
FF_DEPTH = 2
VMEM_LIMIT = 56 * 1024 * 1024
LOG2E = 1.4426950408889634
ATTN_ROWS = 256
ATTN_DEPTH = 4
EXP2_ZERO = 160.0


def _mm(a, b):
    return jnp.dot(a.astype(BF16), b.astype(BF16), preferred_element_type=F32)


def _mm_nt(a, b):
    return lax.dot_general(a.astype(BF16), b.astype(BF16), (((1,), (1,)), ((), ())),
                           preferred_element_type=F32)


def _mm_tn(a, b):
    return lax.dot_general(a.astype(BF16), b.astype(BF16), (((0,), (0,)), ((), ())),
                           preferred_element_type=F32)


def _split2_mm(x, m):
    hi = x.astype(BF16)
    lo = (x - hi.astype(F32)).astype(BF16)
    return (jnp.dot(hi, m, preferred_element_type=F32)
            + jnp.dot(lo, m, preferred_element_type=F32))


def _cumsum_rows(tri, x):
    h1 = x.astype(BF16)
    r1 = x - h1.astype(F32)
    h2 = r1.astype(BF16)
    h3 = (r1 - h2.astype(F32)).astype(BF16)
    w = x.shape[1]
    res = jnp.dot(tri, jnp.concatenate([h1, h2, h3], axis=1), preferred_element_type=F32)
    return res[:, :w] + res[:, w:2 * w] + res[:, 2 * w:]


def _softplus(z):
    return jnp.maximum(z, 0.0) + jnp.log(1.0 + jnp.exp(-jnp.abs(z)))


def _sigmoid(z):
    return 1.0 / (1.0 + jnp.exp(-z))


def _rms_rows(x, gain):
    ms = jnp.mean(x * x, axis=-1, keepdims=True)
    return x * lax.rsqrt(ms + RMS_EPS) * gain


def _shift_rows(x, k, prev_rows):
    rolled = pltpu.roll(x, k, 0)
    row = lax.broadcasted_iota(jnp.int32, x.shape, 0)
    for i in range(k):
        rolled = jnp.where(row == i, prev_rows[i], rolled)
    return rolled


def _rms_matmul_kernel(x_ref, g_ref, w_ref, o_ref, *, n_chunk):
    xn = _rms_rows(x_ref[...], g_ref[...]).astype(BF16)
    n = o_ref.shape[-1]
    for j in range(0, n, n_chunk):
        o_ref[:, j:j + n_chunk] = jnp.dot(xn, w_ref[:, j:j + n_chunk],
                                          preferred_element_type=F32).astype(o_ref.dtype)


def _rms_matmul(h, gain, w, tl, out_dtype):
    b, lp, d = h.shape
    n = w.shape[1]
    return pl.pallas_call(
        functools.partial(_rms_matmul_kernel, n_chunk=256),
        grid=(b, lp // tl),
        in_specs=[pl.BlockSpec((None, tl, d), lambda i, t: (i, t, 0)),
                  pl.BlockSpec((1, d), lambda i, t: (0, 0)),
                  pl.BlockSpec((d, n), lambda i, t: (0, 0))],
        out_specs=pl.BlockSpec((None, tl, n), lambda i, t: (i, t, 0)),
        out_shape=jax.ShapeDtypeStruct((b, lp, n), out_dtype),
        compiler_params=pltpu.CompilerParams(
            dimension_semantics=("parallel", "parallel"), vmem_limit_bytes=VMEM_LIMIT),
        name="rms_in_proj",
    )(h, gain, w)


def _rwkv_kernel(r_ref, k_ref, v_ref, lo_ref, prm_ref, mul_ref, w2_ref, a2_ref, g2_ref,
                 hsum_ref, tri_ref, lvl_ref, o_ref,
                 s_scr, prev_scr, prevl_scr, y_scr, q_scr, gh_scr, bg_scr, *, tl, group):
    t = pl.program_id(2)

    @pl.when(t == 0)
    def _():
        s_scr[...] = jnp.zeros_like(s_scr)
        prev_scr[...] = jnp.zeros_like(prev_scr)
        prevl_scr[...] = jnp.zeros_like(prevl_scr)
        gh_scr[...] = jnp.zeros_like(gh_scr)
        bg_scr[...] = jnp.zeros_like(bg_scr)

    prm = prm_ref[...]
    row = lambda i: prm[i:i + 1, :]
    hsum = hsum_ref[...]

    def tshift(x, prev, mu):
        prevx = _shift_rows(x, 1, [prev])
        return x + (prevx - x) * mu

    r_raw, k_raw, v_raw, lo_raw = (ref[...].astype(F32) for ref in (r_ref, k_ref, v_ref, lo_ref))
    r = tshift(r_raw, prev_scr[0:1, :], row(0))
    k = tshift(k_raw, prev_scr[1:2, :], row(1))
    v = tshift(v_raw, prev_scr[2:3, :], row(2))
    lo = tshift(lo_raw, prevl_scr[0:1, :], mul_ref[...])
    prev_scr[0:1, :] = r_raw[tl - 1:tl, :]
    prev_scr[1:2, :] = k_raw[tl - 1:tl, :]
    prev_scr[2:3, :] = v_raw[tl - 1:tl, :]
    prevl_scr[0:1, :] = lo_raw[tl - 1:tl, :]

    lo01 = lo[:, :LANES]
    w_pre = row(3) + _mm(jnp.tanh(lo01), w2_ref[...])
    logw = -jnp.exp(-_softplus(-w_pre) - 0.5)
    a = _sigmoid(row(4) + _mm(lo01, a2_ref[...]))
    g = _mm(_sigmoid(lo[:, LANES:]), g2_ref[...])
    kk = k * row(5)
    kk = kk / jnp.maximum(jnp.sqrt(_mm(kk * kk, hsum)), 1e-12)
    kf = k * (1.0 + (a - 1.0) * row(6))
    ka = kk * a
    bonus = _split2_mm(r * kf * row(7), hsum) * v

    q_scr[0] = r
    q_scr[1] = logw
    q_scr[2] = kf
    q_scr[3] = v
    q_scr[4] = kk
    q_scr[5] = ka

    lane = lax.broadcasted_iota(jnp.int32, (1, LANES), 1)
    m0 = lane < HEAD_DIM
    tri = tri_ref[...]
    ri = lax.broadcasted_iota(jnp.int32, (LANES, LANES), 0)
    ci = lax.broadcasted_iota(jnp.int32, (LANES, LANES), 1)
    rt_, ct_ = ri & (CHUNK - 1), ci & (CHUNK - 1)
    strict = ct_ < rt_
    incl = ct_ <= rt_
    eye = ri == ci

    def stack(x):
        return jnp.concatenate([jnp.where(m0, x, 0.0), jnp.where(m0, 0.0, x)], axis=0).astype(BF16)

    def group_pre(bases, between):
        rng = range(len(bases))
        ld = [[q_scr[i, pl.ds(b, CHUNK), :] for i in range(6)] for b in bases]
        cum = [_cumsum_rows(tri, ld[j][1]) for j in rng]
        ops = []
        for j in rng:
            rc, lw, kc, vc, kkc, kac = ld[j]
            cum_c = cum[j][CHUNK - 1:CHUNK, :]
            w_in = jnp.exp(cum[j])
            w_inv = jnp.exp(-cum[j])
            w_ex = jnp.exp(cum[j] - lw)
            w_end = jnp.exp(cum_c - cum[j])
            ops.append(dict(
                at=stack(-kkc * w_ex), rt=stack(rc * w_in), bt=stack(kac * w_inv), kt=stack(kc * w_inv),
                bh=stack(kac * w_end), kh=stack(kc * w_end), v=stack(vc), wc=jnp.exp(cum_c)))
        am = [_mm_nt(jnp.concatenate([o["at"], o["rt"]], axis=0), jnp.concatenate([o["bt"], o["kt"]], axis=0))
              for o in ops]
        between()
        l_ab = [jnp.where(strict, m[:LANES, :LANES], 0.0) for m in am]
        a_ak = [jnp.where(strict, m[:LANES, LANES:], 0.0).astype(BF16) for m in am]
        m_rb = [jnp.where(incl, m[LANES:, :LANES], 0.0).astype(BF16) for m in am]
        m_rk = [jnp.where(incl, m[LANES:, LANES:], 0.0).astype(BF16) for m in am]
        lv0 = lvl_ref[0]
        tm = [jnp.where(eye, 1.0, 0.0) + l * lv0 for l in l_ab]
        av = [_mm(a_ak[j], ops[j]["v"]) for j in rng]
        between()
        for lv in range(1, 6):
            b = 1 << lv
            lvm = lvl_ref[lv]
            if b < 8:
                z = [_mm(l_ab[j] * lvm, tm[j]) for j in rng]
                between()
                tm = [tm[j] + _mm(tm[j], z[j]) for j in rng]
                between()
            else:
                odd = lambda x: jnp.concatenate([x[s:s + b] for s in range(b, LANES, 2 * b)], axis=0)
                lvm_o = odd(lvm)
                z_o = [_mm(odd(l_ab[j]) * lvm_o, tm[j]) for j in rng]
                between()
                zero = jnp.zeros((b, LANES), F32)

                def spread(y, base=None):
                    out = []
                    for i, s in enumerate(range(0, LANES, b)):
                        piece = y[(i // 2) * b:(i // 2 + 1) * b] if i % 2 else None
                        if base is None:
                            out.append(piece if i % 2 else zero)
                        else:
                            out.append(base[s:s + b] + piece if i % 2 else base[s:s + b])
                    return jnp.concatenate(out, axis=0)

                upd = [_mm(odd(tm[j]), spread(z_o[j])) for j in rng]
                tm = [spread(upd[j], tm[j]) for j in rng]
                between()
        pq = [_mm(tm[j], jnp.concatenate([ops[j]["at"], av[j].astype(BF16)], axis=1)) for j in rng]
        between()
        p_s = [m[:, :LANES].astype(BF16) for m in pq]
        qv = [jnp.concatenate([pq[j][:, LANES:].astype(BF16), ops[j]["v"]], axis=0) for j in rng]
        rp = [ops[j]["rt"].astype(F32) + _mm(m_rb[j], p_s[j]) for j in rng]
        y0 = [_mm(jnp.concatenate([m_rb[j], m_rk[j]], axis=1), qv[j]) for j in rng]
        gm = [jnp.where(eye, ops[j]["wc"], 0.0) + _mm_tn(ops[j]["bh"], p_s[j]) for j in rng]
        hm = [_mm_tn(jnp.concatenate([ops[j]["bh"], ops[j]["kh"]], axis=0), qv[j]) for j in rng]
        return rp, y0, gm, hm

    chain = {"j": 0, "s": s_scr[...]}

    def chain_step():
        j, s = chain["j"], chain["s"]
        if j == group:
            return
        ys = _mm(gh_scr[j, 0], s) + gh_scr[j, 1]
        y_scr[j * CHUNK:(j + 1) * CHUNK, :] = ys[:CHUNK] + ys[CHUNK:]
        chain["s"] = _mm(gh_scr[j, 2], s) + gh_scr[j, 3]
        chain["j"] = j + 1

    new_ops = group_pre([j * CHUNK for j in range(group)], chain_step)
    while chain["j"] < group:
        chain_step()
    s_scr[...] = chain["s"]

    y = y_scr[...]
    mean = _split2_mm(y, hsum) * (1.0 / HEAD_DIM)
    yc = y - mean
    var = _mm(yc * yc, hsum) * (1.0 / HEAD_DIM)
    yn = yc * lax.rsqrt(var + GN_EPS) * row(8) + row(9)
    o_ref[...] = ((yn + bg_scr[0]) * bg_scr[1]).astype(o_ref.dtype)

    for i, vals in enumerate(new_ops):
        for j in range(group):
            gh_scr[j, i] = vals[j]
    bg_scr[0] = bonus
    bg_scr[1] = g


def _rwkv(p, prm, mu_l, w2p, a2p, g2, hsum, tri, lvl, tl, group):
    b, lp, _ = p.shape
    nt = lp // tl
    cur = lambda t: jnp.minimum(t, nt - 1)
    blk = lambda off: pl.BlockSpec((None, tl, LANES), lambda i, h, t: (i, cur(t), off + h))
    const2 = lambda shape: pl.BlockSpec(shape, lambda i, h, t: (0, 0))
    return pl.pallas_call(
        functools.partial(_rwkv_kernel, tl=tl, group=group),
        grid=(b, PAIRS_RWKV, nt + 1),
        in_specs=[blk(0), blk(PAIRS_RWKV), blk(2 * PAIRS_RWKV),
                  pl.BlockSpec((None, tl, LORA_COLS), lambda i, h, t: (i, cur(t), 3 * RWKV_DIM // LORA_COLS)),
                  pl.BlockSpec((None, 16, LANES), lambda i, h, t: (h, 0, 0)),
                  const2((1, LORA_COLS)),
                  pl.BlockSpec((LANES, LANES), lambda i, h, t: (0, h)),
                  pl.BlockSpec((LANES, LANES), lambda i, h, t: (0, h)),
                  pl.BlockSpec((LANES, LANES), lambda i, h, t: (0, h)),
                  const2((LANES, LANES)),
                  const2((CHUNK, CHUNK)),
                  pl.BlockSpec((6, LANES, LANES), lambda i, h, t: (0, 0, 0))],
        out_specs=pl.BlockSpec((None, tl, LANES), lambda i, h, t: (i, jnp.maximum(t - 1, 0), h)),
        out_shape=jax.ShapeDtypeStruct((b, lp, RWKV_DIM), BF16),
        scratch_shapes=[pltpu.VMEM((LANES, LANES), F32),
                        pltpu.VMEM((8, LANES), F32),
                        pltpu.VMEM((8, LORA_COLS), F32),
                        pltpu.VMEM((tl, LANES), F32),
                        pltpu.VMEM((6, tl, LANES), F32),
                        pltpu.VMEM((group, 4, LANES, LANES), F32),
                        pltpu.VMEM((2, tl, LANES), F32)],
        compiler_params=pltpu.CompilerParams(
            dimension_semantics=("parallel", "parallel", "arbitrary"), vmem_limit_bytes=VMEM_LIMIT),
        name="rwkv7_mix",
    )(p, p, p, p, prm, mu_l, w2p, a2p, g2, hsum, tri, lvl)


def _s5_kernel(u_ref, bm_ref, cm_ref, lam_ref, tab_ref, d_ref, gw_ref, gb_ref, o_ref,
               h0_scr, hs_scr, *, tl):
    t = pl.program_id(1)

    @pl.when(t == 0)
    def _():
        h0_scr[...] = jnp.zeros_like(h0_scr)

    n_lv = int(math.log2(S5_ROWS))
    n_grp = S5_SUB // S5_ROWS
    row_i = lax.broadcasted_iota(jnp.int32, (n_grp, S5_ROWS, LANES), 1)

    def sub_body(si, carry):
        r0 = pl.multiple_of(si * S5_SUB, S5_SUB)
        u = u_ref[pl.ds(r0, S5_SUB), :].astype(F32)
        bu = _mm(u, bm_ref[...])
        for lb in range(S5_LANES // LANES):
            ls = slice(lb * LANES, (lb + 1) * LANES)
            li = slice(S5_LANES + lb * LANES, S5_LANES + (lb + 1) * LANES)
            hr = bu[:, ls].reshape(n_grp, S5_ROWS, LANES)
            hi = bu[:, li].reshape(n_grp, S5_ROWS, LANES)
            for lv in range(n_lv):
                sh = 1 << lv
                lr = lam_ref[2 * lv:2 * lv + 1, ls]
                lm = lam_ref[2 * lv + 1:2 * lv + 2, ls]
                keep = row_i >= sh
                sr = jnp.where(keep, pltpu.roll(hr, sh, 1), 0.0)
                sm = jnp.where(keep, pltpu.roll(hi, sh, 1), 0.0)
                hr, hi = hr + (lr * sr - lm * sm), hi + (lr * sm + lm * sr)
            tr, tm = tab_ref[0, :, ls], tab_ref[1, :, ls]
            cr, ci = h0_scr[0:1, ls], h0_scr[1:2, ls]
            out_r, out_i = [], []
            for g in range(n_grp):
                gr = hr[g] + (tr * cr - tm * ci)
                gi = hi[g] + (tr * ci + tm * cr)
                cr, ci = gr[S5_ROWS - 1:S5_ROWS, :], gi[S5_ROWS - 1:S5_ROWS, :]
                out_r.append(gr)
                out_i.append(gi)
            h0_scr[0:1, ls] = cr
            h0_scr[1:2, ls] = ci
            hs_scr[:, ls] = jnp.concatenate(out_r, axis=0).astype(BF16)
            hs_scr[:, li] = jnp.concatenate(out_i, axis=0).astype(BF16)
        y = jnp.dot(hs_scr[...], cm_ref[...], preferred_element_type=F32) + d_ref[...] * u
        y = 0.5 * y * (1.0 + jnp.tanh(0.7978845608028654 * (y + 0.044715 * (y * y * y))))
        z = _mm(y, gw_ref[...]) + gb_ref[...]
        o_ref[pl.ds(r0, S5_SUB), :] = (y * _sigmoid(z)).astype(o_ref.dtype)
        return carry

    lax.fori_loop(0, tl // S5_SUB, sub_body, 0)


def _s5(p, bm, cm, lam_pows, tab, d_skip, glu_w, glu_b, tl):
    b, lp, _ = p.shape
    c2 = lambda a: pl.BlockSpec(a.shape, lambda i, t: (0,) * a.ndim)
    return pl.pallas_call(
        functools.partial(_s5_kernel, tl=tl),
        grid=(b, lp // tl),
        in_specs=[pl.BlockSpec((None, tl, S5_DIM), lambda i, t: (i, t, RWKV_COLS // S5_DIM)),
                  c2(bm), c2(cm), c2(lam_pows), c2(tab), c2(d_skip), c2(glu_w), c2(glu_b)],
        out_specs=pl.BlockSpec((None, tl, S5_DIM), lambda i, t: (i, t, 0)),
        out_shape=jax.ShapeDtypeStruct((b, lp, S5_DIM), BF16),
        scratch_shapes=[pltpu.VMEM((8, S5_LANES), F32),
                        pltpu.VMEM((S5_SUB, 2 * S5_LANES), BF16)],
        compiler_params=pltpu.CompilerParams(
            dimension_semantics=("parallel", "arbitrary"), vmem_limit_bytes=VMEM_LIMIT),
        name="s5_mix",
    )(p, bm, cm, lam_pows, tab, d_skip, glu_w, glu_b)


def _mix_ffn_kernel(*refs, tl, n_in):
    h_ref = refs[0]
    xs = refs[1:1 + n_in]
    ws = refs[1 + n_in:1 + 2 * n_in]
    g_ref, wg_ref, wv_ref, cw_ref, cb_ref, wd_ref, o_ref, xn_scr, prev_scr = refs[1 + 2 * n_in:]
    t = pl.program_id(1)
    n_ch = D_FF // FF_CHUNK

    @pl.when(t == 0)
    def _():
        prev_scr[...] = jnp.zeros_like(prev_scr)

    x = h_ref[...]
    for x_ref, w_ref in zip(xs, ws):
        x = x + jnp.dot(x_ref[...], w_ref[...], preferred_element_type=F32)
    xn_scr[...] = _rms_rows(x, g_ref[...]).astype(BF16)
    o_ref[...] = x

    def conv(hh, prev8, cw, cb):
        p6, p7 = prev8[6:7, :], prev8[7:8, :]
        s1 = _shift_rows(hh, 1, [p7])
        s2 = _shift_rows(hh, 2, [p6, p7])
        return cw[0:1, :] * s2 + cw[1:2, :] * s1 + cw[2:3, :] * hh + cb

    def up(c):
        cs = slice(c * FF_CHUNK, (c + 1) * FF_CHUNK)
        xn = xn_scr[...]
        return (jnp.dot(xn, wg_ref[:, cs], preferred_element_type=F32),
                jnp.dot(xn, wv_ref[:, cs], preferred_element_type=F32))

    pending = [up(c) for c in range(FF_DEPTH)]
    for c in range(n_ch):
        hg, hv = pending.pop(0)
        if c + FF_DEPTH < n_ch:
            pending.append(up(c + FF_DEPTH))
        cg = conv(hg, prev_scr[0, c], cw_ref[0, c], cb_ref[0, c])
        cv = conv(hv, prev_scr[1, c], cw_ref[1, c], cb_ref[1, c])
        prev_scr[0, c] = hg[tl - 8:tl, :]
        prev_scr[1, c] = hv[tl - 8:tl, :]
        act = (cg * _sigmoid(cg) * cv).astype(BF16)
        o_ref[...] += jnp.dot(act, wd_ref[c * FF_CHUNK:(c + 1) * FF_CHUNK, :], preferred_element_type=F32)


def _mix_ffn(h, xs, ws, gain, wg, wv, cw, cb, wd, tl):
    b, lp, d = h.shape
    n_ch = D_FF // FF_CHUNK
    full = lambda a: pl.BlockSpec(a.shape, lambda i, t: (0,) * a.ndim, pipeline_mode=pl.Buffered(1))
    row = lambda n: pl.BlockSpec((None, tl, n), lambda i, t: (i, t, 0))
    consts = (*ws, gain, wg, wv, cw, cb, wd)
    return pl.pallas_call(
        functools.partial(_mix_ffn_kernel, tl=tl, n_in=len(xs)),
        grid=(b, lp // tl),
        in_specs=[row(d)] + [row(x.shape[-1]) for x in xs] + [full(a) for a in consts],
        out_specs=row(d),
        out_shape=jax.ShapeDtypeStruct((b, lp, d), F32),
        scratch_shapes=[pltpu.VMEM((tl, d), BF16),
                        pltpu.VMEM((2, n_ch, 8, FF_CHUNK), F32)],
        compiler_params=pltpu.CompilerParams(
            dimension_semantics=("parallel", "arbitrary"), vmem_limit_bytes=VMEM_LIMIT),
        name="mix_out_mlp",
    )(h, *xs, *consts)


def _fox_in_kernel(x_ref, g_ref, w_ref, qg_ref, kg_ref, bf_ref, avg_ref, tri_ref,
                   pq_ref, pk_ref, oq_ref, ok_ref, q_ref, k_ref, v_ref, cb_ref, carry_scr, *, tl):
    t = pl.program_id(1)

    @pl.when(t == 0)
    def _():
        carry_scr[...] = jnp.zeros_like(carry_scr)

    xn = _rms_rows(x_ref[...], g_ref[...]).astype(BF16)
    v_ref[...] = jnp.dot(xn, w_ref[:, 2 * D_MODEL:3 * D_MODEL], preferred_element_type=F32).astype(BF16)
    f = jnp.dot(xn, w_ref[:, 3 * D_MODEL:], preferred_element_type=F32) + bf_ref[...]
    log_f = jnp.minimum(f, 0.0) - jnp.log(1.0 + jnp.exp(-jnp.abs(f)))
    cum = _cumsum_rows(tri_ref[...], log_f) + carry_scr[0:1, :]
    carry_scr[0:1, :] = cum[tl - 1:tl, :]
    lane = lax.broadcasted_iota(jnp.int32, (1, LANES), 1)
    c = jnp.where(lane < FOX_HEADS, cum * LOG2E, 0.0)
    cb_ref[...] = jnp.concatenate([c[0:1, :], c[tl - 1:tl, :], jnp.zeros((6, LANES), F32)], axis=0)
    c1 = c.astype(BF16).astype(F32)
    c2 = (c - c1).astype(BF16).astype(F32)
    c3 = (c - c1 - c2).astype(BF16).astype(F32)
    c123 = (c1 + pltpu.roll(c2, FOX_HEADS, 1) + pltpu.roll(c3, 2 * FOX_HEADS, 1)).astype(BF16)
    feat_q = jnp.dot(c123, pq_ref[...], preferred_element_type=F32) + oq_ref[...]
    feat_k = jnp.dot(c123, pk_ref[...], preferred_element_type=F32) + ok_ref[...]

    avg = avg_ref[...]
    m0 = lane < HEAD_DIM
    for off, gain_ref, feat, dst in ((0, qg_ref, feat_q, q_ref), (D_MODEL, kg_ref, feat_k, k_ref)):
        z_all = jnp.dot(xn, w_ref[:, off:off + D_MODEL], preferred_element_type=F32)
        for s in range(PAIRS_FOX):
            z = z_all[:, s * LANES:(s + 1) * LANES]
            ms = _mm(z * z, avg)
            zn = z * lax.rsqrt(ms + RMS_EPS) * gain_ref[...]
            for hh in range(2):
                hs = slice((2 * s + hh) * LANES, (2 * s + hh + 1) * LANES)
                own = m0 if hh == 0 else jnp.logical_not(m0)
                dst[:, hs] = jnp.where(own, zn, feat[:, hs]).astype(BF16)


def _fox_in(h, gain, w, qg, kg, bf, avg, tri, pq, pk, oq, ok, tl):
    b, lp, d = h.shape
    c2 = lambda a: pl.BlockSpec(a.shape, lambda i, t: (0,) * a.ndim)
    act = lambda n: (pl.BlockSpec((None, tl, n), lambda i, t: (i, t, 0)),
                     jax.ShapeDtypeStruct((b, lp, n), BF16))
    bounds = (pl.BlockSpec((None, None, 8, LANES), lambda i, t: (i, t, 0, 0)),
              jax.ShapeDtypeStruct((b, lp // tl, 8, LANES), F32))
    specs, shapes = zip(act(2 * d), act(2 * d), act(d), bounds)
    consts = (gain, w, qg, kg, bf, avg, tri, pq, pk, oq, ok)
    return pl.pallas_call(
        functools.partial(_fox_in_kernel, tl=tl),
        grid=(b, lp // tl),
        in_specs=[pl.BlockSpec((None, tl, d), lambda i, t: (i, t, 0))] + [c2(a) for a in consts],
        out_specs=list(specs),
        out_shape=list(shapes),
        scratch_shapes=[pltpu.VMEM((8, LANES), F32)],
        compiler_params=pltpu.CompilerParams(
            dimension_semantics=("parallel", "arbitrary"), vmem_limit_bytes=VMEM_LIMIT),
        name="fox_in_proj",
    )(h, *consts)


def _fox_feature_consts():
    pq = np.zeros((LANES, FOX_HEADS * LANES), np.float32)
    pk = np.zeros_like(pq)
    oq = np.zeros((1, FOX_HEADS * LANES), np.float32)
    ok = np.zeros_like(oq)
    for h in range(FOX_HEADS):
        base = h * LANES + HEAD_DIM * (1 - h % 2)
        for i in range(3):
            pq[i * FOX_HEADS + h, base + i] = 1.0
            pk[i * FOX_HEADS + h, base + 3 + i] = -1.0
            oq[0, base + 3 + i] = 1.0
            ok[0, base + i] = 1.0
    return jnp.asarray(pq, BF16), jnp.asarray(pk, BF16), jnp.asarray(oq), jnp.asarray(ok)


def _fox_attn_kernel(first_ref, q_ref, k_ref, v_ref, o_ref, m_scr, acc_scr, *, ta):
    qi = pl.program_id(2)
    m_scr[...] = jnp.full_like(m_scr, -1e30)
    acc_scr[...] = jnp.zeros_like(acc_scr)

    lane = lax.broadcasted_iota(jnp.int32, (1, LANES), 1)
    m0 = lane < HEAD_DIM
    ones_lane = (lane == HEAD_DIM, lane == 0)

    rt = min(ta, ATTN_ROWS)

    def step(k0s, masked):
        units = [(kb, r, h) for kb in range(len(k0s)) for r in range(ta // rt) for h in range(2)]

        def n_keys(r):
            return (r + 1) * rt if masked else ta

        va = []
        for k0 in k0s:
            v = v_ref[pl.ds(k0, ta), :]
            va.append([jnp.where(m0 if h == 0 else jnp.logical_not(m0), v,
                                 jnp.where(ones_lane[h], 1.0, 0.0).astype(BF16)) for h in range(2)])

        def scores(u):
            kb, r, h = u
            hs = slice(h * LANES, (h + 1) * LANES)
            return lax.dot_general(q_ref[r * rt:(r + 1) * rt, hs], k_ref[pl.ds(k0s[kb], n_keys(r)), hs],
                                   (((1,), (1,)), ((), ())), preferred_element_type=F32)

        def finish(u, sc):
            kb, r, h = u
            rows = slice(r * rt, (r + 1) * rt)
            if masked:
                rr = lax.broadcasted_iota(jnp.int32, sc.shape, 0) + r * rt
                cc = lax.broadcasted_iota(jnp.int32, sc.shape, 1)
                sc = jnp.where(cc <= rr, sc, -1e30)
            m_prev = m_scr[h, rows]
            m_new = jnp.maximum(m_prev, jnp.max(sc, axis=1, keepdims=True))
            p = jnp.exp2(sc - m_new).astype(BF16)
            alpha = jnp.exp2(m_prev - m_new)
            m_scr[h, rows] = m_new
            acc_scr[h, rows] = alpha * acc_scr[h, rows] + jnp.dot(p, va[kb][h][:n_keys(r)],
                                                                  preferred_element_type=F32)

        pending = [scores(u) for u in units[:ATTN_DEPTH]]
        for i, u in enumerate(units):
            sc = pending.pop(0)
            if i + ATTN_DEPTH < len(units):
                pending.append(scores(units[i + ATTN_DEPTH]))
            finish(u, sc)

    step([pl.multiple_of(qi * ta, ta)], True)
    first = first_ref[(pl.program_id(0) * PAIRS_FOX + pl.program_id(1)) * pl.num_programs(2) + qi]
    count = qi - first

    def body(kj, carry):
        k0 = pl.multiple_of((first + 2 * kj) * ta, ta)
        step([k0, k0 + ta], False)
        return carry

    lax.fori_loop(0, count // 2, body, 0)

    @pl.when(count % 2 == 1)
    def _():
        step([pl.multiple_of((qi - 1) * ta, ta)], False)

    a0, a1 = acc_scr[0], acc_scr[1]
    o0 = a0 / a0[:, HEAD_DIM:HEAD_DIM + 1]
    o1 = a1 / a1[:, 0:1]
    o_ref[...] = jnp.where(m0, o0, o1).astype(o_ref.dtype)


def _fox_first_block(cb, qg, kg):
    bsz, nq = cb.shape[:2]
    c_first = cb[:, :, 0, :FOX_HEADS]
    c_last = cb[:, :, 1, :FOX_HEADS]
    bound = 64.0 * jnp.max(jnp.abs(qg)) * jnp.max(jnp.abs(kg))
    reach = 2.0 * bound + c_first[:, :, None, :] - c_last[:, None, :, :]
    needed = (reach >= -EXP2_ZERO).reshape(bsz, nq, nq, PAIRS_FOX, 2).any(-1)
    earlier = jnp.arange(nq)[None, :] < jnp.arange(nq)[:, None]
    count = jnp.sum(needed & earlier[None, :, :, None], axis=2)
    first = jnp.arange(nq)[None, :, None] - count
    return jnp.transpose(first, (0, 2, 1)).reshape(-1).astype(jnp.int32)


def _fox_attn(first, q, k, v, ta):
    b, lp, d = v.shape
    grid_spec = pltpu.PrefetchScalarGridSpec(
        num_scalar_prefetch=1,
        grid=(b, PAIRS_FOX, lp // ta),
        in_specs=[pl.BlockSpec((None, ta, 2 * LANES), lambda i, h, t, f: (i, t, h)),
                  pl.BlockSpec((None, lp, 2 * LANES), lambda i, h, t, f: (i, 0, h)),
                  pl.BlockSpec((None, lp, LANES), lambda i, h, t, f: (i, 0, h))],
        out_specs=pl.BlockSpec((None, ta, LANES), lambda i, h, t, f: (i, t, h)),
        scratch_shapes=[pltpu.VMEM((2, ta, 1), F32),
                        pltpu.VMEM((2, ta, LANES), F32)],
    )
    return pl.pallas_call(
        functools.partial(_fox_attn_kernel, ta=ta),
        grid_spec=grid_spec,
        out_shape=jax.ShapeDtypeStruct((b, lp, d), BF16),
        compiler_params=pltpu.CompilerParams(
            dimension_semantics=("parallel", "parallel", "parallel"), vmem_limit_bytes=VMEM_LIMIT),
        name="fox_attention",
    )(first, q, k, v)


def _block_diag(blocks):
    g, r, c = blocks.shape
    eye = jnp.eye(g, dtype=blocks.dtype)
    return (eye[:, None, :, None] * blocks[:, :, None, :]).reshape(g * r, g * c)


def _cmul(ar, ai, br, bi):
    return ar * br - ai * bi, ar * bi + ai * br


def _s5_params(a_re, a_im, log_dt, b_re, b_im, c_re, c_im):
    dt = jnp.exp(log_dt)[:, None]
    mag = jnp.exp(a_re * dt)
    lam_re, lam_im = mag * jnp.cos(a_im * dt), mag * jnp.sin(a_im * dt)
    den = a_re * a_re + a_im * a_im
    z_re = ((lam_re - 1.0) * a_re + lam_im * a_im) / den
    z_im = (lam_im * a_re - (lam_re - 1.0) * a_im) / den
    bb_re = z_re[..., None] * b_re - z_im[..., None] * b_im
    bb_im = z_re[..., None] * b_im + z_im[..., None] * b_re
    to_in = lambda m: _block_diag(jnp.swapaxes(m, 1, 2))
    bm = jnp.concatenate([to_in(bb_re), to_in(bb_im)], axis=1).astype(BF16)
    to_out = lambda m: _block_diag(jnp.swapaxes(m, 1, 2))
    cm = jnp.concatenate([to_out(c_re), -to_out(c_im)], axis=0).astype(BF16)
    lr, li = lam_re.reshape(1, S5_LANES), lam_im.reshape(1, S5_LANES)
    n_lv = int(math.log2(S5_ROWS))
    pows, tr, ti = [], lr, li
    pr, pi = lr, li
    for _ in range(n_lv):
        pows += [pr, pi]
        nr, ni = _cmul(tr, ti, pr, pi)
        tr, ti = jnp.concatenate([tr, nr], axis=0), jnp.concatenate([ti, ni], axis=0)
        pr, pi = _cmul(pr, pi, pr, pi)
    pows = jnp.concatenate(pows + [jnp.zeros((16 - 2 * n_lv, S5_LANES), F32)], axis=0)
    return bm, cm, pows, jnp.stack([tr, ti])


def _rwkv_consts():
    idx = np.arange(LANES)
    hsum = (idx[:, None] // HEAD_DIM == idx[None, :] // HEAD_DIM).astype(np.float32)
    t = idx % CHUNK
    lv = [(t[:, None] // 2 == t[None, :] // 2)]
    b = 2
    while b < CHUNK:
        lv.append((t[:, None] // (2 * b) == t[None, :] // (2 * b)) & (t[:, None] // b != t[None, :] // b))
        b *= 2
    tri = (np.arange(CHUNK)[None, :] <= np.arange(CHUNK)[:, None]).astype(np.float32)
    return (jnp.asarray(hsum, BF16), jnp.asarray(tri, BF16), jnp.asarray(np.stack(lv).astype(np.float32)))


def _time_tile(length):
    return 768 if length > 2048 else 128


def kernel(x, meta_tokens, mix_norm, ffn_norm, even_w_in, even_w_out, rwkv_mu, rwkv_w0, rwkv_w2, rwkv_a0, rwkv_a2, rwkv_g2, rwkv_k_k, rwkv_k_a, rwkv_r_k, rwkv_ln_w, rwkv_ln_b, s5_a_re, s5_a_im, s5_log_dt, s5_b_re, s5_b_im, s5_c_re, s5_c_im, s5_d, s5_glu_w, s5_glu_b, odd_w_in, odd_w_out, fox_b_f, fox_q_gain, fox_k_gain, ffn_w_up, ffn_conv_w, ffn_conv_b, ffn_w_down):
    bsz, seq, d = x.shape
    length = N_META + seq
    tl = _time_tile(length)
    lp = -(-length // tl) * tl
    meta = jnp.broadcast_to(meta_tokens[None].astype(x.dtype), (bsz, N_META, d))
    h = jnp.concatenate([meta, x, jnp.zeros((bsz, lp - length, d), x.dtype)], axis=1)

    hsum, tri64, lvl = _rwkv_consts()
    n_ch = D_FF // FF_CHUNK

    def mix_ffn(h, xs, ws, layer):
        w_up = ffn_w_up[layer].astype(BF16)
        cw = ffn_conv_w[layer].reshape(3, 2, n_ch, FF_CHUNK).transpose(1, 2, 0, 3)
        cw = jnp.pad(cw, ((0, 0), (0, 0), (0, 5), (0, 0)))
        cb = ffn_conv_b[layer].reshape(2, n_ch, 1, FF_CHUNK)
        return _mix_ffn(h, xs, ws, ffn_norm[layer][None], w_up[:, :D_FF], w_up[:, D_FF:], cw, cb,
                        ffn_w_down[layer].astype(BF16), tl)

    p = _rms_matmul(h, mix_norm[0][None], even_w_in[0].astype(BF16), tl, BF16)
    pairs = lambda vec: vec.reshape(PAIRS_RWKV, 1, LANES)
    mu = rwkv_mu[0]
    prm = jnp.concatenate(
        [pairs(mu[:RWKV_DIM]), pairs(mu[RWKV_DIM:2 * RWKV_DIM]), pairs(mu[2 * RWKV_DIM:3 * RWKV_DIM]),
         pairs(rwkv_w0[0]), pairs(rwkv_a0[0]), pairs(rwkv_k_k[0]), pairs(rwkv_k_a[0]),
         pairs(rwkv_r_k[0].reshape(-1)), pairs(rwkv_ln_w[0]), pairs(rwkv_ln_b[0]),
         jnp.zeros((PAIRS_RWKV, 6, LANES), F32)], axis=1)
    zeros64 = jnp.zeros((64, RWKV_DIM), F32)
    w2p = jnp.concatenate([rwkv_w2[0], zeros64], axis=0).astype(BF16)
    a2p = jnp.concatenate([zeros64, rwkv_a2[0]], axis=0).astype(BF16)
    a_out = _rwkv(p, prm, mu[3 * RWKV_DIM:][None], w2p, a2p, rwkv_g2[0].astype(BF16),
                  hsum, tri64, lvl, tl, tl // CHUNK)
    bm, cm, pows, tab = _s5_params(s5_a_re[0], s5_a_im[0], s5_log_dt[0], s5_b_re[0], s5_b_im[0],
                                   s5_c_re[0], s5_c_im[0])
    b_out = _s5(p, bm, cm, pows, tab, s5_d[0][None], s5_glu_w[0].astype(BF16), s5_glu_b[0][None], tl)
    w_out = even_w_out[0].astype(BF16)
    h = mix_ffn(h, [a_out, b_out], [w_out[:RWKV_DIM], w_out[RWKV_DIM:]], 0)

    w_in = jnp.pad(odd_w_in[0], ((0, 0), (0, ODD_IN_PAD - odd_w_in.shape[-1]))).astype(BF16)
    scale = HEAD_DIM ** -0.5
    qg = jnp.tile(fox_q_gain[0], 2)[None] * (scale * LOG2E)
    kg = jnp.tile(fox_k_gain[0], 2)[None]
    bf = jnp.pad(fox_b_f[0], (0, LANES - FOX_HEADS))[None]
    tri_t = jnp.asarray(np.tril(np.ones((tl, tl), np.float32)), BF16)
    q, k, v, cb = _fox_in(h, mix_norm[1][None], w_in, qg, kg, bf, hsum * (1.0 / HEAD_DIM), tri_t,
                      *_fox_feature_consts(), tl)
    o = _fox_attn(_fox_first_block(cb, qg, kg), q, k, v, tl)
    h = mix_ffn(h, [o], [odd_w_out[0].astype(BF16)], 1)
    return h[:, N_META:N_META + seq]
```

```python
import functools
import math

import jax
import jax.numpy as jnp
import numpy as np
from jax import lax
from jax.experimental import pallas as pl
from jax.experimental.pallas import tpu as pltpu

F32 = jnp.float32
BF16 = jnp.bfloat16

D_MODEL = 1024
N_META = 16
RMS_EPS = 1e-6
GN_EPS = 64e-5

RWKV_HEADS = 12
HEAD_DIM = 64
RWKV_DIM = RWKV_HEADS * HEAD_DIM
LORA_COLS = 256
RWKV_COLS = 3 * RWKV_DIM + LORA_COLS
S5_DIM = 256
S5_GROUP = 16
S5_GROUPS = 16
S5_STATE = 64
S5_LANES = S5_GROUPS * S5_STATE
EVEN_IN = RWKV_COLS + S5_DIM
FOX_HEADS = 16
ODD_IN_PAD = 3 * D_MODEL + 128
D_FF = 2816
LANES = 128
PAIRS_RWKV = RWKV_DIM // LANES
PAIRS_FOX = D_MODEL // LANES
CHUNK = 64
S5_SUB = 128
S5_ROWS = 8
FF_CHUNK = 256
FF_DEPTH = 2
VMEM_LIMIT = 56 * 1024 * 1024
LOG2E = 1.4426950408889634
ATTN_ROWS = 256
ATTN_DEPTH = 4
EXP2_ZERO = 160.0


def _mm(a, b):
    return jnp.dot(a.astype(BF16), b.astype(BF16), preferred_element_type=F32)


def _mm_nt(a, b):
    return lax.dot_general(a.astype(BF16), b.astype(BF16), (((1,), (1,)), ((), ())),
                           preferred_element_type=F32)


def _mm_tn(a, b):
    return lax.dot_general(a.astype(BF16), b.astype(BF16), (((0,), (0,)), ((), ())),
                           preferred_element_type=F32)


def _cumsum_rows(tri, x):
    h1 = x.astype(BF16)
    r1 = x - h1.astype(F32)
    h2 = r1.astype(BF16)
    h3 = (r1 - h2.astype(F32)).astype(BF16)
    w = x.shape[1]
    res = jnp.dot(tri, jnp.concatenate([h1, h2, h3], axis=1), preferred_element_type=F32)
    return res[:, :w] + res[:, w:2 * w] + res[:, 2 * w:]


def _softplus(z):
    return jnp.maximum(z, 0.0) + jnp.log(1.0 + jnp.exp(-jnp.abs(z)))


def _sigmoid(z):
    return 1.0 / (1.0 + jnp.exp(-z))


def _rms_rows(x, gain):
    ms = jnp.mean(x * x, axis=-1, keepdims=True)
    return x * lax.rsqrt(ms + RMS_EPS) * gain


def _shift_rows(x, k, prev_rows):
    rolled = pltpu.roll(x, k, 0)
    row = lax.broadcasted_iota(jnp.int32, x.shape, 0)
    for i in range(k):
        rolled = jnp.where(row == i, prev_rows[i], rolled)
    return rolled


def _rms_matmul_kernel(x_ref, g_ref, w_ref, o_ref, *, n_chunk):
    xn = _rms_rows(x_ref[...], g_ref[...]).astype(BF16)
    n = o_ref.shape[-1]
    for j in range(0, n, n_chunk):
        o_ref[:, j:j + n_chunk] = jnp.dot(xn, w_ref[:, j:j + n_chunk],
                                          preferred_element_type=F32).astype(o_ref.dtype)


def _rms_matmul(h, gain, w, tl, out_dtype):
    b, lp, d = h.shape
    n = w.shape[1]
    return pl.pallas_call(
        functools.partial(_rms_matmul_kernel, n_chunk=256),
        grid=(b, lp // tl),
        in_specs=[pl.BlockSpec((None, tl, d), lambda i, t: (i, t, 0)),
                  pl.BlockSpec((1, d), lambda i, t: (0, 0)),
                  pl.BlockSpec((d, n), lambda i, t: (0, 0))],
        out_specs=pl.BlockSpec((None, tl, n), lambda i, t: (i, t, 0)),
        out_shape=jax.ShapeDtypeStruct((b, lp, n), out_dtype),
        compiler_params=pltpu.CompilerParams(
            dimension_semantics=("parallel", "parallel"), vmem_limit_bytes=VMEM_LIMIT),
        name="rms_in_proj",
    )(h, gain, w)


def _rwkv_kernel(r_ref, k_ref, v_ref, lo_ref, prm_ref, mul_ref, w2_ref, a2_ref, g2_ref,
                 hsum_ref, tri_ref, lvl_ref, o_ref,
                 s_scr, prev_scr, prevl_scr, y_scr, q_scr, gh_scr, bg_scr, *, tl, group):
    t = pl.program_id(2)

    @pl.when(t == 0)
    def _():
        s_scr[...] = jnp.zeros_like(s_scr)
        prev_scr[...] = jnp.zeros_like(prev_scr)
        prevl_scr[...] = jnp.zeros_like(prevl_scr)
        gh_scr[...] = jnp.zeros_like(gh_scr)
        bg_scr[...] = jnp.zeros_like(bg_scr)

    prm = prm_ref[...]
    row = lambda i: prm[i:i + 1, :]
    hsum = hsum_ref[...]

    def tshift(x, prev, mu):
        prevx = _shift_rows(x, 1, [prev])
        return x + (prevx - x) * mu

    r_raw, k_raw, v_raw, lo_raw = (ref[...].astype(F32) for ref in (r_ref, k_ref, v_ref, lo_ref))
    r = tshift(r_raw, prev_scr[0:1, :], row(0))
    k = tshift(k_raw, prev_scr[1:2, :], row(1))
    v = tshift(v_raw, prev_scr[2:3, :], row(2))
    lo = tshift(lo_raw, prevl_scr[0:1, :], mul_ref[...])
    prev_scr[0:1, :] = r_raw[tl - 1:tl, :]
    prev_scr[1:2, :] = k_raw[tl - 1:tl, :]
    prev_scr[2:3, :] = v_raw[tl - 1:tl, :]
    prevl_scr[0:1, :] = lo_raw[tl - 1:tl, :]

    lo01 = lo[:, :LANES]
    w_pre = row(3) + _mm(jnp.tanh(lo01), w2_ref[...])
    logw = -jnp.exp(-_softplus(-w_pre) - 0.5)
    a = _sigmoid(row(4) + _mm(lo01, a2_ref[...]))
    g = _mm(_sigmoid(lo[:, LANES:]), g2_ref[...])
    kk = k * row(5)
    kk = kk / jnp.maximum(jnp.sqrt(_mm(kk * kk, hsum)), 1e-12)
    kf = k * (1.0 + (a - 1.0) * row(6))
    ka = kk * a
    bonus = _mm(r * kf * row(7), hsum) * v

    q_scr[0] = r
    q_scr[1] = logw
    q_scr[2] = kf
    q_scr[3] = v
    q_scr[4] = kk
    q_scr[5] = ka

    lane = lax.broadcasted_iota(jnp.int32, (1, LANES), 1)
    m0 = lane < HEAD_DIM
    tri = tri_ref[...]
    ri = lax.broadcasted_iota(jnp.int32, (LANES, LANES), 0)
    ci = lax.broadcasted_iota(jnp.int32, (LANES, LANES), 1)
    rt_, ct_ = ri & (CHUNK - 1), ci & (CHUNK - 1)
    strict = ct_ < rt_
    incl = ct_ <= rt_
    eye = ri == ci

    def stack(x):
        return jnp.concatenate([jnp.where(m0, x, 0.0), jnp.where(m0, 0.0, x)], axis=0).astype(BF16)

    def group_pre(bases, between):
        rng = range(len(bases))
        ld = [[q_scr[i, pl.ds(b, CHUNK), :] for i in range(6)] for b in bases]
        cum = [_cumsum_rows(tri, ld[j][1]) for j in rng]
        ops = []
        for j in rng:
            rc, lw, kc, vc, kkc, kac = ld[j]
            cum_c = cum[j][CHUNK - 1:CHUNK, :]
            w_in = jnp.exp(cum[j])
            w_inv = jnp.exp(-cum[j])
            w_ex = jnp.exp(cum[j] - lw)
            w_end = jnp.exp(cum_c - cum[j])
            ops.append(dict(
                at=stack(-kkc * w_ex), rt=stack(rc * w_in), bt=stack(kac * w_inv), kt=stack(kc * w_inv),
                bh=stack(kac * w_end), kh=stack(kc * w_end), v=stack(vc), wc=jnp.exp(cum_c)))
        am = [_mm_nt(jnp.concatenate([o["at"], o["rt"]], axis=0), jnp.concatenate([o["bt"], o["kt"]], axis=0))
              for o in ops]
        between()
        l_ab = [jnp.where(strict, m[:LANES, :LANES], 0.0) for m in am]
        a_ak = [jnp.where(strict, m[:LANES, LANES:], 0.0).astype(BF16) for m in am]
        m_rb = [jnp.where(incl, m[LANES:, :LANES], 0.0).astype(BF16) for m in am]
        m_rk = [jnp.where(incl, m[LANES:, LANES:], 0.0).astype(BF16) for m in am]
        lv0 = lvl_ref[0]
        tm = [jnp.where(eye, 1.0, 0.0) + l * lv0 for l in l_ab]
        av = [_mm(a_ak[j], ops[j]["v"]) for j in rng]
        between()
        for lv in range(1, 6):
            b = 1 << lv
            lvm = lvl_ref[lv]
            if b < 8:
                z = [_mm(l_ab[j] * lvm, tm[j]) for j in rng]
                between()
                tm = [tm[j] + _mm(tm[j], z[j]) for j in rng]
                between()
            else:
                odd = lambda x: jnp.concatenate([x[s:s + b] for s in range(b, LANES, 2 * b)], axis=0)
                lvm_o = odd(lvm)
                z_o = [_mm(odd(l_ab[j]) * lvm_o, tm[j]) for j in rng]
                between()
                zero = jnp.zeros((b, LANES), F32)

                def spread(y, base=None):
                    out = []
                    for i, s in enumerate(range(0, LANES, b)):
                        piece = y[(i // 2) * b:(i // 2 + 1) * b] if i % 2 else None
                        if base is None:
                            out.append(piece if i % 2 else zero)
                        else:
                            out.append(base[s:s + b] + piece if i % 2 else base[s:s + b])
                    return jnp.concatenate(out, axis=0)

                upd = [_mm(odd(tm[j]), spread(z_o[j])) for j in rng]
                tm = [spread(upd[j], tm[j]) for j in rng]
                between()
        pq = [_mm(tm[j], jnp.concatenate([ops[j]["at"], av[j].astype(BF16)], axis=1)) for j in rng]
        between()
        p_s = [m[:, :LANES].astype(BF16) for m in pq]
        qv = [jnp.concatenate([pq[j][:, LANES:].astype(BF16), ops[j]["v"]], axis=0) for j in rng]
        rp = [ops[j]["rt"].astype(F32) + _mm(m_rb[j], p_s[j]) for j in rng]
        y0 = [_mm(jnp.concatenate([m_rb[j], m_rk[j]], axis=1), qv[j]) for j in rng]
        gm = [jnp.where(eye, ops[j]["wc"], 0.0) + _mm_tn(ops[j]["bh"], p_s[j]) for j in rng]
        hm = [_mm_tn(jnp.concatenate([ops[j]["bh"], ops[j]["kh"]], axis=0), qv[j]) for j in rng]
        return rp, y0, gm, hm

    chain = {"j": 0, "s": s_scr[...]}

    def chain_step():
        j, s = chain["j"], chain["s"]
        if j == group:
            return
        ys = _mm(gh_scr[j, 0], s) + gh_scr[j, 1]
        y_scr[j * CHUNK:(j + 1) * CHUNK, :] = ys[:CHUNK] + ys[CHUNK:]
        chain["s"] = _mm(gh_scr[j, 2], s) + gh_scr[j, 3]
        chain["j"] = j + 1

    new_ops = group_pre([j * CHUNK for j in range(group)], chain_step)
    while chain["j"] < group:
        chain_step()
    s_scr[...] = chain["s"]

    y = y_scr[...]
    mean = _mm(y, hsum) * (1.0 / HEAD_DIM)
    yc = y - mean
    var = _mm(yc * yc, hsum) * (1.0 / HEAD_DIM)
    yn = yc * lax.rsqrt(var + GN_EPS) * row(8) + row(9)
    o_ref[...] = ((yn + bg_scr[0]) * bg_scr[1]).astype(o_ref.dtype)

    for i, vals in enumerate(new_ops):
        for j in range(group):
            gh_scr[j, i] = vals[j]
    bg_scr[0] = bonus
    bg_scr[1] = g


def _rwkv(p, prm, mu_l, w2p, a2p, g2, hsum, tri, lvl, tl, group):
    b, lp, _ = p.shape
    nt = lp // tl
    cur = lambda t: jnp.minimum(t, nt - 1)
    blk = lambda off: pl.BlockSpec((None, tl, LANES), lambda i, h, t: (i, cur(t), off + h))
    const2 = lambda shape: pl.BlockSpec(shape, lambda i, h, t: (0, 0))
    return pl.pallas_call(
        functools.partial(_rwkv_kernel, tl=tl, group=group),
        grid=(b, PAIRS_RWKV, nt + 1),
        in_specs=[blk(0), blk(PAIRS_RWKV), blk(2 * PAIRS_RWKV),
                  pl.BlockSpec((None, tl, LORA_COLS), lambda i, h, t: (i, cur(t), 3 * RWKV_DIM // LORA_COLS)),
                  pl.BlockSpec((None, 16, LANES), lambda i, h, t: (h, 0, 0)),
                  const2((1, LORA_COLS)),
                  pl.BlockSpec((LANES, LANES), lambda i, h, t: (0, h)),
                  pl.BlockSpec((LANES, LANES), lambda i, h, t: (0, h)),
                  pl.BlockSpec((LANES, LANES), lambda i, h, t: (0, h)),
                  const2((LANES, LANES)),
                  const2((CHUNK, CHUNK)),
                  pl.BlockSpec((6, LANES, LANES), lambda i, h, t: (0, 0, 0))],
        out_specs=pl.BlockSpec((None, tl, LANES), lambda i, h, t: (i, jnp.maximum(t - 1, 0), h)),
        out_shape=jax.ShapeDtypeStruct((b, lp, RWKV_DIM), BF16),
        scratch_shapes=[pltpu.VMEM((LANES, LANES), F32),
                        pltpu.VMEM((8, LANES), F32),
                        pltpu.VMEM((8, LORA_COLS), F32),
                        pltpu.VMEM((tl, LANES), F32),
                        pltpu.VMEM((6, tl, LANES), F32),
                        pltpu.VMEM((group, 4, LANES, LANES), F32),
                        pltpu.VMEM((2, tl, LANES), F32)],
        compiler_params=pltpu.CompilerParams(
            dimension_semantics=("parallel", "parallel", "arbitrary"), vmem_limit_bytes=VMEM_LIMIT),
        name="rwkv7_mix",
    )(p, p, p, p, prm, mu_l, w2p, a2p, g2, hsum, tri, lvl)


def _s5_kernel(u_ref, bm_ref, cm_ref, lam_ref, tab_ref, d_ref, gw_ref, gb_ref, o_ref,
               h0_scr, hs_scr, *, tl):
    t = pl.program_id(1)

    @pl.when(t == 0)
    def _():
        h0_scr[...] = jnp.zeros_like(h0_scr)

    n_lv = int(math.log2(S5_ROWS))
    n_grp = S5_SUB // S5_ROWS
    row_i = lax.broadcasted_iota(jnp.int32, (n_grp, S5_ROWS, LANES), 1)

    def sub_body(si, carry):
        r0 = pl.multiple_of(si * S5_SUB, S5_SUB)
        u = u_ref[pl.ds(r0, S5_SUB), :].astype(F32)
        bu = _mm(u, bm_ref[...])
        for lb in range(S5_LANES // LANES):
            ls = slice(lb * LANES, (lb + 1) * LANES)
            li = slice(S5_LANES + lb * LANES, S5_LANES + (lb + 1) * LANES)
            hr = bu[:, ls].reshape(n_grp, S5_ROWS, LANES)
            hi = bu[:, li].reshape(n_grp, S5_ROWS, LANES)
            for lv in range(n_lv):
                sh = 1 << lv
                lr = lam_ref[2 * lv:2 * lv + 1, ls]
                lm = lam_ref[2 * lv + 1:2 * lv + 2, ls]
                keep = row_i >= sh
                sr = jnp.where(keep, pltpu.roll(hr, sh, 1), 0.0)
                sm = jnp.where(keep, pltpu.roll(hi, sh, 1), 0.0)
                hr, hi = hr + (lr * sr - lm * sm), hi + (lr * sm + lm * sr)
            tr, tm = tab_ref[0, :, ls], tab_ref[1, :, ls]
            cr, ci = h0_scr[0:1, ls], h0_scr[1:2, ls]
            out_r, out_i = [], []
            for g in range(n_grp):
                gr = hr[g] + (tr * cr - tm * ci)
                gi = hi[g] + (tr * ci + tm * cr)
                cr, ci = gr[S5_ROWS - 1:S5_ROWS, :], gi[S5_ROWS - 1:S5_ROWS, :]
                out_r.append(gr)
                out_i.append(gi)
            h0_scr[0:1, ls] = cr
            h0_scr[1:2, ls] = ci
            hs_scr[:, ls] = jnp.concatenate(out_r, axis=0).astype(BF16)
            hs_scr[:, li] = jnp.concatenate(out_i, axis=0).astype(BF16)
        y = jnp.dot(hs_scr[...], cm_ref[...], preferred_element_type=F32) + d_ref[...] * u
        y = 0.5 * y * (1.0 + jnp.tanh(0.7978845608028654 * (y + 0.044715 * (y * y * y))))
        z = _mm(y, gw_ref[...]) + gb_ref[...]
        o_ref[pl.ds(r0, S5_SUB), :] = (y * _sigmoid(z)).astype(o_ref.dtype)
        return carry

    lax.fori_loop(0, tl // S5_SUB, sub_body, 0)


def _s5(p, bm, cm, lam_pows, tab, d_skip, glu_w, glu_b, tl):
    b, lp, _ = p.shape
    c2 = lambda a: pl.BlockSpec(a.shape, lambda i, t: (0,) * a.ndim)
    return pl.pallas_call(
        functools.partial(_s5_kernel, tl=tl),
        grid=(b, lp // tl),
        in_specs=[pl.BlockSpec((None, tl, S5_DIM), lambda i, t: (i, t, RWKV_COLS // S5_DIM)),
                  c2(bm), c2(cm), c2(lam_pows), c2(tab), c2(d_skip), c2(glu_w), c2(glu_b)],
        out_specs=pl.BlockSpec((None, tl, S5_DIM), lambda i, t: (i, t, 0)),
        out_shape=jax.ShapeDtypeStruct((b, lp, S5_DIM), BF16),
        scratch_shapes=[pltpu.VMEM((8, S5_LANES), F32),
                        pltpu.VMEM((S5_SUB, 2 * S5_LANES), BF16)],
        compiler_params=pltpu.CompilerParams(
            dimension_semantics=("parallel", "arbitrary"), vmem_limit_bytes=VMEM_LIMIT),
        name="s5_mix",
    )(p, bm, cm, lam_pows, tab, d_skip, glu_w, glu_b)


def _mix_ffn_kernel(*refs, tl, n_in):
    h_ref = refs[0]
    xs = refs[1:1 + n_in]
    ws = refs[1 + n_in:1 + 2 * n_in]
    g_ref, wg_ref, wv_ref, cw_ref, cb_ref, wd_ref, o_ref, xn_scr, prev_scr = refs[1 + 2 * n_in:]
    t = pl.program_id(1)
    n_ch = D_FF // FF_CHUNK

    @pl.when(t == 0)
    def _():
        prev_scr[...] = jnp.zeros_like(prev_scr)

    x = h_ref[...]
    for x_ref, w_ref in zip(xs, ws):
        x = x + jnp.dot(x_ref[...], w_ref[...], preferred_element_type=F32)
    xn_scr[...] = _rms_rows(x, g_ref[...]).astype(BF16)
    o_ref[...] = x

    def conv(hh, prev8, cw, cb):
        p6, p7 = prev8[6:7, :], prev8[7:8, :]
        s1 = _shift_rows(hh, 1, [p7])
        s2 = _shift_rows(hh, 2, [p6, p7])
        return cw[0:1, :] * s2 + cw[1:2, :] * s1 + cw[2:3, :] * hh + cb

    def up(c):
        cs = slice(c * FF_CHUNK, (c + 1) * FF_CHUNK)
        xn = xn_scr[...]
        return (jnp.dot(xn, wg_ref[:, cs], preferred_element_type=F32),
                jnp.dot(xn, wv_ref[:, cs], preferred_element_type=F32))

    pending = [up(c) for c in range(FF_DEPTH)]
    for c in range(n_ch):
        hg, hv = pending.pop(0)
        if c + FF_DEPTH < n_ch:
            pending.append(up(c + FF_DEPTH))
        cg = conv(hg, prev_scr[0, c], cw_ref[0, c], cb_ref[0, c])
        cv = conv(hv, prev_scr[1, c], cw_ref[1, c], cb_ref[1, c])
        prev_scr[0, c] = hg[tl - 8:tl, :]
        prev_scr[1, c] = hv[tl - 8:tl, :]
        act = (cg * _sigmoid(cg) * cv).astype(BF16)
        o_ref[...] += jnp.dot(act, wd_ref[c * FF_CHUNK:(c + 1) * FF_CHUNK, :], preferred_element_type=F32)


def _mix_ffn(h, xs, ws, gain, wg, wv, cw, cb, wd, tl):
    b, lp, d = h.shape
    n_ch = D_FF // FF_CHUNK
    full = lambda a: pl.BlockSpec(a.shape, lambda i, t: (0,) * a.ndim, pipeline_mode=pl.Buffered(1))
    row = lambda n: pl.BlockSpec((None, tl, n), lambda i, t: (i, t, 0))
    consts = (*ws, gain, wg, wv, cw, cb, wd)
    return pl.pallas_call(
        functools.partial(_mix_ffn_kernel, tl=tl, n_in=len(xs)),
        grid=(b, lp // tl),
        in_specs=[row(d)] + [row(x.shape[-1]) for x in xs] + [full(a) for a in consts],
        out_specs=row(d),
        out_shape=jax.ShapeDtypeStruct((b, lp, d), F32),
        scratch_shapes=[pltpu.VMEM((tl, d), BF16),
                        pltpu.VMEM((2, n_ch, 8, FF_CHUNK), F32)],
        compiler_params=pltpu.CompilerParams(
            dimension_semantics=("parallel", "arbitrary"), vmem_limit_bytes=VMEM_LIMIT),
        name="mix_out_mlp",
    )(h, *xs, *consts)


def _fox_in_kernel(x_ref, g_ref, w_ref, qg_ref, kg_ref, bf_ref, avg_ref, tri_ref,
                   pq_ref, pk_ref, oq_ref, ok_ref, q_ref, k_ref, v_ref, cb_ref, carry_scr, *, tl):
    t = pl.program_id(1)

    @pl.when(t == 0)
    def _():
        carry_scr[...] = jnp.zeros_like(carry_scr)

    xn = _rms_rows(x_ref[...], g_ref[...]).astype(BF16)
    v_ref[...] = jnp.dot(xn, w_ref[:, 2 * D_MODEL:3 * D_MODEL], preferred_element_type=F32).astype(BF16)
    f = jnp.dot(xn, w_ref[:, 3 * D_MODEL:], preferred_element_type=F32) + bf_ref[...]
    log_f = jnp.minimum(f, 0.0) - jnp.log(1.0 + jnp.exp(-jnp.abs(f)))
    cum = _cumsum_rows(tri_ref[...], log_f) + carry_scr[0:1, :]
    carry_scr[0:1, :] = cum[tl - 1:tl, :]
    lane = lax.broadcasted_iota(jnp.int32, (1, LANES), 1)
    c = jnp.where(lane < FOX_HEADS, cum * LOG2E, 0.0)
    cb_ref[...] = jnp.concatenate([c[0:1, :], c[tl - 1:tl, :], jnp.zeros((6, LANES), F32)], axis=0)
    c1 = c.astype(BF16).astype(F32)
    c2 = (c - c1).astype(BF16).astype(F32)
    c3 = (c - c1 - c2).astype(BF16).astype(F32)
    c123 = (c1 + pltpu.roll(c2, FOX_HEADS, 1) + pltpu.roll(c3, 2 * FOX_HEADS, 1)).astype(BF16)
    feat_q = jnp.dot(c123, pq_ref[...], preferred_element_type=F32) + oq_ref[...]
    feat_k = jnp.dot(c123, pk_ref[...], preferred_element_type=F32) + ok_ref[...]

    avg = avg_ref[...]
    m0 = lane < HEAD_DIM
    for off, gain_ref, feat, dst in ((0, qg_ref, feat_q, q_ref), (D_MODEL, kg_ref, feat_k, k_ref)):
        z_all = jnp.dot(xn, w_ref[:, off:off + D_MODEL], preferred_element_type=F32)
        for s in range(PAIRS_FOX):
            z = z_all[:, s * LANES:(s + 1) * LANES]
            ms = _mm(z * z, avg)
            zn = z * lax.rsqrt(ms + RMS_EPS) * gain_ref[...]
            for hh in range(2):
                hs = slice((2 * s + hh) * LANES, (2 * s + hh + 1) * LANES)
                own = m0 if hh == 0 else jnp.logical_not(m0)
                dst[:, hs] = jnp.where(own, zn, feat[:, hs]).astype(BF16)


def _fox_in(h, gain, w, qg, kg, bf, avg, tri, pq, pk, oq, ok, tl):
    b, lp, d = h.shape
    c2 = lambda a: pl.BlockSpec(a.shape, lambda i, t: (0,) * a.ndim)
    act = lambda n: (pl.BlockSpec((None, tl, n), lambda i, t: (i, t, 0)),
                     jax.ShapeDtypeStruct((b, lp, n), BF16))
    bounds = (pl.BlockSpec((None, None, 8, LANES), lambda i, t: (i, t, 0, 0)),
              jax.ShapeDtypeStruct((b, lp // tl, 8, LANES), F32))
    specs, shapes = zip(act(2 * d), act(2 * d), act(d), bounds)
    consts = (gain, w, qg, kg, bf, avg, tri, pq, pk, oq, ok)
    return pl.pallas_call(
        functools.partial(_fox_in_kernel, tl=tl),
        grid=(b, lp // tl),
        in_specs=[pl.BlockSpec((None, tl, d), lambda i, t: (i, t, 0))] + [c2(a) for a in consts],
        out_specs=list(specs),
        out_shape=list(shapes),
        scratch_shapes=[pltpu.VMEM((8, LANES), F32)],
        compiler_params=pltpu.CompilerParams(
            dimension_semantics=("parallel", "arbitrary"), vmem_limit_bytes=VMEM_LIMIT),
        name="fox_in_proj",
    )(h, *consts)


def _fox_feature_consts():
    pq = np.zeros((LANES, FOX_HEADS * LANES), np.float32)
    pk = np.zeros_like(pq)
    oq = np.zeros((1, FOX_HEADS * LANES), np.float32)
    ok = np.zeros_like(oq)
    for h in range(FOX_HEADS):
        base = h * LANES + HEAD_DIM * (1 - h % 2)
        for i in range(3):
            pq[i * FOX_HEADS + h, base + i] = 1.0
            pk[i * FOX_HEADS + h, base + 3 + i] = -1.0
            oq[0, base + 3 + i] = 1.0
            ok[0, base + i] = 1.0
    return jnp.asarray(pq, BF16), jnp.asarray(pk, BF16), jnp.asarray(oq), jnp.asarray(ok)


def _fox_attn_kernel(first_ref, q_ref, k_ref, v_ref, o_ref, m_scr, acc_scr, *, ta):
    qi = pl.program_id(2)
    m_scr[...] = jnp.full_like(m_scr, -1e30)
    acc_scr[...] = jnp.zeros_like(acc_scr)

    lane = lax.broadcasted_iota(jnp.int32, (1, LANES), 1)
    m0 = lane < HEAD_DIM
    ones_lane = (lane == HEAD_DIM, lane == 0)

    rt = min(ta, ATTN_ROWS)

    def step(k0s, masked):
        units = [(kb, r, h) for kb in range(len(k0s)) for r in range(ta // rt) for h in range(2)]

        def n_keys(r):
            return (r + 1) * rt if masked else ta

        va = []
        for k0 in k0s:
            v = v_ref[pl.ds(k0, ta), :]
            va.append([jnp.where(m0 if h == 0 else jnp.logical_not(m0), v,
                                 jnp.where(ones_lane[h], 1.0, 0.0).astype(BF16)) for h in range(2)])

        def scores(u):
            kb, r, h = u
            hs = slice(h * LANES, (h + 1) * LANES)
            return lax.dot_general(q_ref[r * rt:(r + 1) * rt, hs], k_ref[pl.ds(k0s[kb], n_keys(r)), hs],
                                   (((1,), (1,)), ((), ())), preferred_element_type=F32)

        def finish(u, sc):
            kb, r, h = u
            rows = slice(r * rt, (r + 1) * rt)
            if masked:
                rr = lax.broadcasted_iota(jnp.int32, sc.shape, 0) + r * rt
                cc = lax.broadcasted_iota(jnp.int32, sc.shape, 1)
                sc = jnp.where(cc <= rr, sc, -1e30)
            m_prev = m_scr[h, rows]
            m_new = jnp.maximum(m_prev, jnp.max(sc, axis=1, keepdims=True))
            p = jnp.exp2(sc - m_new).astype(BF16)
            alpha = jnp.exp2(m_prev - m_new)
            m_scr[h, rows] = m_new
            acc_scr[h, rows] = alpha * acc_scr[h, rows] + jnp.dot(p, va[kb][h][:n_keys(r)],
                                                                  preferred_element_type=F32)

        pending = [scores(u) for u in units[:ATTN_DEPTH]]
        for i, u in enumerate(units):
            sc = pending.pop(0)
            if i + ATTN_DEPTH < len(units):
                pending.append(scores(units[i + ATTN_DEPTH]))
            finish(u, sc)

    step([pl.multiple_of(qi * ta, ta)], True)
    first = first_ref[(pl.program_id(0) * PAIRS_FOX + pl.program_id(1)) * pl.num_programs(2) + qi]
    count = qi - first

    def body(kj, carry):
        k0 = pl.multiple_of((first + 2 * kj) * ta, ta)
        step([k0, k0 + ta], False)
        return carry

    lax.fori_loop(0, count // 2, body, 0)

    @pl.when(count % 2 == 1)
    def _():
        step([pl.multiple_of((qi - 1) * ta, ta)], False)

    a0, a1 = acc_scr[0], acc_scr[1]
    o0 = a0 / a0[:, HEAD_DIM:HEAD_DIM + 1]
    o1 = a1 / a1[:, 0:1]
    o_ref[...] = jnp.where(m0, o0, o1).astype(o_ref.dtype)


def _fox_first_block(cb, qg, kg):
    bsz, nq = cb.shape[:2]
    c_first = cb[:, :, 0, :FOX_HEADS]
    c_last = cb[:, :, 1, :FOX_HEADS]
    bound = 64.0 * jnp.max(jnp.abs(qg)) * jnp.max(jnp.abs(kg))
    reach = 2.0 * bound + c_first[:, :, None, :] - c_last[:, None, :, :]
    needed = (reach >= -EXP2_ZERO).reshape(bsz, nq, nq, PAIRS_FOX, 2).any(-1)
    earlier = jnp.arange(nq)[None, :] < jnp.arange(nq)[:, None]
    count = jnp.sum(needed & earlier[None, :, :, None], axis=2)
    first = jnp.arange(nq)[None, :, None] - count
    return jnp.transpose(first, (0, 2, 1)).reshape(-1).astype(jnp.int32)


def _fox_attn(first, q, k, v, ta):
    b, lp, d = v.shape
    grid_spec = pltpu.PrefetchScalarGridSpec(
        num_scalar_prefetch=1,
        grid=(b, PAIRS_FOX, lp // ta),
        in_specs=[pl.BlockSpec((None, ta, 2 * LANES), lambda i, h, t, f: (i, t, h)),
                  pl.BlockSpec((None, lp, 2 * LANES), lambda i, h, t, f: (i, 0, h)),
                  pl.BlockSpec((None, lp, LANES), lambda i, h, t, f: (i, 0, h))],
        out_specs=pl.BlockSpec((None, ta, LANES), lambda i, h, t, f: (i, t, h)),
        scratch_shapes=[pltpu.VMEM((2, ta, 1), F32),
                        pltpu.VMEM((2, ta, LANES), F32)],
    )
    return pl.pallas_call(
        functools.partial(_fox_attn_kernel, ta=ta),
        grid_spec=grid_spec,
        out_shape=jax.ShapeDtypeStruct((b, lp, d), BF16),
        compiler_params=pltpu.CompilerParams(
            dimension_semantics=("parallel", "parallel", "parallel"), vmem_limit_bytes=VMEM_LIMIT),
        name="fox_attention",
    )(first, q, k, v)


def _block_diag(blocks):
    g, r, c = blocks.shape
    eye = jnp.eye(g, dtype=blocks.dtype)
    return (eye[:, None, :, None] * blocks[:, :, None, :]).reshape(g * r, g * c)


def _cmul(ar, ai, br, bi):
    return ar * br - ai * bi, ar * bi + ai * br


def _s5_params(a_re, a_im, log_dt, b_re, b_im, c_re, c_im):
    dt = jnp.exp(log_dt)[:, None]
    mag = jnp.exp(a_re * dt)
    lam_re, lam_im = mag * jnp.cos(a_im * dt), mag * jnp.sin(a_im * dt)
    den = a_re * a_re + a_im * a_im
    z_re = ((lam_re - 1.0) * a_re + lam_im * a_im) / den
    z_im = (lam_im * a_re - (lam_re - 1.0) * a_im) / den
    bb_re = z_re[..., None] * b_re - z_im[..., None] * b_im
    bb_im = z_re[..., None] * b_im + z_im[..., None] * b_re
    to_in = lambda m: _block_diag(jnp.swapaxes(m, 1, 2))
    bm = jnp.concatenate([to_in(bb_re), to_in(bb_im)], axis=1).astype(BF16)
    to_out = lambda m: _block_diag(jnp.swapaxes(m, 1, 2))
    cm = jnp.concatenate([to_out(c_re), -to_out(c_im)], axis=0).astype(BF16)
    lr, li = lam_re.reshape(1, S5_LANES), lam_im.reshape(1, S5_LANES)
    n_lv = int(math.log2(S5_ROWS))
    pows, tr, ti = [], lr, li
    pr, pi = lr, li
    for _ in range(n_lv):
        pows += [pr, pi]
        nr, ni = _cmul(tr, ti, pr, pi)
        tr, ti = jnp.concatenate([tr, nr], axis=0), jnp.concatenate([ti, ni], axis=0)
        pr, pi = _cmul(pr, pi, pr, pi)
    pows = jnp.concatenate(pows + [jnp.zeros((16 - 2 * n_lv, S5_LANES), F32)], axis=0)
    return bm, cm, pows, jnp.stack([tr, ti])


def _rwkv_consts():
    idx = np.arange(LANES)
    hsum = (idx[:, None] // HEAD_DIM == idx[None, :] // HEAD_DIM).astype(np.float32)
    t = idx % CHUNK
    lv = [(t[:, None] // 2 == t[None, :] // 2)]
    b = 2
    while b < CHUNK:
        lv.append((t[:, None] // (2 * b) == t[None, :] // (2 * b)) & (t[:, None] // b != t[None, :] // b))
        b *= 2
    tri = (np.arange(CHUNK)[None, :] <= np.arange(CHUNK)[:, None]).astype(np.float32)
    return (jnp.asarray(hsum, BF16), jnp.asarray(tri, BF16), jnp.asarray(np.stack(lv).astype(np.float32)))


def _time_tile(length):
    return 768 if length > 2048 else 128


def kernel(x, meta_tokens, mix_norm, ffn_norm, even_w_in, even_w_out, rwkv_mu, rwkv_w0, rwkv_w2, rwkv_a0, rwkv_a2, rwkv_g2, rwkv_k_k, rwkv_k_a, rwkv_r_k, rwkv_ln_w, rwkv_ln_b, s5_a_re, s5_a_im, s5_log_dt, s5_b_re, s5_b_im, s5_c_re, s5_c_im, s5_d, s5_glu_w, s5_glu_b, odd_w_in, odd_w_out, fox_b_f, fox_q_gain, fox_k_gain, ffn_w_up, ffn_conv_w, ffn_conv_b, ffn_w_down):
    bsz, seq, d = x.shape
    length = N_META + seq
    tl = _time_tile(length)
    lp = -(-length // tl) * tl
    meta = jnp.broadcast_to(meta_tokens[None].astype(x.dtype), (bsz, N_META, d))
    h = jnp.concatenate([meta, x, jnp.zeros((bsz, lp - length, d), x.dtype)], axis=1)

    hsum, tri64, lvl = _rwkv_consts()
    n_ch = D_FF // FF_CHUNK

    def mix_ffn(h, xs, ws, layer):
        w_up = ffn_w_up[layer].astype(BF16)
        cw = ffn_conv_w[layer].reshape(3, 2, n_ch, FF_CHUNK).transpose(1, 2, 0, 3)
        cw = jnp.pad(cw, ((0, 0), (0, 0), (0, 5), (0, 0)))
        cb = ffn_conv_b[layer].reshape(2, n_ch, 1, FF_CHUNK)
        return _mix_ffn(h, xs, ws, ffn_norm[layer][None], w_up[:, :D_FF], w_up[:, D_FF:], cw, cb,
                        ffn_w_down[layer].astype(BF16), tl)

    p = _rms_matmul(h, mix_norm[0][None], even_w_in[0].astype(BF16), tl, BF16)
    pairs = lambda vec: vec.reshape(PAIRS_RWKV, 1, LANES)
    mu = rwkv_mu[0]
    prm = jnp.concatenate(
        [pairs(mu[:RWKV_DIM]), pairs(mu[RWKV_DIM:2 * RWKV_DIM]), pairs(mu[2 * RWKV_DIM:3 * RWKV_DIM]),
         pairs(rwkv_w0[0]), pairs(rwkv_a0[0]), pairs(rwkv_k_k[0]), pairs(rwkv_k_a[0]),
         pairs(rwkv_r_k[0].reshape(-1)), pairs(rwkv_ln_w[0]), pairs(rwkv_ln_b[0]),
         jnp.zeros((PAIRS_RWKV, 6, LANES), F32)], axis=1)
    zeros64 = jnp.zeros((64, RWKV_DIM), F32)
    w2p = jnp.concatenate([rwkv_w2[0], zeros64], axis=0).astype(BF16)
    a2p = jnp.concatenate([zeros64, rwkv_a2[0]], axis=0).astype(BF16)
    a_out = _rwkv(p, prm, mu[3 * RWKV_DIM:][None], w2p, a2p, rwkv_g2[0].astype(BF16),
                  hsum, tri64, lvl, tl, tl // CHUNK)
    bm, cm, pows, tab = _s5_params(s5_a_re[0], s5_a_im[0], s5_log_dt[0], s5_b_re[0], s5_b_im[0],
                                   s5_c_re[0], s5_c_im[0])
    b_out = _s5(p, bm, cm, pows, tab, s5_d[0][None], s5_glu_w[0].astype(BF16), s5_glu_b[0][None], tl)
    w_out = even_w_out[0].astype(BF16)
    h = mix_ffn(h, [a_out, b_out], [w_out[:RWKV_DIM], w_out[RWKV_DIM:]], 0)

    w_in = jnp.pad(odd_w_in[0], ((0, 0), (0, ODD_IN_PAD - odd_w_in.shape[-1]))).astype(BF16)
    scale = HEAD_DIM ** -0.5
    qg = jnp.tile(fox_q_gain[0], 2)[None] * (scale * LOG2E)
    kg = jnp.tile(fox_k_gain[0], 2)[None]
    bf = jnp.pad(fox_b_f[0], (0, LANES - FOX_HEADS))[None]
    tri_t = jnp.asarray(np.tril(np.ones((tl, tl), np.float32)), BF16)
    q, k, v, cb = _fox_in(h, mix_norm[1][None], w_in, qg, kg, bf, hsum * (1.0 / HEAD_DIM), tri_t,
                      *_fox_feature_consts(), tl)
    o = _fox_attn(_fox_first_block(cb, qg, kg), q, k, v, tl)
    h = mix_ffn(h, [o], [odd_w_out[0].astype(BF16)], 1)
    return h[:, N_META:N_META + seq]
```
